```python
import math
import jax, jax.numpy as jnp
from jax import lax
import numpy as np

D_MODEL = 1024
BATCH = 8
SEQ = 2048
DEPTH = 4
DEC_BATCH = 128
DEC_SEQ = 4
PAST_LEN = 16384
PAGE_SIZE = 128

D_MIX = D_MODEL
RET_H = 4
RET_DK = D_MIX // 16
RET_DV = D_MIX // 16
GDN_H = 4
GDN_DK = D_MIX // 8
GDN_DV = D_MIX // 8
GLA_H = 4
GLA_DK = D_MIX // 32
GLA_DV = D_MIX // 16
GLA_RANK = 16
GLA_GATE_NORM = 16.0
CONV_W = 4
GDN_CONV_DIM = GDN_H * (2 * GDN_DK + GDN_DV)
D_FF = ((8 * D_MODEL // 3 + 255) // 256) * 256
CHUNK = 64
ROPE_BASE = 10000.0
EPS = 1e-6
GN_EPS = 1e-5

IN_SPLITS = (RET_H * RET_DK, RET_H * RET_DK, RET_H * RET_DV, RET_H * RET_DV,
             GDN_H * GDN_DK, GDN_H * GDN_DK, GDN_H * GDN_DV, GDN_H, GDN_H, GDN_H * GDN_DV,
             GLA_H * GLA_DK, GLA_H * GLA_DK, GLA_H * GLA_DV, GLA_RANK, GLA_H * GLA_DV)
D_IN = sum(IN_SPLITS)

kernel_name = "hybrid_retention_gdn_gla_macaron_step"


def _split_cols(z):
    offs = np.cumsum(np.array(IN_SPLITS))[:-1]
    return jnp.split(z, [int(o) for o in offs], axis=-1)


def _rmsnorm(x, w):
    xf = x.astype(jnp.float32)
    y = xf * lax.rsqrt(jnp.mean(xf * xf, axis=-1, keepdims=True) + EPS)
    return (y * w.astype(jnp.float32)).astype(x.dtype)


def _rms_f32(x):
    return x * lax.rsqrt(jnp.mean(x * x, axis=-1, keepdims=True) + EPS)


def _l2norm(x):
    return x * lax.rsqrt(jnp.sum(x * x, axis=-1, keepdims=True) + EPS)


def _swiglu(x, w1, w3, w2):
    return (jax.nn.silu(x @ w1) * (x @ w3)) @ w2


def _rotary(x, pos):
    half = x.shape[-1] // 2
    inv = ROPE_BASE ** (-jnp.arange(half, dtype=jnp.float32) / half)
    ang = pos.astype(jnp.float32)[:, None] * inv[None, :]
    cos = jnp.cos(ang)[None, :, None, :]
    sin = jnp.sin(ang)[None, :, None, :]
    x1, x2 = x[..., :half], x[..., half:]
    return jnp.concatenate([x1 * cos - x2 * sin, x1 * sin + x2 * cos], axis=-1)


def _chunk_len(t):
    return math.gcd(t, CHUNK)


def _to_chunks(x, c):
    b, t = x.shape[:2]
    return jnp.swapaxes(x.reshape((b, t // c, c) + x.shape[2:]), 0, 1)


def _from_chunks(y):
    n, b, c = y.shape[:3]
    return jnp.swapaxes(y, 0, 1).reshape((b, n * c) + y.shape[3:])


def _retention_chunked(q, k, v, log_gamma, s0):
    c = _chunk_len(q.shape[1])
    idx = jnp.arange(c, dtype=jnp.float32)
    diff = idx[:, None] - idx[None, :]
    causal = diff >= 0
    dec_intra = jnp.where(causal[None], jnp.exp(jnp.where(causal, diff, 0.0)[None] * log_gamma[:, None, None]), 0.0)
    dec_q = jnp.exp((idx + 1.0)[:, None] * log_gamma[None, :])[None, :, :, None]
    dec_k = jnp.exp((c - 1.0 - idx)[:, None] * log_gamma[None, :])[None, :, :, None]
    dec_chunk = jnp.exp(c * log_gamma)[None, :, None, None]

    def step(s, inp):
        qc, kc, vc = inp
        scores = jnp.einsum('bihd,bjhd->bhij', qc, kc) * dec_intra[None]
        o = jnp.einsum('bhij,bjhv->bihv', scores, vc) + jnp.einsum('bihd,bhdv->bihv', qc, s) * dec_q
        s = s * dec_chunk + jnp.einsum('bjhd,bjhv->bhdv', kc * dec_k, vc)
        return s, o

    s, o = lax.scan(step, s0, (_to_chunks(q, c), _to_chunks(k, c), _to_chunks(v, c)))
    return _from_chunks(o), s


def _gated_delta_chunked(q, k, v, g, beta, s0):
    c = _chunk_len(q.shape[1])
    dv = v.shape[-1]
    idx = jnp.arange(c)
    causal = idx[:, None] >= idx[None, :]
    strict = idx[:, None] > idx[None, :]
    eye = jnp.eye(c, dtype=jnp.float32)

    def step(s, inp):
        qc, kc, vc, gc, bc = inp
        b = jnp.swapaxes(jnp.cumsum(gc, axis=1), 1, 2)
        beta_h = jnp.swapaxes(bc, 1, 2)
        diff = b[..., :, None] - b[..., None, :]
        dec = jnp.where(causal, jnp.exp(jnp.where(causal, diff, 0.0)), 0.0)
        kk = jnp.einsum('bihd,bjhd->bhij', kc, kc)
        a_mat = eye + jnp.where(strict, kk * dec, 0.0) * beta_h[..., :, None]
        k_h = jnp.swapaxes(kc, 1, 2)
        v_h = jnp.swapaxes(vc, 1, 2)
        rhs = jnp.concatenate([v_h, k_h * jnp.exp(b)[..., None]], axis=-1) * beta_h[..., None]
        sol = lax.linalg.triangular_solve(a_mat, rhs, left_side=True, lower=True)
        u = sol[..., :dv] - jnp.einsum('bhik,bhkv->bhiv', sol[..., dv:], s)
        qk = jnp.einsum('bihd,bjhd->bhij', qc, kc) * dec
        o = jnp.exp(b)[..., None] * jnp.einsum('bihk,bhkv->bhiv', qc, s) + jnp.einsum('bhij,bhjv->bhiv', qk, u)
        b_last = b[..., -1:]
        s = jnp.exp(b_last)[..., None] * s + jnp.einsum('bhjk,bhjv->bhkv', k_h * jnp.exp(b_last - b)[..., None], u)
        return s, jnp.swapaxes(o, 1, 2)

    s, o = lax.scan(step, s0, (_to_chunks(q, c), _to_chunks(k, c), _to_chunks(v, c),
                               _to_chunks(g, c), _to_chunks(beta, c)))
    return _from_chunks(o), s


def _gla_chunked(q, k, v, g, s0):
    c = _chunk_len(q.shape[1])
    idx = jnp.arange(c)
    m = (idx[:, None] >= idx[None, :])[None, :, :, None, None]

    def step(s, inp):
        qc, kc, vc, gc = inp
        b = jnp.cumsum(gc, axis=1)
        diff = b[:, :, None] - b[:, None, :]
        dec = jnp.where(m, jnp.exp(jnp.where(m, diff, 0.0)), 0.0)
        scores = jnp.einsum('bihd,bjhd,bijhd->bhij', qc, kc, dec)
        o = jnp.einsum('bhij,bjhv->bihv', scores, vc) + jnp.einsum('bihk,bhkv->bihv', qc * jnp.exp(b), s)
        b_last = b[:, -1:]
        s = jnp.exp(b_last[:, 0])[..., None] * s + jnp.einsum('bjhk,bjhv->bhkv', kc * jnp.exp(b_last - b), vc)
        return s, o

    s, o = lax.scan(step, s0, (_to_chunks(q, c), _to_chunks(k, c), _to_chunks(v, c), _to_chunks(g, c)))
    return _from_chunks(o), s


def _short_conv(xc, buf, w):
    t = xc.shape[1]
    xp = jnp.concatenate([buf, xc], axis=1)
    out = xp[:, 0:t] * w[0]
    for i in range(1, CONV_W):
        out = out + xp[:, i:i + t] * w[i]
    return jax.nn.silu(out), xp[:, t:]


def _mixer(h, pos, s_ret, s_gdn, conv_buf, s_gla, w_in, ret_gn_w, ret_gn_b, gdn_conv_w, gdn_A_log,
           gdn_dt_bias, gdn_norm_w, gla_gate_w, gla_gate_b, gla_norm_w, w_out):
    f32 = jnp.float32
    bsz, t, _ = h.shape
    z = (h @ w_in).astype(f32)
    rq, rk, rv, rg, dq, dk, dv, da, db, dg, lq, lk, lv, llr, lg = _split_cols(z)

    def heads(a, n):
        return a.reshape(bsz, t, n, -1)

    log_gamma = jnp.log(1.0 - 2.0 ** (-5.0 - jnp.arange(RET_H, dtype=f32)))
    qr = _rotary(heads(rq, RET_H), pos)
    kr = _rotary(heads(rk, RET_H), pos) * (RET_DK ** -0.5)
    o_r, s_ret = _retention_chunked(qr, kr, heads(rv, RET_H), log_gamma, s_ret.astype(f32))
    mu = jnp.mean(o_r, axis=-1, keepdims=True)
    var = jnp.mean(jnp.square(o_r - mu), axis=-1, keepdims=True)
    o_r = ((o_r - mu) * lax.rsqrt(var + GN_EPS)).reshape(bsz, t, -1) * ret_gn_w + ret_gn_b
    y_r = jax.nn.silu(rg) * o_r

    qkv, conv_buf = _short_conv(jnp.concatenate([dq, dk, dv], axis=-1), conv_buf.astype(f32), gdn_conv_w)
    cq, ck, cv = jnp.split(qkv, [GDN_H * GDN_DK, 2 * GDN_H * GDN_DK], axis=-1)
    qd = _l2norm(heads(cq, GDN_H)) * (GDN_DK ** -0.5)
    kd = _l2norm(heads(ck, GDN_H))
    gd = -jnp.exp(gdn_A_log.astype(f32)) * jax.nn.softplus(da + gdn_dt_bias)
    beta = jax.nn.sigmoid(db)
    o_d, s_gdn = _gated_delta_chunked(qd, kd, heads(cv, GDN_H), gd, beta, s_gdn.astype(f32))
    y_d = (_rms_f32(o_d) * gdn_norm_w).reshape(bsz, t, -1) * jax.nn.silu(dg)

    gk = jax.nn.log_sigmoid(llr @ gla_gate_w + gla_gate_b) / GLA_GATE_NORM
    o_l, s_gla = _gla_chunked(heads(lq, GLA_H) * (GLA_DK ** -0.5), heads(lk, GLA_H), heads(lv, GLA_H),
                              heads(gk, GLA_H), s_gla.astype(f32))
    y_l = (_rms_f32(o_l) * gla_norm_w).reshape(bsz, t, -1) * jax.nn.silu(lg)

    y = jnp.concatenate([y_r, y_d, y_l], axis=-1).astype(h.dtype) @ w_out
    return y, s_ret, s_gdn, conv_buf, s_gla


def setup_inputs(seed: int = 0) -> dict:
    key = jax.random.key(seed)
    ks = jax.random.split(key, 32)
    f32 = jnp.float32

    def nrm(k, shape, scale):
        return scale * jax.random.normal(k, shape, f32)

    def gain(k, shape):
        return 1.0 + nrm(k, shape, 0.02)

    dt = jnp.exp(jax.random.uniform(ks[16], (DEPTH, GDN_H), f32, math.log(1e-3), math.log(1e-1)))
    return {
        "x_prompt": nrm(ks[0], (BATCH, SEQ, D_MODEL), 1.0),
        "x_sample": nrm(ks[1], (DEC_BATCH, DEC_SEQ, D_MODEL), 1.0),
        "state_ret": nrm(ks[2], (DEPTH, DEC_BATCH, RET_H, RET_DK, RET_DV), 0.5),
        "state_gdn": nrm(ks[3], (DEPTH, DEC_BATCH, GDN_H, GDN_DK, GDN_DV), 0.1),
        "state_gdn_conv": nrm(ks[4], (DEPTH, DEC_BATCH, CONV_W - 1, GDN_CONV_DIM), 1.0),
        "state_gla": nrm(ks[5], (DEPTH, DEC_BATCH, GLA_H, GLA_DK, GLA_DV), 0.5),
        "norm_ffn1": gain(ks[6], (DEPTH, D_MODEL)),
        "ffn1_w1": nrm(ks[7], (DEPTH, D_MODEL, D_FF), D_MODEL ** -0.5),
        "ffn1_w3": nrm(ks[8], (DEPTH, D_MODEL, D_FF), D_MODEL ** -0.5),
        "ffn1_w2": nrm(ks[9], (DEPTH, D_FF, D_MODEL), D_FF ** -0.5),
        "norm_mix": gain(ks[10], (DEPTH, D_MODEL)),
        "w_in": nrm(ks[11], (DEPTH, D_MODEL, D_IN), D_MODEL ** -0.5),
        "ret_gn_w": gain(ks[12], (DEPTH, RET_H * RET_DV)),
        "ret_gn_b": nrm(ks[13], (DEPTH, RET_H * RET_DV), 0.02),
        "gdn_conv_w": nrm(ks[14], (DEPTH, CONV_W, GDN_CONV_DIM), CONV_W ** -0.5),
        "gdn_A_log": jnp.log(jax.random.uniform(ks[15], (DEPTH, GDN_H), f32, 1.0, 16.0)),
        "gdn_dt_bias": dt + jnp.log(-jnp.expm1(-dt)),
        "gdn_norm_w": gain(ks[17], (DEPTH, GDN_DV)),
        "gla_gate_w": nrm(ks[18], (DEPTH, GLA_RANK, GLA_H * GLA_DK), GLA_RANK ** -0.5),
        "gla_gate_b": nrm(ks[19], (DEPTH, GLA_H * GLA_DK), 0.1),
        "gla_norm_w": gain(ks[20], (DEPTH, GLA_DV)),
        "w_out": nrm(ks[21], (DEPTH, D_MIX, D_MODEL), D_MIX ** -0.5),
        "norm_ffn2": gain(ks[22], (DEPTH, D_MODEL)),
        "ffn2_w1": nrm(ks[23], (DEPTH, D_MODEL, D_FF), D_MODEL ** -0.5),
        "ffn2_w3": nrm(ks[24], (DEPTH, D_MODEL, D_FF), D_MODEL ** -0.5),
        "ffn2_w2": nrm(ks[25], (DEPTH, D_FF, D_MODEL), D_FF ** -0.5),
        "norm_final": gain(ks[26], (D_MODEL,)),
    }


def reference(x_prompt, x_sample, state_ret, state_gdn, state_gdn_conv, state_gla,
              norm_ffn1, ffn1_w1, ffn1_w3, ffn1_w2, norm_mix, w_in, ret_gn_w, ret_gn_b,
              gdn_conv_w, gdn_A_log, gdn_dt_bias, gdn_norm_w, gla_gate_w, gla_gate_b, gla_norm_w,
              w_out, norm_ffn2, ffn2_w1, ffn2_w3, ffn2_w2, norm_final):
    f32 = jnp.float32

    def run_layer(x, pos, sr, sd, sc, sl, l):
        h = _rmsnorm(x, norm_ffn1[l])
        x = x + 0.5 * _swiglu(h, ffn1_w1[l], ffn1_w3[l], ffn1_w2[l])
        h = _rmsnorm(x, norm_mix[l])
        y, sr, sd, sc, sl = _mixer(h, pos, sr, sd, sc, sl, w_in[l], ret_gn_w[l], ret_gn_b[l],
                                   gdn_conv_w[l], gdn_A_log[l], gdn_dt_bias[l], gdn_norm_w[l],
                                   gla_gate_w[l], gla_gate_b[l], gla_norm_w[l], w_out[l])
        x = x + y
        h = _rmsnorm(x, norm_ffn2[l])
        x = x + 0.5 * _swiglu(h, ffn2_w1[l], ffn2_w3[l], ffn2_w2[l])
        return x, sr, sd, sc, sl

    bp, tp = x_prompt.shape[:2]
    ts = x_sample.shape[1]
    pos_p = jnp.arange(tp, dtype=jnp.int32)
    pos_s = PAST_LEN + jnp.arange(ts, dtype=jnp.int32)

    xp, xs = x_prompt, x_sample
    p_ret, p_gdn, p_conv, p_gla = [], [], [], []
    s_ret, s_gdn, s_conv, s_gla = [], [], [], []
    for l in range(DEPTH):
        xp, a, b, c, d = run_layer(xp, pos_p,
                                   jnp.zeros((bp, RET_H, RET_DK, RET_DV), f32),
                                   jnp.zeros((bp, GDN_H, GDN_DK, GDN_DV), f32),
                                   jnp.zeros((bp, CONV_W - 1, GDN_CONV_DIM), f32),
                                   jnp.zeros((bp, GLA_H, GLA_DK, GLA_DV), f32), l)
        p_ret.append(a); p_gdn.append(b); p_conv.append(c); p_gla.append(d)
        xs, a, b, c, d = run_layer(xs, pos_s, state_ret[l], state_gdn[l], state_gdn_conv[l], state_gla[l], l)
        s_ret.append(a); s_gdn.append(b); s_conv.append(c); s_gla.append(d)

    y_prompt = _rmsnorm(xp, norm_final)
    y_sample = _rmsnorm(xs, norm_final)
    new_ret_p = jnp.stack(p_ret).astype(state_ret.dtype)
    new_gdn_p = jnp.stack(p_gdn).astype(state_gdn.dtype)
    new_conv_p = jnp.stack(p_conv).astype(state_gdn_conv.dtype)
    new_gla_p = jnp.stack(p_gla).astype(state_gla.dtype)
    new_ret_s = jnp.stack(s_ret).astype(state_ret.dtype)
    new_gdn_s = jnp.stack(s_gdn).astype(state_gdn.dtype)
    new_conv_s = jnp.stack(s_conv).astype(state_gdn_conv.dtype)
    new_gla_s = jnp.stack(s_gla).astype(state_gla.dtype)
    return (y_prompt, y_sample, new_ret_p, new_gdn_p, new_conv_p, new_gla_p,
            new_ret_s, new_gdn_s, new_conv_s, new_gla_s)
```

```python
import functools
import math

import numpy as np
import jax
import jax.numpy as jnp
from jax import lax
from jax.experimental import pallas as pl
from jax.experimental.pallas import tpu as pltpu

F32 = jnp.float32
BF16 = jnp.bfloat16
HI = lax.Precision.HIGHEST

D_MODEL = 1024
DEPTH = 4
PAST_LEN = 16384
NH = 4
RET_DK = 64
RET_DV = 64
GDN_DK = 128
GDN_DV = 128
GLA_DK = 32
GLA_DV = 64
GLA_RANK = 16
GLA_GATE_NORM = 16.0
CONV_W = 4
CONV_DIM = NH * (2 * GDN_DK + GDN_DV)
D_FF = 2816
ROPE_BASE = 10000.0
EPS = 1e-6
GN_EPS = 1e-5

C = 64
R = NH * C
LANES = 128

Z_RQ, Z_RK, Z_RV, Z_RG = 0, 256, 512, 768
Z_CONV = 1024
Z_DG = 2560
Z_LQ, Z_LK, Z_LV, Z_LG = 3072, 3200, 3328, 3584
Z_SM = 3840
NZ = 3968
SM_DA, SM_DB, SM_LLR = 0, 4, 8

NN = (((1,), (0,)), ((), ()))
NT = (((1,), (1,)), ((), ()))
TN = (((0,), (0,)), ((), ()))

VMEM_LIMIT = 56 * 1024 * 1024


def _mm(a, b, dims=NN, hi=False):
    if hi:
        return lax.dot_general(a, b, dims, precision=HI, preferred_element_type=F32)
    return lax.dot_general(a.astype(BF16), b.astype(BF16), dims, preferred_element_type=F32)


def _silu(x):
    return x * jax.nn.sigmoid(x)


def _softplus(x):
    return jnp.maximum(x, 0.0) + jnp.log1p(jnp.exp(-jnp.abs(x)))


def _log_sigmoid(x):
    return jnp.minimum(x, 0.0) - jnp.log1p(jnp.exp(-jnp.abs(x)))


def _rms_rows(x, w):
    ms = jnp.mean(x * x, axis=-1, keepdims=True)
    return x * lax.rsqrt(ms + EPS) * w


def _stack_heads(x, width):
    lane = lax.broadcasted_iota(jnp.int32, x.shape, 1) // width
    return jnp.concatenate([jnp.where(lane == h, x, 0.0) for h in range(NH)], axis=0)


def _unstack_heads(xs, width):
    lane = lax.broadcasted_iota(jnp.int32, (C, xs.shape[1]), 1) // width
    out = jnp.zeros((C, xs.shape[1]), F32)
    for h in range(NH):
        out = jnp.where(lane == h, xs[h * C:(h + 1) * C], out)
    return out


def _ffn_kernel(x_ref, nw_ref, w1_ref, w3_ref, w2_ref, fw_ref, o_ref, h_ref, acc_ref, *, final):
    j = pl.program_id(1)

    @pl.when(j == 0)
    def _():
        h_ref[...] = _rms_rows(x_ref[...], nw_ref[...]).astype(BF16)
        acc_ref[...] = jnp.zeros_like(acc_ref)

    h = h_ref[...]
    a = jnp.dot(h, w1_ref[...], preferred_element_type=F32)
    g = jnp.dot(h, w3_ref[...], preferred_element_type=F32)
    acc_ref[...] += jnp.dot((_silu(a) * g).astype(BF16), w2_ref[...], preferred_element_type=F32)

    @pl.when(j == pl.num_programs(1) - 1)
    def _():
        y = x_ref[...] + 0.5 * acc_ref[...]
        if final:
            y = _rms_rows(y, fw_ref[...])
        o_ref[...] = y


def _ffn(x, nw, w1, w3, w2, fw, *, final, tm=512, tf=256):
    n = x.shape[0]
    grid = (n // tm, D_FF // tf)
    return pl.pallas_call(
        functools.partial(_ffn_kernel, final=final),
        grid=grid,
        in_specs=[
            pl.BlockSpec((tm, D_MODEL), lambda i, j: (i, 0)),
            pl.BlockSpec((1, D_MODEL), lambda i, j: (0, 0)),
            pl.BlockSpec((D_MODEL, tf), lambda i, j: (0, j)),
            pl.BlockSpec((D_MODEL, tf), lambda i, j: (0, j)),
            pl.BlockSpec((tf, D_MODEL), lambda i, j: (j, 0)),
            pl.BlockSpec((1, D_MODEL), lambda i, j: (0, 0)),
        ],
        out_specs=pl.BlockSpec((tm, D_MODEL), lambda i, j: (i, 0)),
        out_shape=jax.ShapeDtypeStruct((n, D_MODEL), F32),
        scratch_shapes=[pltpu.VMEM((tm, D_MODEL), BF16), pltpu.VMEM((tm, D_MODEL), F32)],
        compiler_params=pltpu.CompilerParams(
            dimension_semantics=("parallel", "arbitrary"), vmem_limit_bytes=VMEM_LIMIT),
        name="ffn",
    )(x, nw, w1, w3, w2, fw)


def _levels(L):
    return [s for s in (1, 2, 4, 8, 16, 32) if s < L]


def _chunk_consts(L):
    i = np.arange(C)
    sid, p = i // L, i % L
    same = sid[:, None] == sid[None, :]
    causal = same & (i[:, None] >= i[None, :])
    strict = same & (i[:, None] > i[None, :])
    lv = _levels(L)
    mall, pm = [], [np.tile(np.eye(C, dtype=bool), (NH, 1))]
    for s in lv:
        blk = p // s
        inblk = same & (blk[:, None] == blk[None, :])
        mall.append((blk % 2 == 1)[:, None] & inblk & (i[None, :] <= i[:, None]))
        mall.append((blk % 2 == 0)[:, None] & inblk & (i[None, :] > i[:, None]))
        pm.append(np.tile(same & (blk % 2 == 1)[:, None] & (blk[None, :] == blk[:, None] - 1), (NH, 1)))
    r = np.arange(R)
    bdfull = (r[:, None] // C) == (r[None, :] // C)
    tile2 = lambda m: np.tile(m, (NH, NH))
    bdc = bdfull & tile2(causal)
    bds = bdfull & tile2(strict)
    pr = np.tile(p, NH)
    lm = [bds & ((pr[:, None] // (2 * s)) == (pr[None, :] // (2 * s))) & ((pr[:, None] // s) != (pr[None, :] // s))
          for s in lv]
    f = lambda m: jnp.asarray(np.asarray(m, dtype=np.float32))
    gla_bd = (np.arange(NH * GLA_DK)[:, None] // GLA_DK) == (np.arange(NH * GLA_DV)[None, :] // GLA_DV)
    return dict(
        lt=f(causal), same=f(same), mall=f(np.concatenate(mall, axis=0)), pm=f(np.stack(pm)),
        bdfull=f(bdfull), bdc=f(bdc), bds=f(bds), lm=f(np.stack(lm)), glabd=f(gla_bd),
    )


def _ret_consts(L):
    log_gamma = jnp.log(1.0 - 2.0 ** (-5.0 - jnp.arange(NH, dtype=F32)))
    i = np.arange(C)
    sid, p = i // L, (i % L).astype(np.float32)
    causal = (sid[:, None] == sid[None, :]) & (i[:, None] >= i[None, :])
    diff = jnp.asarray(np.where(causal, p[:, None] - p[None, :], 0.0).astype(np.float32))
    dec = jnp.where(causal[None], jnp.exp(diff[None] * log_gamma[:, None, None]), 0.0)
    lg_l = jnp.repeat(log_gamma, RET_DV)[None, :]
    pj = jnp.asarray(p)[:, None]
    return dict(
        decs=dec.reshape(R, C),
        dq=jnp.exp((pj + 1.0) * lg_l),
        dk=jnp.exp((L - 1.0 - pj) * lg_l),
        dch=jnp.exp(L * lg_l),
    )


def _rope_tables(pos):
    half = RET_DK // 2
    inv = ROPE_BASE ** (-jnp.arange(half, dtype=F32) / half)
    ang = pos.astype(F32)[:, None] * inv[None, :]
    cos, sin = jnp.cos(ang), jnp.sin(ang)
    return (jnp.tile(jnp.concatenate([cos, cos], axis=1), (1, NH)),
            jnp.tile(jnp.concatenate([-sin, sin], axis=1), (1, NH)))


def _rotary(x, cos, sin):
    lane = lax.broadcasted_iota(jnp.int32, x.shape, 1)
    swapped = jnp.where((lane % RET_DK) < RET_DK // 2,
                        pltpu.roll(x, x.shape[1] - RET_DK // 2, 1), pltpu.roll(x, RET_DK // 2, 1))
    return x * cos + swapped * sin


def _ret_intra(z, cos, sin, k):
    q = _rotary(z[:, Z_RQ:Z_RQ + 256], cos, sin)
    kk = _rotary(z[:, Z_RK:Z_RK + 256], cos, sin) * (RET_DK ** -0.5)
    v = z[:, Z_RV:Z_RV + 256]
    sc = _mm(_stack_heads(q, RET_DK), kk, NT) * k["decs"][...]
    o = _unstack_heads(_mm(sc, v), RET_DV)
    return o, q * k["dq"][...], kk * k["dk"][...], v


def _ret_out(o, rg, k):
    g = k["bdfull"][...] * (1.0 / RET_DV)
    mu = _mm(o, g, hi=True)
    d = o - mu
    var = _mm(d * d, g, hi=True)
    return _silu(rg) * (d * lax.rsqrt(var + GN_EPS) * k["ret_gn_w"][...] + k["ret_gn_b"][...])


def _gla_intra(z, k, levels):
    sm = z[:, Z_SM:Z_SM + LANES]
    gk = _log_sigmoid(_mm(sm, k["gwp"][...], hi=True) + k["gla_gate_b"][...]) * (1.0 / GLA_GATE_NORM)
    q = z[:, Z_LQ:Z_LQ + 128] * (GLA_DK ** -0.5)
    kk = z[:, Z_LK:Z_LK + 128]
    v = z[:, Z_LV:Z_LV + 256]
    b = _mm(k["lt"][...], gk, hi=True)
    blast = _mm(k["same"][...], gk, hi=True)
    sc = _mm(_stack_heads(q, GLA_DK), kk, NT) * k["pm"][0]
    if levels:
        ex = jnp.exp(_mm(k["mall"][...], gk, hi=True))
        for li in range(len(levels)):
            qs = q * ex[(2 * li) * C:(2 * li + 1) * C]
            ks = kk * ex[(2 * li + 1) * C:(2 * li + 2) * C]
            sc = sc + _mm(_stack_heads(qs, GLA_DK), ks, NT) * k["pm"][li + 1]
    o = _unstack_heads(_mm(sc, v), GLA_DV)
    return o, q * jnp.exp(b), kk * jnp.exp(blast - b), v, gk


def _gla_out(o, lg, k):
    ms = _mm(o * o, k["bdfull"][...] * (1.0 / GLA_DV), hi=True)
    return o * lax.rsqrt(ms + EPS) * k["gla_norm_w"][...] * _silu(lg)


def _lane_col(blk, lane0):
    lane = lax.broadcasted_iota(jnp.int32, blk.shape, 1)
    st = jnp.concatenate([jnp.where(lane == lane0 + h, blk, 0.0) for h in range(NH)], axis=0)
    return st, jnp.sum(st, axis=-1, keepdims=True)


def _gdn_intra(z, k, levels):
    conv = z[:, Z_CONV:Z_CONV + CONV_DIM]
    sm = z[:, Z_SM:Z_SM + LANES]

    def l2n(x):
        return x * lax.rsqrt(jnp.sum(x * x, axis=-1, keepdims=True) + EPS)

    qst = jnp.concatenate([l2n(conv[:, h * 128:(h + 1) * 128]) for h in range(NH)], axis=0) * (GDN_DK ** -0.5)
    kst = jnp.concatenate([l2n(conv[:, 512 + h * 128:512 + (h + 1) * 128]) for h in range(NH)], axis=0)
    vst = jnp.concatenate([conv[:, 1024 + h * 128:1024 + (h + 1) * 128] for h in range(NH)], axis=0)
    gd = -jnp.exp(k["alog"][...]) * _softplus(sm + k["dtb"][...])
    beta = jax.nn.sigmoid(sm)
    bst, bcol = _lane_col(_mm(k["lt"][...], gd, hi=True), SM_DA)
    _, blcol = _lane_col(_mm(k["same"][...], gd, hi=True), SM_DA)
    _, betacol = _lane_col(beta, SM_DB)
    brow = _mm(jnp.ones((8, LANES), F32), bst, NT, hi=True)[0:1]
    bdc = k["bdc"][...] > 0.5
    dec = jnp.where(bdc, jnp.exp(jnp.where(bdc, bcol - brow, 0.0)), 0.0)
    a = _mm(kst, kst, NT) * dec * k["bds"][...] * betacol
    row = lax.broadcasted_iota(jnp.int32, (R, R), 0)
    coli = lax.broadcasted_iota(jnp.int32, (R, R), 1)
    x = jnp.where(row == coli, 1.0, 0.0)
    for li in range(len(levels)):
        m = a * k["lm"][li]
        x = x - m if li == 0 else x - _mm(x, _mm(m, x))
    ebcol = jnp.exp(bcol)
    rhs = jnp.concatenate([vst, kst * ebcol], axis=1) * betacol
    sol = _mm(x, rhs)
    qk = _mm(qst, kst, NT) * dec
    kdec = kst * jnp.exp(blcol - bcol)
    return dict(qst=qst, solv=sol[:, :GDN_DV], solk=sol[:, GDN_DV:], qk=qk, ebcol=ebcol, kdec=kdec,
                eblast=jnp.exp(blcol))


def _gdn_out(ost, dg, k):
    yst = _rms_rows(ost, k["gdn_norm_w"][...])
    y = jnp.concatenate([yst[h * C:(h + 1) * C] for h in range(NH)], axis=1)
    return y * _silu(dg)


_PARAM_NAMES = ("alog", "dtb", "gwp", "gla_gate_b", "ret_gn_w", "ret_gn_b", "gdn_norm_w", "gla_norm_w", "conv_w")
_CONST_NAMES = ("lt", "same", "mall", "pm", "bdfull", "bdc", "bds", "lm", "glabd", "decs", "dq", "dk", "dch")


def _mixer_prompt_kernel(*refs, tc, levels):
    n_in = 6 + len(_PARAM_NAMES) + len(_CONST_NAMES)
    x_ref, nw_ref, win_ref, wout_ref, cos_ref, sin_ref = refs[:6]
    k = dict(zip(_PARAM_NAMES + _CONST_NAMES, refs[6:n_in]))
    o_ref, sret_ref, sgdn_ref, sgla_ref, conv_ref = refs[n_in:n_in + 5]
    z_ref, xp_ref, y_ref = refs[n_in + 5:]
    t = pl.program_id(1)

    @pl.when(t == 0)
    def _():
        sret_ref[...] = jnp.zeros_like(sret_ref)
        sgdn_ref[...] = jnp.zeros_like(sgdn_ref)
        sgla_ref[...] = jnp.zeros_like(sgla_ref)
        xp_ref[pl.ds(0, 8), :] = jnp.zeros((8, CONV_DIM), F32)

    h = _rms_rows(x_ref[0], nw_ref[...]).astype(BF16)
    z_ref[...] = jnp.dot(h, win_ref[...], preferred_element_type=F32)

    xp_ref[pl.ds(8, tc), :] = z_ref[:, Z_CONV:Z_CONV + CONV_DIM]
    cw = k["conv_w"]
    acc = xp_ref[pl.ds(8, tc), :] * cw[3:4, :]
    for i in range(CONV_W - 1):
        acc = acc + xp_ref[pl.ds(5 + i, tc), :] * cw[i:i + 1, :]
    z_ref[:, Z_CONV:Z_CONV + CONV_DIM] = _silu(acc)
    xp_ref[pl.ds(0, 8), :] = xp_ref[pl.ds(tc, 8), :]

    for c in range(tc // C):
        z = z_ref[pl.ds(c * C, C), :]
        cos = cos_ref[pl.ds(c * C, C), :]
        sin = sin_ref[pl.ds(c * C, C), :]

        o, qd, kd, v = _ret_intra(z, cos, sin, k)
        s = sret_ref[0]
        o = o + _mm(qd, s)
        sret_ref[0] = s * k["dch"][...] + _mm(kd, v, TN) * k["bdfull"][...]
        y_r = _ret_out(o, z[:, Z_RG:Z_RG + 256], k)

        g = _gdn_intra(z, k, levels)
        us, qss = [], []
        for hh in range(NH):
            lhs = jnp.concatenate([g["solk"][hh * C:(hh + 1) * C], g["qst"][hh * C:(hh + 1) * C]], axis=0)
            r = _mm(lhs, sgdn_ref[0, hh])
            us.append(g["solv"][hh * C:(hh + 1) * C] - r[:C])
            qss.append(r[C:])
        ust = jnp.concatenate(us, axis=0)
        ost = g["ebcol"] * jnp.concatenate(qss, axis=0) + _mm(g["qk"], ust)
        for hh in range(NH):
            sl = slice(hh * C, (hh + 1) * C)
            sgdn_ref[0, hh] = (g["eblast"][hh * C:hh * C + 1] * sgdn_ref[0, hh]
                               + _mm(g["kdec"][sl], us[hh], TN))
        y_d = _gdn_out(ost, z[:, Z_DG:Z_DG + 512], k)

        o, qe, ke, v, gk = _gla_intra(z, k, levels)
        s = sgla_ref[0]
        o = o + _mm(qe, s)
        escale = jnp.exp(_mm(gk, jnp.ones((C, NH * GLA_DV), F32), TN, hi=True))
        sgla_ref[0] = s * escale + _mm(ke, v, TN) * k["glabd"][...]
        y_l = _gla_out(o, z[:, Z_LG:Z_LG + 256], k)

        y_ref[pl.ds(c * C, C), :] = jnp.concatenate([y_r, y_d, y_l], axis=1).astype(BF16)

    o_ref[0] = x_ref[0] + jnp.dot(y_ref[...], wout_ref[...], preferred_element_type=F32)

    @pl.when(t == pl.num_programs(1) - 1)
    def _():
        conv_ref[0] = xp_ref[pl.ds(5, 3), :]


def _const_spec(a):
    nd = a.ndim
    return pl.BlockSpec(a.shape, lambda b, t, _n=nd: (0,) * _n)


def _mixer_prompt(x, nw, win, wout, cos, sin, params, consts, *, tc=256):
    bsz, tlen, _ = x.shape
    levels = _levels(C)
    extras = [params[n] for n in _PARAM_NAMES] + [consts[n] for n in _CONST_NAMES]
    in_specs = [
        pl.BlockSpec((1, tc, D_MODEL), lambda b, t: (b, t, 0)),
        _const_spec(nw), _const_spec(win), _const_spec(wout),
        pl.BlockSpec((tc, 256), lambda b, t: (t, 0)),
        pl.BlockSpec((tc, 256), lambda b, t: (t, 0)),
    ] + [_const_spec(a) for a in extras]
    out_shape = (
        jax.ShapeDtypeStruct((bsz, tlen, D_MODEL), F32),
        jax.ShapeDtypeStruct((bsz, 256, 256), F32),
        jax.ShapeDtypeStruct((bsz, NH, GDN_DK, GDN_DV), F32),
        jax.ShapeDtypeStruct((bsz, NH * GLA_DK, NH * GLA_DV), F32),
        jax.ShapeDtypeStruct((bsz, CONV_W - 1, CONV_DIM), F32),
    )
    out_specs = (
        pl.BlockSpec((1, tc, D_MODEL), lambda b, t: (b, t, 0)),
        pl.BlockSpec((1, 256, 256), lambda b, t: (b, 0, 0)),
        pl.BlockSpec((1, NH, GDN_DK, GDN_DV), lambda b, t: (b, 0, 0, 0)),
        pl.BlockSpec((1, NH * GLA_DK, NH * GLA_DV), lambda b, t: (b, 0, 0)),
        pl.BlockSpec((1, CONV_W - 1, CONV_DIM), lambda b, t: (b, 0, 0)),
    )
    return pl.pallas_call(
        functools.partial(_mixer_prompt_kernel, tc=tc, levels=levels),
        grid=(bsz, tlen // tc),
        in_specs=in_specs,
        out_specs=out_specs,
        out_shape=out_shape,
        scratch_shapes=[
            pltpu.VMEM((tc, NZ), F32),
            pltpu.VMEM((tc + 8, CONV_DIM), F32),
            pltpu.VMEM((tc, D_MODEL), BF16),
        ],
        compiler_params=pltpu.CompilerParams(
            dimension_semantics=("parallel", "arbitrary"), vmem_limit_bytes=VMEM_LIMIT),
        name="mixer_prompt",
    )(x, nw, win, wout, cos, sin, *extras)


SEQ_S = 4
NSEQ = C // SEQ_S


def _mixer_sample_kernel(*refs, levels):
    n_in = 10 + len(_PARAM_NAMES) + len(_CONST_NAMES)
    (x_ref, nw_ref, win_ref, wout_ref, cos_ref, sin_ref,
     sret_in, sgdn_in, sgla_in, cbuf_ref) = refs[:10]
    k = dict(zip(_PARAM_NAMES + _CONST_NAMES, refs[10:n_in]))
    o_ref, sret_out, sgdn_out, sgla_out, cout_ref = refs[n_in:n_in + 5]

    x = x_ref[...]
    h = _rms_rows(x, nw_ref[...]).astype(BF16)
    z = jnp.dot(h, win_ref[...], preferred_element_type=F32)

    rowi = lax.broadcasted_iota(jnp.int32, (C, 1), 0)
    tpos = rowi % SEQ_S

    xc = z[:, Z_CONV:Z_CONV + CONV_DIM]
    cb = cbuf_ref[...]
    cw = k["conv_w"]
    acc = xc * cw[3:4, :]
    for i in range(CONV_W - 1):
        cur = pltpu.roll(xc, 3 - i, 0)
        old = cb if i == 0 else pltpu.roll(cb, C - i, 0)
        acc = acc + jnp.where(tpos + i >= 3, cur, old) * cw[i:i + 1, :]
    cout_ref[...] = pltpu.roll(xc, C - 1, 0)
    z = jnp.concatenate([z[:, :Z_CONV], _silu(acc), z[:, Z_CONV + CONV_DIM:]], axis=1)

    def rowmask(n):
        return (rowi // SEQ_S) == n

    o, qd, kd, v = _ret_intra(z, cos_ref[...], sin_ref[...], k)
    dch = k["dch"][...]
    qh = [qd[:, hh * 64:(hh + 1) * 64] for hh in range(NH)]
    kh = [kd[:, hh * 64:(hh + 1) * 64] for hh in range(NH)]
    vh = [v[:, hh * 64:(hh + 1) * 64] for hh in range(NH)]

    def ret_body(n, accs):
        rm = rowmask(n)
        out = []
        for hh in range(NH):
            s = sret_in[n, hh]
            out.append(jnp.where(rm, _mm(qh[hh], s), accs[hh]))
            sret_out[n, hh] = s * dch[:, hh * 64:(hh + 1) * 64] + _mm(jnp.where(rm, kh[hh], 0.0), vh[hh], TN)
        return tuple(out)

    accs = lax.fori_loop(0, NSEQ, ret_body, tuple(jnp.zeros((C, 64), F32) for _ in range(NH)))
    o = o + jnp.concatenate(accs, axis=1)
    y_r = _ret_out(o, z[:, Z_RG:Z_RG + 256], k)

    g = _gdn_intra(z, k, levels)
    lhs = [jnp.concatenate([g["solk"][hh * C:(hh + 1) * C], g["qst"][hh * C:(hh + 1) * C]], axis=0)
           for hh in range(NH)]
    rowi2 = lax.broadcasted_iota(jnp.int32, (2 * C, 1), 0)

    def gdn_body1(n, accs):
        rm2 = ((rowi2 % C) // SEQ_S) == n
        return tuple(jnp.where(rm2, _mm(lhs[hh], sgdn_in[n, hh]), accs[hh]) for hh in range(NH))

    accs = lax.fori_loop(0, NSEQ, gdn_body1, tuple(jnp.zeros((2 * C, GDN_DV), F32) for _ in range(NH)))
    us = [g["solv"][hh * C:(hh + 1) * C] - accs[hh][:C] for hh in range(NH)]
    ust = jnp.concatenate(us, axis=0)
    ost = g["ebcol"] * jnp.concatenate([accs[hh][C:] for hh in range(NH)], axis=0) + _mm(g["qk"], ust)
    kdh = [g["kdec"][hh * C:(hh + 1) * C] for hh in range(NH)]
    ebl = [g["eblast"][hh * C:(hh + 1) * C] for hh in range(NH)]

    def gdn_body2(n, carry):
        rm = rowmask(n)
        first = rowi == n * SEQ_S
        for hh in range(NH):
            scale = jnp.sum(jnp.where(first, ebl[hh], 0.0), axis=0, keepdims=True)
            sgdn_out[n, hh] = scale * sgdn_in[n, hh] + _mm(jnp.where(rm, kdh[hh], 0.0), us[hh], TN)
        return carry

    lax.fori_loop(0, NSEQ, gdn_body2, 0)
    y_d = _gdn_out(ost, z[:, Z_DG:Z_DG + 512], k)

    o, qe, ke, v, gk = _gla_intra(z, k, levels)
    qh = [qe[:, hh * 32:(hh + 1) * 32] for hh in range(NH)]
    kh = [ke[:, hh * 32:(hh + 1) * 32] for hh in range(NH)]
    gh = [gk[:, hh * 32:(hh + 1) * 32] for hh in range(NH)]
    vh = [v[:, hh * 64:(hh + 1) * 64] for hh in range(NH)]
    ones = jnp.ones((C, GLA_DV), F32)

    def gla_body(n, accs):
        rm = rowmask(n)
        out = []
        for hh in range(NH):
            s = sgla_in[n, hh]
            out.append(jnp.where(rm, _mm(qh[hh], s), accs[hh]))
            escale = jnp.exp(_mm(jnp.where(rm, gh[hh], 0.0), ones, TN, hi=True))
            sgla_out[n, hh] = s * escale + _mm(jnp.where(rm, kh[hh], 0.0), vh[hh], TN)
        return tuple(out)

    accs = lax.fori_loop(0, NSEQ, gla_body, tuple(jnp.zeros((C, 64), F32) for _ in range(NH)))
    o = o + jnp.concatenate(accs, axis=1)
    y_l = _gla_out(o, z[:, Z_LG:Z_LG + 256], k)

    y = jnp.concatenate([y_r, y_d, y_l], axis=1).astype(BF16)
    o_ref[...] = x + jnp.dot(y, wout_ref[...], preferred_element_type=F32)


def _const_spec1(a):
    nd = a.ndim
    return pl.BlockSpec(a.shape, lambda i, _n=nd: (0,) * _n)


def _mixer_sample(x, nw, win, wout, cos, sin, sret, sgdn, sgla, cbuf, params, consts):
    n = x.shape[0]
    nb = sret.shape[0]
    levels = _levels(SEQ_S)
    extras = [params[nm] for nm in _PARAM_NAMES] + [consts[nm] for nm in _CONST_NAMES]
    st_specs = [
        pl.BlockSpec((NSEQ, NH, RET_DK, RET_DV), lambda i: (i, 0, 0, 0)),
        pl.BlockSpec((NSEQ, NH, GDN_DK, GDN_DV), lambda i: (i, 0, 0, 0)),
        pl.BlockSpec((NSEQ, NH, GLA_DK, GLA_DV), lambda i: (i, 0, 0, 0)),
        pl.BlockSpec((C, CONV_DIM), lambda i: (i, 0)),
    ]
    in_specs = [
        pl.BlockSpec((C, D_MODEL), lambda i: (i, 0)),
        _const_spec1(nw), _const_spec1(win), _const_spec1(wout), _const_spec1(cos), _const_spec1(sin),
    ] + st_specs + [_const_spec1(a) for a in extras]
    out_shape = (
        jax.ShapeDtypeStruct((n, D_MODEL), F32),
        jax.ShapeDtypeStruct((nb, NH, RET_DK, RET_DV), F32),
        jax.ShapeDtypeStruct((nb, NH, GDN_DK, GDN_DV), F32),
        jax.ShapeDtypeStruct((nb, NH, GLA_DK, GLA_DV), F32),
        jax.ShapeDtypeStruct((n, CONV_DIM), F32),
    )
    out_specs = (pl.BlockSpec((C, D_MODEL), lambda i: (i, 0)),) + tuple(st_specs)
    return pl.pallas_call(
        functools.partial(_mixer_sample_kernel, levels=levels),
        grid=(n // C,),
        in_specs=in_specs,
        out_specs=out_specs,
        out_shape=out_shape,
        compiler_params=pltpu.CompilerParams(
            dimension_semantics=("parallel",), vmem_limit_bytes=VMEM_LIMIT),
        name="mixer_sample",
    )(x, nw, win, wout, cos, sin, sret, sgdn, sgla, cbuf, *extras)


def _permute_w_in(w):
    pad = jnp.zeros((D_MODEL, NZ - Z_SM - 24), w.dtype)
    return jnp.concatenate(
        [w[:, 0:2560], w[:, 2568:3080], w[:, 3080:3592], w[:, 3608:3864], w[:, 2560:2568], w[:, 3592:3608], pad],
        axis=1).astype(BF16)


def _row(v, width=None):
    v = v.astype(F32)[None, :]
    if width is not None and v.shape[1] < width:
        v = jnp.pad(v, ((0, 0), (0, width - v.shape[1])))
    return v


def kernel(x_prompt, x_sample, state_ret, state_gdn, state_gdn_conv, state_gla, norm_ffn1, ffn1_w1, ffn1_w3, ffn1_w2, norm_mix, w_in, ret_gn_w, ret_gn_b, gdn_conv_w, gdn_A_log, gdn_dt_bias, gdn_norm_w, gla_gate_w, gla_gate_b, gla_norm_w, w_out, norm_ffn2, ffn2_w1, ffn2_w3, ffn2_w2, norm_final):
    bp, tp, _ = x_prompt.shape
    bs, ts, _ = x_sample.shape
    assert ts == SEQ_S and tp % 256 == 0 and (bs * ts) % C == 0

    consts_p = dict(_chunk_consts(C), **_ret_consts(C))
    consts_s = dict(_chunk_consts(SEQ_S), **_ret_consts(SEQ_S))
    cos_p, sin_p = _rope_tables(jnp.arange(tp, dtype=jnp.int32))
    cos_s, sin_s = _rope_tables(PAST_LEN + (jnp.arange(C, dtype=jnp.int32) % SEQ_S))

    xp = x_prompt.reshape(bp * tp, D_MODEL)
    xs = x_sample.reshape(bs * ts, D_MODEL)
    fw = _row(norm_final)
    outs_p = [[] for _ in range(4)]
    outs_s = [[] for _ in range(4)]
    for l in range(DEPTH):
        params = dict(
            alog=_row(gdn_A_log[l], LANES), dtb=_row(gdn_dt_bias[l], LANES),
            gwp=jnp.zeros((LANES, NH * GLA_DK), F32).at[SM_LLR:SM_LLR + GLA_RANK].set(gla_gate_w[l]),
            gla_gate_b=_row(gla_gate_b[l]), ret_gn_w=_row(ret_gn_w[l]), ret_gn_b=_row(ret_gn_b[l]),
            gdn_norm_w=_row(gdn_norm_w[l]), gla_norm_w=_row(jnp.tile(gla_norm_w[l], NH)),
            conv_w=gdn_conv_w[l].astype(F32),
        )
        win = _permute_w_in(w_in[l])
        wout = w_out[l].astype(BF16)
        f1 = (_row(norm_ffn1[l]), ffn1_w1[l].astype(BF16), ffn1_w3[l].astype(BF16), ffn1_w2[l].astype(BF16))
        f2 = (_row(norm_ffn2[l]), ffn2_w1[l].astype(BF16), ffn2_w3[l].astype(BF16), ffn2_w2[l].astype(BF16))
        nw = _row(norm_mix[l])
        last = l == DEPTH - 1

        xp = _ffn(xp, *f1, fw, final=False)
        xs = _ffn(xs, *f1, fw, final=False)

        xp3, sret, sgdn, sgla, conv = _mixer_prompt(
            xp.reshape(bp, tp, D_MODEL), nw, win, wout, cos_p, sin_p, params, consts_p)
        xp = xp3.reshape(bp * tp, D_MODEL)
        outs_p[0].append(jnp.stack([sret[:, h * 64:(h + 1) * 64, h * 64:(h + 1) * 64] for h in range(NH)], axis=1))
        outs_p[1].append(sgdn)
        outs_p[2].append(conv)
        outs_p[3].append(jnp.stack([sgla[:, h * 32:(h + 1) * 32, h * 64:(h + 1) * 64] for h in range(NH)], axis=1))

        cbuf = jnp.pad(state_gdn_conv[l].astype(F32), ((0, 0), (0, 1), (0, 0))).reshape(bs * ts, CONV_DIM)
        xs, sret, sgdn, sgla, cout = _mixer_sample(
            xs, nw, win, wout, cos_s, sin_s, state_ret[l].astype(F32), state_gdn[l].astype(F32),
            state_gla[l].astype(F32), cbuf, params, consts_s)
        outs_s[0].append(sret)
        outs_s[1].append(sgdn)
        outs_s[2].append(cout.reshape(bs, ts, CONV_DIM)[:, :CONV_W - 1])
        outs_s[3].append(sgla)

        xp = _ffn(xp, *f2, fw, final=last)
        xs = _ffn(xs, *f2, fw, final=last)

    y_prompt = xp.reshape(bp, tp, D_MODEL)
    y_sample = xs.reshape(bs, ts, D_MODEL)
    dts = (state_ret.dtype, state_gdn.dtype, state_gdn_conv.dtype, state_gla.dtype)
    sp = [jnp.stack(o).astype(d) for o, d in zip(outs_p, dts)]
    ss = [jnp.stack(o).astype(d) for o, d in zip(outs_s, dts)]
    return (y_prompt, y_sample, sp[0], sp[1], sp[2], sp[3], ss[0], ss[1], ss[2], ss[3])
```

```python
import functools

import numpy as np
import jax
import jax.numpy as jnp
from jax import lax
from jax.experimental import pallas as pl
from jax.experimental.pallas import tpu as pltpu

F32 = jnp.float32
BF16 = jnp.bfloat16

D_MODEL = 1024
DEPTH = 4
PAST_LEN = 16384
NH = 4
RET_DK = 64
RET_DV = 64
GDN_DK = 128
GDN_DV = 128
GLA_DK = 32
GLA_DV = 64
GLA_RANK = 16
GLA_GATE_NORM = 16.0
CONV_W = 4
CONV_DIM = NH * (2 * GDN_DK + GDN_DV)
D_FF = 2816
ROPE_BASE = 10000.0
EPS = 1e-6
GN_EPS = 1e-5

C = 64
R = NH * C
LANES = 128

Z_RQ, Z_RK, Z_RV, Z_RG = 0, 256, 512, 768
Z_CONV = 1024
Z_DG = 2560
Z_LQ, Z_LK, Z_LV, Z_LG = 3072, 3200, 3328, 3584
Z_SM = 3840
NZ = 3968
SM_DA, SM_DB, SM_LLR = 0, 4, 8

NN = (((1,), (0,)), ((), ()))
NT = (((1,), (1,)), ((), ()))
TN = (((0,), (0,)), ((), ()))

VMEM_LIMIT = 56 * 1024 * 1024
FFN_TM = 512
FFN_TF = 256
MIX_TC = 256


def _mm(a, b, dims=NN):
    return lax.dot_general(a.astype(BF16), b.astype(BF16), dims, preferred_element_type=F32)


def _split2(x):
    hi = x.astype(BF16)
    return hi, (x - hi.astype(F32)).astype(BF16)


def _mm_data_const(x, cb):
    hi, mid = _split2(x)
    m = x.shape[0]
    r = lax.dot_general(jnp.concatenate([hi, mid], axis=0), cb, NN, preferred_element_type=F32)
    return r[:m] + r[m:]


def _mm_const_data(cb, x):
    hi, mid = _split2(x)
    n = x.shape[1]
    r = lax.dot_general(cb, jnp.concatenate([hi, mid], axis=1), NN, preferred_element_type=F32)
    return r[:, :n] + r[:, n:]


def _silu(x):
    return x * jax.nn.sigmoid(x)


def _softplus(x):
    return jnp.maximum(x, 0.0) + jnp.log1p(jnp.exp(-jnp.abs(x)))


def _log_sigmoid(x):
    return jnp.minimum(x, 0.0) - jnp.log1p(jnp.exp(-jnp.abs(x)))


def _rms_rows(x, w):
    ms = jnp.mean(x * x, axis=-1, keepdims=True)
    return x * lax.rsqrt(ms + EPS) * w


def _stack_bd(x, width):
    lane = lax.broadcasted_iota(jnp.int32, x.shape, 1) // width
    zero = jnp.zeros_like(x)
    return jnp.concatenate([jnp.where(lane == h, x, zero) for h in range(NH)], axis=0)


def _col_to_compact(col):
    lane = lax.broadcasted_iota(jnp.int32, (C, R), 1) // C
    out = jnp.zeros((C, R), F32)
    for h in range(NH):
        out = jnp.where(lane == h, col[h * C:(h + 1) * C], out)
    return out


def _ffn_kernel(x_ref, nw_ref, w1_ref, w3_ref, w2_ref, fw_ref, o_ref, h_ref, acc_ref, *, final):
    x = x_ref[...]
    h_ref[...] = _rms_rows(x, nw_ref[...]).astype(BF16)
    for j in range(w1_ref.shape[0]):
        h = h_ref[...]
        a = jnp.dot(h, w1_ref[j], preferred_element_type=F32)
        g = jnp.dot(h, w3_ref[j], preferred_element_type=F32)
        p = jnp.dot((_silu(a) * g).astype(BF16), w2_ref[j], preferred_element_type=F32)
        if j == 0:
            acc_ref[...] = p
        else:
            acc_ref[...] += p
    y = x + 0.5 * acc_ref[...]
    if final:
        y = _rms_rows(y, fw_ref[...])
    o_ref[...] = y


def _resident(a):
    nd = a.ndim
    return pl.BlockSpec(a.shape, lambda *_, _n=nd: (0,) * _n, pipeline_mode=pl.Buffered(1))


def _ffn(x, nw, w1, w3, w2, fw, *, final, tm=FFN_TM):
    n = x.shape[0]
    return pl.pallas_call(
        functools.partial(_ffn_kernel, final=final),
        grid=(n // tm,),
        in_specs=[
            pl.BlockSpec((tm, D_MODEL), lambda i: (i, 0)),
            _resident(nw), _resident(w1), _resident(w3), _resident(w2), _resident(fw),
        ],
        out_specs=pl.BlockSpec((tm, D_MODEL), lambda i: (i, 0)),
        out_shape=jax.ShapeDtypeStruct((n, D_MODEL), F32),
        scratch_shapes=[pltpu.VMEM((tm, D_MODEL), BF16), pltpu.VMEM((tm, D_MODEL), F32)],
        compiler_params=pltpu.CompilerParams(
            dimension_semantics=("parallel",), vmem_limit_bytes=VMEM_LIMIT),
        name="ffn",
    )(x, nw, w1, w3, w2, fw)


def _tile_ffn_weights(w1, w3, w2, tf=FFN_TF):
    nf = D_FF // tf
    up = lambda w: jnp.transpose(w.astype(BF16).reshape(D_MODEL, nf, tf), (1, 0, 2))
    return up(w1), up(w3), w2.astype(BF16).reshape(nf, tf, D_MODEL)


def _levels(L):
    return [s for s in (1, 2, 4, 8, 16, 32) if s < L]


def _chunk_consts(L):
    i = np.arange(C)
    sid, p = i // L, i % L
    same = sid[:, None] == sid[None, :]
    causal = same & (i[:, None] >= i[None, :])
    strict = same & (i[:, None] > i[None, :])
    lv = _levels(L)
    cum, pm, lm = [causal, same], [np.eye(C, dtype=bool)], []
    for s in lv:
        blk = p // s
        inblk = same & (blk[:, None] == blk[None, :])
        odd = (blk % 2 == 1)[:, None]
        cum.append(inblk & np.where(odd, i[None, :] <= i[:, None], i[None, :] > i[:, None]))
        pm.append(same & odd & (blk[None, :] == blk[:, None] - 1))
        lm.append(strict & ((p[:, None] // (2 * s)) == (p[None, :] // (2 * s))) & (blk[:, None] != blk[None, :]))
    r = np.arange(R)
    bdfull = (r[:, None] // C) == (r[None, :] // C)
    wide = lambda m: np.tile(m, (1, NH))
    f = lambda m: jnp.asarray(np.asarray(m, dtype=np.float32))
    gla_bd = (np.arange(NH * GLA_DK)[:, None] // GLA_DK) == (np.arange(NH * GLA_DV)[None, :] // GLA_DV)
    seqsel = np.zeros((C, LANES), np.float32)
    seqsel[i, sid] = 1.0
    return dict(
        cumg=f(np.concatenate(cum, axis=0)).astype(BF16),
        cumd=f(np.concatenate(cum[:2], axis=0)).astype(BF16),
        pmc=f(np.stack([wide(m) for m in pm])),
        lmc=f(np.stack([wide(m) for m in lm])),
        causal4=f(wide(causal)), eye4=f(wide(np.eye(C, dtype=bool))),
        g64=(f(bdfull) * (1.0 / 64.0)).astype(BF16), bdfull=f(bdfull), glabd=f(gla_bd),
        seqsel=jnp.asarray(seqsel).astype(BF16),
    )


def _ret_consts(L):
    log_gamma = jnp.log(1.0 - 2.0 ** (-5.0 - jnp.arange(NH, dtype=F32)))
    i = np.arange(C)
    sid, p = i // L, (i % L).astype(np.float32)
    causal = (sid[:, None] == sid[None, :]) & (i[:, None] >= i[None, :])
    diff = jnp.asarray(np.where(causal, p[:, None] - p[None, :], 0.0).astype(np.float32))
    dec = jnp.where(causal[None], jnp.exp(diff[None] * log_gamma[:, None, None]), 0.0)
    lg_l = jnp.repeat(log_gamma, RET_DV)[None, :]
    pj = jnp.asarray(p)[:, None]
    return dict(
        decc=jnp.concatenate([dec[h] for h in range(NH)], axis=1),
        dq=jnp.exp((pj + 1.0) * lg_l),
        dk=jnp.exp((L - 1.0 - pj) * lg_l),
        dch=jnp.exp(L * lg_l),
    )


def _rope_tables(pos):
    half = RET_DK // 2
    inv = ROPE_BASE ** (-jnp.arange(half, dtype=F32) / half)
    ang = pos.astype(F32)[:, None] * inv[None, :]
    cos, sin = jnp.cos(ang), jnp.sin(ang)
    return (jnp.tile(jnp.concatenate([cos, cos], axis=1), (1, NH)),
            jnp.tile(jnp.concatenate([-sin, sin], axis=1), (1, NH)))


def _rotary(x, cos, sin):
    lane = lax.broadcasted_iota(jnp.int32, x.shape, 1)
    swapped = jnp.where((lane % RET_DK) < RET_DK // 2,
                        pltpu.roll(x, x.shape[1] - RET_DK // 2, 1), pltpu.roll(x, RET_DK // 2, 1))
    return x * cos + swapped * sin


def _ret_intra(z, cos, sin, k):
    q = _rotary(z[:, Z_RQ:Z_RQ + 256], cos, sin)
    kk = _rotary(z[:, Z_RK:Z_RK + 256], cos, sin) * (RET_DK ** -0.5)
    v = z[:, Z_RV:Z_RV + 256]
    sc = _mm(q, _stack_bd(kk.astype(BF16), RET_DK), NT) * k["decc"][...]
    o = _mm(sc, _stack_bd(v.astype(BF16), RET_DV))
    return o, q * k["dq"][...], kk * k["dk"][...], v


def _ret_out(o, rg, k):
    g = k["g64"][...]
    d = o - _mm_data_const(o, g)
    var = _mm_data_const(d * d, g)
    return _silu(rg) * (d * lax.rsqrt(var + GN_EPS) * k["ret_gn_w"][...] + k["ret_gn_b"][...])


def _gla_intra(z, k, nlev):
    sm = z[:, Z_SM:Z_SM + LANES]
    gk = _log_sigmoid(_mm(sm, k["gwp"][...]) + k["gla_gate_b"][...]) * (1.0 / GLA_GATE_NORM)
    q = z[:, Z_LQ:Z_LQ + 128] * (GLA_DK ** -0.5)
    kk = z[:, Z_LK:Z_LK + 128]
    v = z[:, Z_LV:Z_LV + 256]
    cs = _mm_const_data(k["cumg"][...], gk)
    b, blast = cs[:C], cs[C:2 * C]
    sc = _mm(q, _stack_bd(kk.astype(BF16), GLA_DK), NT) * k["pmc"][0]
    for li in range(nlev):
        e = jnp.exp(cs[(2 + li) * C:(3 + li) * C])
        sc = sc + _mm(q * e, _stack_bd((kk * e).astype(BF16), GLA_DK), NT) * k["pmc"][li + 1]
    o = _mm(sc, _stack_bd(v.astype(BF16), GLA_DV))
    return o, q * jnp.exp(b), kk * jnp.exp(blast - b), v, b, blast


def _gla_out(o, lg, k):
    ms = _mm_data_const(o * o, k["g64"][...])
    return o * lax.rsqrt(ms + EPS) * k["gla_norm_w"][...] * _silu(lg)


def _lane_col(blk, lane0):
    lane = lax.broadcasted_iota(jnp.int32, blk.shape, 1)
    st = jnp.concatenate([jnp.where(lane == lane0 + h, blk, 0.0) for h in range(NH)], axis=0)
    return jnp.sum(st, axis=-1, keepdims=True)


def _gdn_front(z, k):
    conv = z[:, Z_CONV:Z_CONV + CONV_DIM]
    sm = z[:, Z_SM:Z_SM + LANES]

    def l2n(x):
        return x * lax.rsqrt(jnp.sum(x * x, axis=-1, keepdims=True) + EPS)

    qs = [l2n(conv[:, h * 128:(h + 1) * 128]) * (GDN_DK ** -0.5) for h in range(NH)]
    ks = [l2n(conv[:, 512 + h * 128:512 + (h + 1) * 128]) for h in range(NH)]
    vst = jnp.concatenate([conv[:, 1024 + h * 128:1024 + (h + 1) * 128] for h in range(NH)], axis=0)
    qst = jnp.concatenate(qs, axis=0)
    kst = jnp.concatenate(ks, axis=0)
    kbd = _stack_bd(jnp.concatenate(ks, axis=1).astype(BF16), GDN_DK)
    gd = -jnp.exp(k["alog"][...]) * _softplus(sm + k["dtb"][...])
    beta = jax.nn.sigmoid(sm)
    cs = _mm_const_data(k["cumd"][...], gd)
    bcol = _lane_col(cs[:C], SM_DA)
    blcol = _lane_col(cs[C:], SM_DA)
    betacol = _lane_col(beta, SM_DB)
    bcolc = _col_to_compact(bcol)
    browc = jnp.sum(bcolc * k["eye4"][...], axis=0, keepdims=True)
    causal = k["causal4"][...] > 0.5
    decc = jnp.where(causal, jnp.exp(jnp.where(causal, bcolc - browc, 0.0)), 0.0)
    ac = _mm(jnp.concatenate(ks, axis=1), kbd, NT) * decc * _col_to_compact(betacol)
    qkc = _mm(jnp.concatenate(qs, axis=1), kbd, NT) * decc
    return dict(ac=ac, qkc=qkc, qst=qst, kst=kst, vst=vst, bcol=bcol, blcol=blcol, betacol=betacol)


def _gdn_inverse(acs, k, nlev):
    xs = [k["eye4"][...] - a * k["lmc"][0] for a in acs]
    for li in range(1, nlev):
        ms = [a * k["lmc"][li] for a in acs]
        ys = [_mm(m, _stack_bd(x.astype(BF16), C)) for m, x in zip(ms, xs)]
        xs = [x - _mm(x, _stack_bd(y.astype(BF16), C)) for x, y in zip(xs, ys)]
    return xs


def _gdn_back(f, xc):
    kb = f["kst"] * f["betacol"]
    rhs = jnp.concatenate([f["vst"] * f["betacol"], kb * jnp.exp(f["bcol"])], axis=1)
    sol = _mm(_stack_bd(xc.astype(BF16), C), rhs)
    return dict(qst=f["qst"], solv=sol[:, :GDN_DV], solk=sol[:, GDN_DV:],
                qk=_stack_bd(f["qkc"].astype(BF16), C), ebcol=jnp.exp(f["bcol"]),
                kdec=f["kst"] * jnp.exp(f["blcol"] - f["bcol"]), eblast=jnp.exp(f["blcol"]))


def _gdn_out(ost, dg, k):
    yst = _rms_rows(ost, k["gdn_norm_w"][...])
    y = jnp.concatenate([yst[h * C:(h + 1) * C] for h in range(NH)], axis=1)
    return y * _silu(dg)


_PARAM_NAMES = ("alog", "dtb", "gwp", "gla_gate_b", "ret_gn_w", "ret_gn_b", "gdn_norm_w", "gla_norm_w", "conv_w")
_CONST_NAMES = ("cumg", "cumd", "pmc", "lmc", "causal4", "eye4", "g64", "bdfull", "glabd", "seqsel",
                "decc", "dq", "dk", "dch")


def _mixer_prompt_kernel(*refs, tc, nlev):
    n_in = 6 + len(_PARAM_NAMES) + len(_CONST_NAMES)
    x_ref, nw_ref, win_ref, wout_ref, cos_ref, sin_ref = refs[:6]
    k = dict(zip(_PARAM_NAMES + _CONST_NAMES, refs[6:n_in]))
    o_ref, sret_ref, sgdn_ref, sgla_ref, conv_ref = refs[n_in:n_in + 5]
    z_ref, xp_ref, y_ref = refs[n_in + 5:]
    t = pl.program_id(1)
    nch = tc // C

    @pl.when(t == 0)
    def _():
        sret_ref[...] = jnp.zeros_like(sret_ref)
        sgdn_ref[...] = jnp.zeros_like(sgdn_ref)
        sgla_ref[...] = jnp.zeros_like(sgla_ref)
        xp_ref[pl.ds(0, 8), :] = jnp.zeros((8, CONV_DIM), F32)

    h = _rms_rows(x_ref[0], nw_ref[...]).astype(BF16)
    z_ref[...] = jnp.dot(h, win_ref[...], preferred_element_type=F32)

    xp_ref[pl.ds(8, tc), :] = z_ref[:, Z_CONV:Z_CONV + CONV_DIM]
    cw = k["conv_w"]
    acc = xp_ref[pl.ds(8, tc), :] * cw[3:4, :]
    for i in range(CONV_W - 1):
        acc = acc + xp_ref[pl.ds(5 + i, tc), :] * cw[i:i + 1, :]
    z_ref[:, Z_CONV:Z_CONV + CONV_DIM] = _silu(acc)
    xp_ref[pl.ds(0, 8), :] = xp_ref[pl.ds(tc, 8), :]

    zs = [z_ref[pl.ds(c * C, C), :] for c in range(nch)]
    fronts = [_gdn_front(z, k) for z in zs]
    xcs = _gdn_inverse([f["ac"] for f in fronts], k, nlev)
    gs = [_gdn_back(f, xc) for f, xc in zip(fronts, xcs)]
    rets = [_ret_intra(z, cos_ref[pl.ds(c * C, C), :], sin_ref[pl.ds(c * C, C), :], k) for c, z in enumerate(zs)]
    glas = [_gla_intra(z, k, nlev) for z in zs]

    for c in range(nch):
        z = zs[c]

        o, qd, kd, v = rets[c]
        s = sret_ref[0]
        o = o + _mm(qd, s)
        sret_ref[0] = s * k["dch"][...] + _mm(kd, v, TN) * k["bdfull"][...]
        y_r = _ret_out(o, z[:, Z_RG:Z_RG + 256], k)

        g = gs[c]
        us, qss = [], []
        for hh in range(NH):
            lhs = jnp.concatenate([g["solk"][hh * C:(hh + 1) * C], g["qst"][hh * C:(hh + 1) * C]], axis=0)
            r = _mm(lhs, sgdn_ref[0, hh])
            us.append(g["solv"][hh * C:(hh + 1) * C] - r[:C])
            qss.append(r[C:])
        ost = g["ebcol"] * jnp.concatenate(qss, axis=0) + _mm(g["qk"], jnp.concatenate(us, axis=0))
        for hh in range(NH):
            sl = slice(hh * C, (hh + 1) * C)
            sgdn_ref[0, hh] = (g["eblast"][hh * C:hh * C + 1] * sgdn_ref[0, hh]
                               + _mm(g["kdec"][sl], us[hh], TN))
        y_d = _gdn_out(ost, z[:, Z_DG:Z_DG + 512], k)

        o, qe, ke, v, b, _ = glas[c]
        s = sgla_ref[0]
        o = o + _mm(qe, s)
        escale = jnp.exp(b[C - 8:, :].T[:, 7:8])
        sgla_ref[0] = s * escale + _mm(ke, v, TN) * k["glabd"][...]
        y_l = _gla_out(o, z[:, Z_LG:Z_LG + 256], k)

        y_ref[pl.ds(c * C, C), :] = jnp.concatenate([y_r, y_d, y_l], axis=1).astype(BF16)

    o_ref[0] = x_ref[0] + jnp.dot(y_ref[...], wout_ref[...], preferred_element_type=F32)

    @pl.when(t == pl.num_programs(1) - 1)
    def _():
        conv_ref[0] = xp_ref[pl.ds(5, 3), :]


def _mixer_prompt(x, nw, win, wout, cos, sin, params, consts, *, tc=MIX_TC):
    bsz, tlen, _ = x.shape
    nlev = len(_levels(C))
    extras = [params[n] for n in _PARAM_NAMES] + [consts[n] for n in _CONST_NAMES]
    in_specs = [
        pl.BlockSpec((1, tc, D_MODEL), lambda b, t: (b, t, 0)),
        _resident(nw), _resident(win), _resident(wout),
        pl.BlockSpec((tc, 256), lambda b, t: (t, 0)),
        pl.BlockSpec((tc, 256), lambda b, t: (t, 0)),
    ] + [_resident(a) for a in extras]
    out_shape = (
        jax.ShapeDtypeStruct((bsz, tlen, D_MODEL), F32),
        jax.ShapeDtypeStruct((bsz, 256, 256), F32),
        jax.ShapeDtypeStruct((bsz, NH, GDN_DK, GDN_DV), F32),
        jax.ShapeDtypeStruct((bsz, NH * GLA_DK, NH * GLA_DV), F32),
        jax.ShapeDtypeStruct((bsz, CONV_W - 1, CONV_DIM), F32),
    )
    out_specs = (
        pl.BlockSpec((1, tc, D_MODEL), lambda b, t: (b, t, 0)),
        pl.BlockSpec((1, 256, 256), lambda b, t: (b, 0, 0)),
        pl.BlockSpec((1, NH, GDN_DK, GDN_DV), lambda b, t: (b, 0, 0, 0)),
        pl.BlockSpec((1, NH * GLA_DK, NH * GLA_DV), lambda b, t: (b, 0, 0)),
        pl.BlockSpec((1, CONV_W - 1, CONV_DIM), lambda b, t: (b, 0, 0)),
    )
    return pl.pallas_call(
        functools.partial(_mixer_prompt_kernel, tc=tc, nlev=nlev),
        grid=(bsz, tlen // tc),
        in_specs=in_specs,
        out_specs=out_specs,
        out_shape=out_shape,
        scratch_shapes=[
            pltpu.VMEM((tc, NZ), F32),
            pltpu.VMEM((tc + 8, CONV_DIM), F32),
            pltpu.VMEM((tc, D_MODEL), BF16),
        ],
        compiler_params=pltpu.CompilerParams(
            dimension_semantics=("parallel", "arbitrary"), vmem_limit_bytes=VMEM_LIMIT),
        name="mixer_prompt",
    )(x, nw, win, wout, cos, sin, *extras)


SEQ_S = 4
NSEQ = C // SEQ_S


def _mixer_sample_kernel(*refs, nlev):
    n_in = 10 + len(_PARAM_NAMES) + len(_CONST_NAMES)
    (x_ref, nw_ref, win_ref, wout_ref, cos_ref, sin_ref,
     sret_in, sgdn_in, sgla_in, cbuf_ref) = refs[:10]
    k = dict(zip(_PARAM_NAMES + _CONST_NAMES, refs[10:n_in]))
    o_ref, sret_out, sgdn_out, sgla_out, cout_ref = refs[n_in:n_in + 5]

    x = x_ref[...]
    h = _rms_rows(x, nw_ref[...]).astype(BF16)
    z = jnp.dot(h, win_ref[...], preferred_element_type=F32)

    rowi = lax.broadcasted_iota(jnp.int32, (C, 1), 0)
    tpos = rowi % SEQ_S

    xc = z[:, Z_CONV:Z_CONV + CONV_DIM]
    cb = cbuf_ref[...]
    cw = k["conv_w"]
    acc = xc * cw[3:4, :]
    for i in range(CONV_W - 1):
        cur = pltpu.roll(xc, 3 - i, 0)
        old = cb if i == 0 else pltpu.roll(cb, C - i, 0)
        acc = acc + jnp.where(tpos + i >= 3, cur, old) * cw[i:i + 1, :]
    cout_ref[...] = pltpu.roll(xc, C - 1, 0)
    z = jnp.concatenate([z[:, :Z_CONV], _silu(acc), z[:, Z_CONV + CONV_DIM:]], axis=1)

    def rowmask(n):
        return (rowi // SEQ_S) == n

    o, qd, kd, v = _ret_intra(z, cos_ref[...], sin_ref[...], k)
    dch = k["dch"][...]
    qh = [qd[:, hh * 64:(hh + 1) * 64] for hh in range(NH)]
    kh = [kd[:, hh * 64:(hh + 1) * 64] for hh in range(NH)]
    vh = [v[:, hh * 64:(hh + 1) * 64] for hh in range(NH)]

    def ret_body(n, accs):
        rm = rowmask(n)
        out = []
        for hh in range(NH):
            s = sret_in[n, hh]
            out.append(jnp.where(rm, _mm(qh[hh], s), accs[hh]))
            sret_out[n, hh] = s * dch[:, hh * 64:(hh + 1) * 64] + _mm(jnp.where(rm, kh[hh], 0.0), vh[hh], TN)
        return tuple(out)

    accs = lax.fori_loop(0, NSEQ, ret_body, tuple(jnp.zeros((C, 64), F32) for _ in range(NH)))
    o = o + jnp.concatenate(accs, axis=1)
    y_r = _ret_out(o, z[:, Z_RG:Z_RG + 256], k)

    f = _gdn_front(z, k)
    g = _gdn_back(f, _gdn_inverse([f["ac"]], k, nlev)[0])
    lhs = [jnp.concatenate([g["solk"][hh * C:(hh + 1) * C], g["qst"][hh * C:(hh + 1) * C]], axis=0)
           for hh in range(NH)]
    rowi2 = lax.broadcasted_iota(jnp.int32, (2 * C, 1), 0)

    def gdn_body1(n, accs):
        rm2 = ((rowi2 % C) // SEQ_S) == n
        return tuple(jnp.where(rm2, _mm(lhs[hh], sgdn_in[n, hh]), accs[hh]) for hh in range(NH))

    accs = lax.fori_loop(0, NSEQ, gdn_body1, tuple(jnp.zeros((2 * C, GDN_DV), F32) for _ in range(NH)))
    us = [g["solv"][hh * C:(hh + 1) * C] - accs[hh][:C] for hh in range(NH)]
    ust = jnp.concatenate(us, axis=0)
    ost = g["ebcol"] * jnp.concatenate([accs[hh][C:] for hh in range(NH)], axis=0) + _mm(g["qk"], ust)
    kdh = [g["kdec"][hh * C:(hh + 1) * C] for hh in range(NH)]
    ebl = [g["eblast"][hh * C:(hh + 1) * C] for hh in range(NH)]

    def gdn_body2(n, carry):
        rm = rowmask(n)
        first = rowi == n * SEQ_S
        for hh in range(NH):
            scale = jnp.sum(jnp.where(first, ebl[hh], 0.0), axis=0, keepdims=True)
            sgdn_out[n, hh] = scale * sgdn_in[n, hh] + _mm(jnp.where(rm, kdh[hh], 0.0), us[hh], TN)
        return carry

    lax.fori_loop(0, NSEQ, gdn_body2, 0)
    y_d = _gdn_out(ost, z[:, Z_DG:Z_DG + 512], k)

    o, qe, ke, v, b, blast = _gla_intra(z, k, nlev)
    qh = [qe[:, hh * 32:(hh + 1) * 32] for hh in range(NH)]
    kh = [ke[:, hh * 32:(hh + 1) * 32] for hh in range(NH)]
    vh = [v[:, hh * 64:(hh + 1) * 64] for hh in range(NH)]
    hi, mid = _split2(blast * (1.0 / SEQ_S))
    blt2 = lax.dot_general(jnp.concatenate([hi, mid], axis=1), k["seqsel"][...], TN, preferred_element_type=F32)
    eblt = jnp.exp(blt2[:NH * GLA_DK] + blt2[NH * GLA_DK:])
    lanei = lax.broadcasted_iota(jnp.int32, (GLA_DK, LANES), 1)

    def gla_body(n, accs):
        rm = rowmask(n)
        out = []
        for hh in range(NH):
            s = sgla_in[n, hh]
            out.append(jnp.where(rm, _mm(qh[hh], s), accs[hh]))
            escale = jnp.sum(jnp.where(lanei == n, eblt[hh * GLA_DK:(hh + 1) * GLA_DK], 0.0),
                             axis=-1, keepdims=True)
            sgla_out[n, hh] = s * escale + _mm(jnp.where(rm, kh[hh], 0.0), vh[hh], TN)
        return tuple(out)

    accs = lax.fori_loop(0, NSEQ, gla_body, tuple(jnp.zeros((C, 64), F32) for _ in range(NH)))
    o = o + jnp.concatenate(accs, axis=1)
    y_l = _gla_out(o, z[:, Z_LG:Z_LG + 256], k)

    y = jnp.concatenate([y_r, y_d, y_l], axis=1).astype(BF16)
    o_ref[...] = x + jnp.dot(y, wout_ref[...], preferred_element_type=F32)


def _mixer_sample(x, nw, win, wout, cos, sin, sret, sgdn, sgla, cbuf, params, consts):
    n = x.shape[0]
    nb = sret.shape[0]
    nlev = len(_levels(SEQ_S))
    extras = [params[nm] for nm in _PARAM_NAMES] + [consts[nm] for nm in _CONST_NAMES]
    st_specs = [
        pl.BlockSpec((NSEQ, NH, RET_DK, RET_DV), lambda i: (i, 0, 0, 0)),
        pl.BlockSpec((NSEQ, NH, GDN_DK, GDN_DV), lambda i: (i, 0, 0, 0)),
        pl.BlockSpec((NSEQ, NH, GLA_DK, GLA_DV), lambda i: (i, 0, 0, 0)),
        pl.BlockSpec((C, CONV_DIM), lambda i: (i, 0)),
    ]
    in_specs = [
        pl.BlockSpec((C, D_MODEL), lambda i: (i, 0)),
        _resident(nw), _resident(win), _resident(wout), _resident(cos), _resident(sin),
    ] + st_specs + [_resident(a) for a in extras]
    out_shape = (
        jax.ShapeDtypeStruct((n, D_MODEL), F32),
        jax.ShapeDtypeStruct((nb, NH, RET_DK, RET_DV), F32),
        jax.ShapeDtypeStruct((nb, NH, GDN_DK, GDN_DV), F32),
        jax.ShapeDtypeStruct((nb, NH, GLA_DK, GLA_DV), F32),
        jax.ShapeDtypeStruct((n, CONV_DIM), F32),
    )
    out_specs = (pl.BlockSpec((C, D_MODEL), lambda i: (i, 0)),) + tuple(st_specs)
    return pl.pallas_call(
        functools.partial(_mixer_sample_kernel, nlev=nlev),
        grid=(n // C,),
        in_specs=in_specs,
        out_specs=out_specs,
        out_shape=out_shape,
        compiler_params=pltpu.CompilerParams(
            dimension_semantics=("parallel",), vmem_limit_bytes=VMEM_LIMIT),
        name="mixer_sample",
    )(x, nw, win, wout, cos, sin, sret, sgdn, sgla, cbuf, *extras)


def _permute_w_in(w):
    pad = jnp.zeros((D_MODEL, NZ - Z_SM - 24), w.dtype)
    return jnp.concatenate(
        [w[:, 0:2560], w[:, 2568:3080], w[:, 3080:3592], w[:, 3608:3864], w[:, 2560:2568], w[:, 3592:3608], pad],
        axis=1).astype(BF16)


def _row(v, width=None):
    v = v.astype(F32)[None, :]
    if width is not None and v.shape[1] < width:
        v = jnp.pad(v, ((0, 0), (0, width - v.shape[1])))
    return v


def kernel(x_prompt, x_sample, state_ret, state_gdn, state_gdn_conv, state_gla, norm_ffn1, ffn1_w1, ffn1_w3, ffn1_w2, norm_mix, w_in, ret_gn_w, ret_gn_b, gdn_conv_w, gdn_A_log, gdn_dt_bias, gdn_norm_w, gla_gate_w, gla_gate_b, gla_norm_w, w_out, norm_ffn2, ffn2_w1, ffn2_w3, ffn2_w2, norm_final):
    bp, tp, _ = x_prompt.shape
    bs, ts, _ = x_sample.shape
    assert ts == SEQ_S and tp % MIX_TC == 0 and (bs * ts) % C == 0 and (bp * tp) % FFN_TM == 0

    consts_p = dict(_chunk_consts(C), **_ret_consts(C))
    consts_s = dict(_chunk_consts(SEQ_S), **_ret_consts(SEQ_S))
    cos_p, sin_p = _rope_tables(jnp.arange(tp, dtype=jnp.int32))
    cos_s, sin_s = _rope_tables(PAST_LEN + (jnp.arange(C, dtype=jnp.int32) % SEQ_S))

    xp = x_prompt.reshape(bp * tp, D_MODEL)
    xs = x_sample.reshape(bs * ts, D_MODEL)
    fw = _row(norm_final)
    outs_p = [[] for _ in range(4)]
    outs_s = [[] for _ in range(4)]
    for l in range(DEPTH):
        params = dict(
            alog=_row(gdn_A_log[l], LANES), dtb=_row(gdn_dt_bias[l], LANES),
            gwp=jnp.zeros((LANES, NH * GLA_DK), F32).at[SM_LLR:SM_LLR + GLA_RANK].set(gla_gate_w[l]).astype(BF16),
            gla_gate_b=_row(gla_gate_b[l]), ret_gn_w=_row(ret_gn_w[l]), ret_gn_b=_row(ret_gn_b[l]),
            gdn_norm_w=_row(gdn_norm_w[l]), gla_norm_w=_row(jnp.tile(gla_norm_w[l], NH)),
            conv_w=gdn_conv_w[l].astype(F32),
        )
        win = _permute_w_in(w_in[l])
        wout = w_out[l].astype(BF16)
        f1 = (_row(norm_ffn1[l]),) + _tile_ffn_weights(ffn1_w1[l], ffn1_w3[l], ffn1_w2[l])
        f2 = (_row(norm_ffn2[l]),) + _tile_ffn_weights(ffn2_w1[l], ffn2_w3[l], ffn2_w2[l])
        nw = _row(norm_mix[l])
        last = l == DEPTH - 1

        xp = _ffn(xp, *f1, fw, final=False)
        xs = _ffn(xs, *f1, fw, final=False)

        xp3, sret, sgdn, sgla, conv = _mixer_prompt(
            xp.reshape(bp, tp, D_MODEL), nw, win, wout, cos_p, sin_p, params, consts_p)
        xp = xp3.reshape(bp * tp, D_MODEL)
        outs_p[0].append(jnp.stack([sret[:, h * 64:(h + 1) * 64, h * 64:(h + 1) * 64] for h in range(NH)], axis=1))
        outs_p[1].append(sgdn)
        outs_p[2].append(conv)
        outs_p[3].append(jnp.stack([sgla[:, h * 32:(h + 1) * 32, h * 64:(h + 1) * 64] for h in range(NH)], axis=1))

        cbuf = jnp.pad(state_gdn_conv[l].astype(F32), ((0, 0), (0, 1), (0, 0))).reshape(bs * ts, CONV_DIM)
        xs, sret, sgdn, sgla, cout = _mixer_sample(
            xs, nw, win, wout, cos_s, sin_s, state_ret[l].astype(F32), state_gdn[l].astype(F32),
            state_gla[l].astype(F32), cbuf, params, consts_s)
        outs_s[0].append(sret)
        outs_s[1].append(sgdn)
        outs_s[2].append(cout.reshape(bs, ts, CONV_DIM)[:, :CONV_W - 1])
        outs_s[3].append(sgla)

        xp = _ffn(xp, *f2, fw, final=last)
        xs = _ffn(xs, *f2, fw, final=last)

    y_prompt = xp.reshape(bp, tp, D_MODEL)
    y_sample = xs.reshape(bs, ts, D_MODEL)
    dts = (state_ret.dtype, state_gdn.dtype, state_gdn_conv.dtype, state_gla.dtype)
    sp = [jnp.stack(o).astype(d) for o, d in zip(outs_p, dts)]
    ss = [jnp.stack(o).astype(d) for o, d in zip(outs_s, dts)]
    return (y_prompt, y_sample, sp[0], sp[1], sp[2], sp[3], ss[0], ss[1], ss[2], ss[3])
```

```python
import functools

import numpy as np
import jax
import jax.numpy as jnp
from jax import lax
from jax.experimental import pallas as pl
from jax.experimental.pallas import tpu as pltpu

F32 = jnp.float32
BF16 = jnp.bfloat16

D_MODEL = 1024
DEPTH = 4
PAST_LEN = 16384
NH = 4
RET_DK = 64
RET_DV = 64
GDN_DK = 128
GDN_DV = 128
GLA_DK = 32
GLA_DV = 64
GLA_RANK = 16
GLA_GATE_NORM = 16.0
CONV_W = 4
CONV_DIM = NH * (2 * GDN_DK + GDN_DV)
D_FF = 2816
ROPE_BASE = 10000.0
EPS = 1e-6
GN_EPS = 1e-5

C = 64
R = NH * C
LANES = 128

Z_RQ, Z_RK, Z_RV, Z_RG = 0, 256, 512, 768
Z_CONV = 1024
Z_DG = 2560
Z_LQ, Z_LK, Z_LV, Z_LG = 3072, 3200, 3328, 3584
Z_SM = 3840
NZ = 3968
SM_DA, SM_DB, SM_LLR = 0, 4, 8

NN = (((1,), (0,)), ((), ()))
NT = (((1,), (1,)), ((), ()))
TN = (((0,), (0,)), ((), ()))

VMEM_LIMIT = 56 * 1024 * 1024
FFN_TM = 512
FFN_TF = 256
MIX_TC = 256


def _mm(a, b, dims=NN):
    return lax.dot_general(a.astype(BF16), b.astype(BF16), dims, preferred_element_type=F32)


def _split2(x):
    hi = x.astype(BF16)
    return hi, (x - hi.astype(F32)).astype(BF16)


def _mm_data_const(x, cb):
    hi, mid = _split2(x)
    m = x.shape[0]
    r = lax.dot_general(jnp.concatenate([hi, mid], axis=0), cb, NN, preferred_element_type=F32)
    return r[:m] + r[m:]


def _mm_const_data(cb, x):
    hi, mid = _split2(x)
    n = x.shape[1]
    r = lax.dot_general(cb, jnp.concatenate([hi, mid], axis=1), NN, preferred_element_type=F32)
    return r[:, :n] + r[:, n:]


def _silu(x):
    return x * jax.nn.sigmoid(x)


def _softplus(x):
    return jnp.maximum(x, 0.0) + jnp.log1p(jnp.exp(-jnp.abs(x)))


def _log_sigmoid(x):
    return jnp.minimum(x, 0.0) - jnp.log1p(jnp.exp(-jnp.abs(x)))


def _rms_rows(x, w):
    ms = jnp.mean(x * x, axis=-1, keepdims=True)
    return x * lax.rsqrt(ms + EPS) * w


def _stack_bd(x, width):
    lane = lax.broadcasted_iota(jnp.int32, x.shape, 1) // width
    zero = jnp.zeros_like(x)
    return jnp.concatenate([jnp.where(lane == h, x, zero) for h in range(NH)], axis=0)


def _col_to_compact(col):
    lane = lax.broadcasted_iota(jnp.int32, (C, R), 1) // C
    out = jnp.zeros((C, R), F32)
    for h in range(NH):
        out = jnp.where(lane == h, col[h * C:(h + 1) * C], out)
    return out


def _ffn_kernel(*refs, first, last, npb, tf):
    n_x = 2 if first else 1
    xrefs, (nw_ref, w1_ref, w3_ref, w2_ref, fw_ref) = refs[:n_x], refs[n_x:n_x + 5]
    n_o = 2 if last else 1
    orefs = refs[n_x + 5:n_x + 5 + n_o]
    h_ref, acc_ref = refs[n_x + 5 + n_o:]
    i = pl.program_id(0)
    x = jnp.where(i < npb, xrefs[0][...], xrefs[1][...]) if first else xrefs[0][...]
    h_ref[...] = _rms_rows(x, nw_ref[...]).astype(BF16)
    for j in range(D_FF // tf):
        h = h_ref[...]
        a = jnp.dot(h, w1_ref[:, j * tf:(j + 1) * tf], preferred_element_type=F32)
        g = jnp.dot(h, w3_ref[:, j * tf:(j + 1) * tf], preferred_element_type=F32)
        p = jnp.dot((_silu(a) * g).astype(BF16), w2_ref[j * tf:(j + 1) * tf, :], preferred_element_type=F32)
        if j == 0:
            acc_ref[...] = p
        else:
            acc_ref[...] += p
    y = x + 0.5 * acc_ref[...]
    if last:
        y = _rms_rows(y, fw_ref[...])

        @pl.when(i < npb)
        def _():
            orefs[0][...] = y

        @pl.when(i >= npb)
        def _():
            orefs[1][...] = y
    else:
        orefs[0][...] = y


def _resident(a):
    nd = a.ndim
    return pl.BlockSpec(a.shape, lambda *_, _n=nd: (0,) * _n, pipeline_mode=pl.Buffered(1))


def _layer_resident(a, l):
    nd = a.ndim - 1
    return pl.BlockSpec((None,) + a.shape[1:], lambda *_, _n=nd: (l,) + (0,) * _n, pipeline_mode=pl.Buffered(1))


def _ffn(xs, l, nw, w1, w3, w2, fw, *, first, last, n_prompt, n_sample, tm=FFN_TM, tf=FFN_TF):
    assert n_sample == tm and n_prompt % tm == 0
    npb = n_prompt // tm
    prompt_blk = pl.BlockSpec((tm, D_MODEL), lambda i: (jnp.minimum(i, npb - 1), 0))
    sample_blk = pl.BlockSpec((tm, D_MODEL), lambda i: (0, 0))
    unified_blk = pl.BlockSpec((tm, D_MODEL), lambda i: (i, 0))
    if last:
        out_specs = (prompt_blk, sample_blk)
        out_shape = (jax.ShapeDtypeStruct((n_prompt, D_MODEL), F32), jax.ShapeDtypeStruct((n_sample, D_MODEL), F32))
    else:
        out_specs = unified_blk
        out_shape = jax.ShapeDtypeStruct((n_prompt + n_sample, D_MODEL), F32)
    return pl.pallas_call(
        functools.partial(_ffn_kernel, first=first, last=last, npb=npb, tf=tf),
        grid=(npb + 1,),
        in_specs=([prompt_blk, sample_blk] if first else [unified_blk]) + [
            _layer_resident(nw, l), _layer_resident(w1, l), _layer_resident(w3, l), _layer_resident(w2, l),
            _resident(fw)],
        out_specs=out_specs,
        out_shape=out_shape,
        scratch_shapes=[pltpu.VMEM((tm, D_MODEL), BF16), pltpu.VMEM((tm, D_MODEL), F32)],
        compiler_params=pltpu.CompilerParams(
            dimension_semantics=("arbitrary",), vmem_limit_bytes=VMEM_LIMIT),
        name="ffn",
    )(*xs, nw, w1, w3, w2, fw)


def _levels(L):
    return [s for s in (1, 2, 4, 8, 16, 32) if s < L]


def _chunk_consts(L):
    i = np.arange(C)
    sid, p = i // L, i % L
    same = sid[:, None] == sid[None, :]
    causal = same & (i[:, None] >= i[None, :])
    strict = same & (i[:, None] > i[None, :])
    lv = _levels(L)
    cum, pm, lm = [causal, same], [np.eye(C, dtype=bool)], []
    for s in lv:
        blk = p // s
        inblk = same & (blk[:, None] == blk[None, :])
        odd = (blk % 2 == 1)[:, None]
        cum.append(inblk & np.where(odd, i[None, :] <= i[:, None], i[None, :] > i[:, None]))
        pm.append(same & odd & (blk[None, :] == blk[:, None] - 1))
        lm.append(strict & ((p[:, None] // (2 * s)) == (p[None, :] // (2 * s))) & (blk[:, None] != blk[None, :]))
    r = np.arange(R)
    bdfull = (r[:, None] // C) == (r[None, :] // C)
    wide = lambda m: np.tile(m, (1, NH))
    f = lambda m: jnp.asarray(np.asarray(m, dtype=np.float32))
    gla_bd = (np.arange(NH * GLA_DK)[:, None] // GLA_DK) == (np.arange(NH * GLA_DV)[None, :] // GLA_DV)
    seqsel = np.zeros((C, LANES), np.float32)
    seqsel[i, sid] = 1.0
    return dict(
        cumg=f(np.concatenate(cum, axis=0)).astype(BF16),
        cumd=f(np.concatenate(cum[:2], axis=0)).astype(BF16),
        pmc=f(np.stack([wide(m) for m in pm])),
        lmc=f(np.stack([wide(m) for m in lm])),
        causal4=f(wide(causal)), eye4=f(wide(np.eye(C, dtype=bool))),
        g64=(f(bdfull) * (1.0 / 64.0)).astype(BF16), bdfull=f(bdfull), glabd=f(gla_bd),
        seqsel=jnp.asarray(seqsel).astype(BF16),
    )


def _ret_consts(L):
    log_gamma = jnp.log(1.0 - 2.0 ** (-5.0 - jnp.arange(NH, dtype=F32)))
    i = np.arange(C)
    sid, p = i // L, (i % L).astype(np.float32)
    causal = (sid[:, None] == sid[None, :]) & (i[:, None] >= i[None, :])
    diff = jnp.asarray(np.where(causal, p[:, None] - p[None, :], 0.0).astype(np.float32))
    dec = jnp.where(causal[None], jnp.exp(diff[None] * log_gamma[:, None, None]), 0.0)
    lg_l = jnp.repeat(log_gamma, RET_DV)[None, :]
    pj = jnp.asarray(p)[:, None]
    return dict(
        decc=jnp.concatenate([dec[h] for h in range(NH)], axis=1),
        dq=jnp.exp((pj + 1.0) * lg_l),
        dk=jnp.exp((L - 1.0 - pj) * lg_l),
        dch=jnp.exp(L * lg_l),
    )


def _rope_tables(pos):
    half = RET_DK // 2
    inv = ROPE_BASE ** (-jnp.arange(half, dtype=F32) / half)
    ang = pos.astype(F32)[:, None] * inv[None, :]
    cos, sin = jnp.cos(ang), jnp.sin(ang)
    return (jnp.tile(jnp.concatenate([cos, cos], axis=1), (1, NH)),
            jnp.tile(jnp.concatenate([-sin, sin], axis=1), (1, NH)))


def _rotary(x, cos, sin):
    lane = lax.broadcasted_iota(jnp.int32, x.shape, 1)
    swapped = jnp.where((lane % RET_DK) < RET_DK // 2,
                        pltpu.roll(x, x.shape[1] - RET_DK // 2, 1), pltpu.roll(x, RET_DK // 2, 1))
    return x * cos + swapped * sin


def _ret_intra(z, cos, sin, k):
    q = _rotary(z[:, Z_RQ:Z_RQ + 256], cos, sin)
    kk = _rotary(z[:, Z_RK:Z_RK + 256], cos, sin) * (RET_DK ** -0.5)
    v = z[:, Z_RV:Z_RV + 256]
    sc = _mm(q, _stack_bd(kk.astype(BF16), RET_DK), NT) * k["decc"][...]
    o = _mm(sc, _stack_bd(v.astype(BF16), RET_DV))
    return o, q * k["dq"][...], kk * k["dk"][...], v


def _ret_out(o, rg, k):
    g = k["g64"][...]
    d = o - _mm_data_const(o, g)
    var = _mm_data_const(d * d, g)
    return _silu(rg) * (d * lax.rsqrt(var + GN_EPS) * k["ret_gn_w"][...] + k["ret_gn_b"][...])


def _gla_intra(z, k, nlev):
    sm = z[:, Z_SM:Z_SM + LANES]
    gk = _log_sigmoid(_mm(sm, k["gwp"][...]) + k["gla_gate_b"][...]) * (1.0 / GLA_GATE_NORM)
    q = z[:, Z_LQ:Z_LQ + 128] * (GLA_DK ** -0.5)
    kk = z[:, Z_LK:Z_LK + 128]
    v = z[:, Z_LV:Z_LV + 256]
    cs = _mm_const_data(k["cumg"][...], gk)
    b, blast = cs[:C], cs[C:2 * C]
    sc = _mm(q, _stack_bd(kk.astype(BF16), GLA_DK), NT) * k["pmc"][0]
    for li in range(nlev):
        e = jnp.exp(cs[(2 + li) * C:(3 + li) * C])
        sc = sc + _mm(q * e, _stack_bd((kk * e).astype(BF16), GLA_DK), NT) * k["pmc"][li + 1]
    o = _mm(sc, _stack_bd(v.astype(BF16), GLA_DV))
    return o, q * jnp.exp(b), kk * jnp.exp(blast - b), v, b, blast


def _gla_out(o, lg, k):
    ms = _mm_data_const(o * o, k["g64"][...])
    return o * lax.rsqrt(ms + EPS) * k["gla_norm_w"][...] * _silu(lg)


def _lane_col(blk, lane0):
    lane = lax.broadcasted_iota(jnp.int32, blk.shape, 1)
    st = jnp.concatenate([jnp.where(lane == lane0 + h, blk, 0.0) for h in range(NH)], axis=0)
    return jnp.sum(st, axis=-1, keepdims=True)


def _gdn_front(z, k):
    conv = z[:, Z_CONV:Z_CONV + CONV_DIM]
    sm = z[:, Z_SM:Z_SM + LANES]

    def l2n(x):
        return x * lax.rsqrt(jnp.sum(x * x, axis=-1, keepdims=True) + EPS)

    qs = [l2n(conv[:, h * 128:(h + 1) * 128]) * (GDN_DK ** -0.5) for h in range(NH)]
    ks = [l2n(conv[:, 512 + h * 128:512 + (h + 1) * 128]) for h in range(NH)]
    vst = jnp.concatenate([conv[:, 1024 + h * 128:1024 + (h + 1) * 128] for h in range(NH)], axis=0)
    qst = jnp.concatenate(qs, axis=0)
    kst = jnp.concatenate(ks, axis=0)
    kbd = _stack_bd(jnp.concatenate(ks, axis=1).astype(BF16), GDN_DK)
    gd = -jnp.exp(k["alog"][...]) * _softplus(sm + k["dtb"][...])
    beta = jax.nn.sigmoid(sm)
    cs = _mm_const_data(k["cumd"][...], gd)
    bcol = _lane_col(cs[:C], SM_DA)
    blcol = _lane_col(cs[C:], SM_DA)
    betacol = _lane_col(beta, SM_DB)
    bcolc = _col_to_compact(bcol)
    browc = jnp.sum(bcolc * k["eye4"][...], axis=0, keepdims=True)
    causal = k["causal4"][...] > 0.5
    decc = jnp.where(causal, jnp.exp(jnp.where(causal, bcolc - browc, 0.0)), 0.0)
    ac = _mm(jnp.concatenate(ks, axis=1), kbd, NT) * decc * _col_to_compact(betacol)
    qkc = _mm(jnp.concatenate(qs, axis=1), kbd, NT) * decc
    return dict(ac=ac, qkc=qkc, qst=qst, kst=kst, vst=vst, bcol=bcol, blcol=blcol, betacol=betacol)


def _gdn_inverse(acs, k, nlev):
    xs = [k["eye4"][...] - a * k["lmc"][0] for a in acs]
    for li in range(1, nlev):
        ms = [a * k["lmc"][li] for a in acs]
        ys = [_mm(m, _stack_bd(x.astype(BF16), C)) for m, x in zip(ms, xs)]
        xs = [x - _mm(x, _stack_bd(y.astype(BF16), C)) for x, y in zip(xs, ys)]
    return xs


def _gdn_back(f, xc):
    kb = f["kst"] * f["betacol"]
    rhs = jnp.concatenate([f["vst"] * f["betacol"], kb * jnp.exp(f["bcol"])], axis=1)
    sol = _mm(_stack_bd(xc.astype(BF16), C), rhs)
    return dict(qst=f["qst"], solv=sol[:, :GDN_DV], solk=sol[:, GDN_DV:],
                qk=_stack_bd(f["qkc"].astype(BF16), C), ebcol=jnp.exp(f["bcol"]),
                kdec=f["kst"] * jnp.exp(f["blcol"] - f["bcol"]), eblast=jnp.exp(f["blcol"]))


def _gdn_out(ost, dg, k):
    yst = _rms_rows(ost, k["gdn_norm_w"][...])
    y = jnp.concatenate([yst[h * C:(h + 1) * C] for h in range(NH)], axis=1)
    return y * _silu(dg)


_PARAM_NAMES = ("alog", "dtb", "gwp", "gla_gate_b", "ret_gn_w", "ret_gn_b", "gdn_norm_w", "gla_norm_w", "conv_w")
_CONST_NAMES = ("cumg", "cumd", "pmc", "lmc", "causal4", "eye4", "g64", "bdfull", "glabd", "seqsel",
                "decc", "dq", "dk", "dch")


def _mixer_prompt_kernel(*refs, tc, nlev):
    n_in = 6 + len(_PARAM_NAMES) + len(_CONST_NAMES)
    x_ref, nw_ref, win_ref, wout_ref, cos_ref, sin_ref = refs[:6]
    k = dict(zip(_PARAM_NAMES + _CONST_NAMES, refs[6:n_in]))
    o_ref, sret_ref, sgdn_ref, sgla_ref, conv_ref = refs[n_in:n_in + 5]
    z_ref, xp_ref, y_ref = refs[n_in + 5:]
    t = pl.program_id(1)
    nch = tc // C

    @pl.when(t == 0)
    def _():
        sret_ref[...] = jnp.zeros_like(sret_ref)
        sgdn_ref[...] = jnp.zeros_like(sgdn_ref)
        sgla_ref[...] = jnp.zeros_like(sgla_ref)
        xp_ref[pl.ds(0, 8), :] = jnp.zeros((8, CONV_DIM), F32)

    h = _rms_rows(x_ref[...], nw_ref[...]).astype(BF16)
    xp_ref[pl.ds(8, tc), :] = jnp.dot(h, win_ref[:, Z_CONV:Z_CONV + CONV_DIM], preferred_element_type=F32)
    z_ref[:, :Z_CONV] = jnp.dot(h, win_ref[:, :Z_CONV], preferred_element_type=F32)
    z_ref[:, Z_DG:] = jnp.dot(h, win_ref[:, Z_DG:], preferred_element_type=F32)

    cw = k["conv_w"]
    acc = xp_ref[pl.ds(8, tc), :] * cw[3:4, :]
    for i in range(CONV_W - 1):
        acc = acc + xp_ref[pl.ds(5 + i, tc), :] * cw[i:i + 1, :]
    z_ref[:, Z_CONV:Z_CONV + CONV_DIM] = _silu(acc)
    xp_ref[pl.ds(0, 8), :] = xp_ref[pl.ds(tc, 8), :]

    zs = [z_ref[pl.ds(c * C, C), :] for c in range(nch)]
    fronts = [_gdn_front(z, k) for z in zs]
    xcs = _gdn_inverse([f["ac"] for f in fronts], k, nlev)
    gs = [_gdn_back(f, xc) for f, xc in zip(fronts, xcs)]
    rets = [_ret_intra(z, cos_ref[pl.ds(c * C, C), :], sin_ref[pl.ds(c * C, C), :], k) for c, z in enumerate(zs)]
    glas = [_gla_intra(z, k, nlev) for z in zs]

    for c in range(nch):
        z = zs[c]

        o, qd, kd, v = rets[c]
        s = sret_ref[0]
        o = o + _mm(qd, s)
        sret_ref[0] = s * k["dch"][...] + _mm(kd, v, TN) * k["bdfull"][...]
        y_r = _ret_out(o, z[:, Z_RG:Z_RG + 256], k)

        g = gs[c]
        us, qss = [], []
        for hh in range(NH):
            lhs = jnp.concatenate([g["solk"][hh * C:(hh + 1) * C], g["qst"][hh * C:(hh + 1) * C]], axis=0)
            r = _mm(lhs, sgdn_ref[0, hh])
            us.append(g["solv"][hh * C:(hh + 1) * C] - r[:C])
            qss.append(r[C:])
        ost = g["ebcol"] * jnp.concatenate(qss, axis=0) + _mm(g["qk"], jnp.concatenate(us, axis=0))
        for hh in range(NH):
            sl = slice(hh * C, (hh + 1) * C)
            sgdn_ref[0, hh] = (g["eblast"][hh * C:hh * C + 1] * sgdn_ref[0, hh]
                               + _mm(g["kdec"][sl], us[hh], TN))
        y_d = _gdn_out(ost, z[:, Z_DG:Z_DG + 512], k)

        o, qe, ke, v, b, _ = glas[c]
        s = sgla_ref[0]
        o = o + _mm(qe, s)
        escale = jnp.exp(b[C - 8:, :].T[:, 7:8])
        sgla_ref[0] = s * escale + _mm(ke, v, TN) * k["glabd"][...]
        y_l = _gla_out(o, z[:, Z_LG:Z_LG + 256], k)

        y_ref[pl.ds(c * C, C), :] = jnp.concatenate([y_r, y_d, y_l], axis=1).astype(BF16)

    o_ref[...] = x_ref[...] + jnp.dot(y_ref[...], wout_ref[...], preferred_element_type=F32)

    @pl.when(t == pl.num_programs(1) - 1)
    def _():
        conv_ref[0] = xp_ref[pl.ds(5, 3), :]


def _mixer_prompt(x, l, nw, win, wout, cos, sin, params, consts, *, bsz, tlen, tc=MIX_TC):
    nlev = len(_levels(C))
    nt = tlen // tc
    in_specs = [
        pl.BlockSpec((tc, D_MODEL), lambda b, t: (b * nt + t, 0)),
        _layer_resident(nw, l), _layer_resident(win, l), _layer_resident(wout, l),
        pl.BlockSpec((tc, 256), lambda b, t: (t, 0)),
        pl.BlockSpec((tc, 256), lambda b, t: (t, 0)),
    ] + [_layer_resident(params[n], l) for n in _PARAM_NAMES] + [_resident(consts[n]) for n in _CONST_NAMES]
    extras = [params[n] for n in _PARAM_NAMES] + [consts[n] for n in _CONST_NAMES]
    out_shape = (
        jax.ShapeDtypeStruct(x.shape, F32),
        jax.ShapeDtypeStruct((bsz, 256, 256), F32),
        jax.ShapeDtypeStruct((bsz, NH, GDN_DK, GDN_DV), F32),
        jax.ShapeDtypeStruct((bsz, NH * GLA_DK, NH * GLA_DV), F32),
        jax.ShapeDtypeStruct((bsz, CONV_W - 1, CONV_DIM), F32),
    )
    out_specs = (
        pl.BlockSpec((tc, D_MODEL), lambda b, t: (b * nt + t, 0)),
        pl.BlockSpec((1, 256, 256), lambda b, t: (b, 0, 0)),
        pl.BlockSpec((1, NH, GDN_DK, GDN_DV), lambda b, t: (b, 0, 0, 0)),
        pl.BlockSpec((1, NH * GLA_DK, NH * GLA_DV), lambda b, t: (b, 0, 0)),
        pl.BlockSpec((1, CONV_W - 1, CONV_DIM), lambda b, t: (b, 0, 0)),
    )
    return pl.pallas_call(
        functools.partial(_mixer_prompt_kernel, tc=tc, nlev=nlev),
        grid=(bsz, nt),
        in_specs=in_specs,
        out_specs=out_specs,
        out_shape=out_shape,
        input_output_aliases={0: 0},
        scratch_shapes=[
            pltpu.VMEM((tc, NZ), F32),
            pltpu.VMEM((tc + 8, CONV_DIM), F32),
            pltpu.VMEM((tc, D_MODEL), BF16),
        ],
        compiler_params=pltpu.CompilerParams(
            dimension_semantics=("parallel", "arbitrary"), vmem_limit_bytes=VMEM_LIMIT),
        name="mixer_prompt",
    )(x, nw, win, wout, cos, sin, *extras)


SEQ_S = 4
NSEQ = C // SEQ_S


def _mixer_sample_kernel(*refs, nlev):
    n_in = 10 + len(_PARAM_NAMES) + len(_CONST_NAMES)
    (x_ref, nw_ref, win_ref, wout_ref, cos_ref, sin_ref,
     sret_in, sgdn_in, sgla_in, cbuf_ref) = refs[:10]
    k = dict(zip(_PARAM_NAMES + _CONST_NAMES, refs[10:n_in]))
    o_ref, sret_out, sgdn_out, sgla_out, cout_ref = refs[n_in + 3:n_in + 8]

    x = x_ref[...]
    h = _rms_rows(x, nw_ref[...]).astype(BF16)
    z = jnp.dot(h, win_ref[...], preferred_element_type=F32)

    rowi = lax.broadcasted_iota(jnp.int32, (C, 1), 0)
    tpos = rowi % SEQ_S

    xc = z[:, Z_CONV:Z_CONV + CONV_DIM]
    cb = cbuf_ref[...]
    cw = k["conv_w"]
    acc = xc * cw[3:4, :]
    for i in range(CONV_W - 1):
        cur = pltpu.roll(xc, 3 - i, 0)
        old = cb if i == 0 else pltpu.roll(cb, C - i, 0)
        acc = acc + jnp.where(tpos + i >= 3, cur, old) * cw[i:i + 1, :]
    cout_ref[...] = pltpu.roll(xc, C - 1, 0)
    z = jnp.concatenate([z[:, :Z_CONV], _silu(acc), z[:, Z_CONV + CONV_DIM:]], axis=1)

    seq_of_row = rowi // SEQ_S
    seq_of_lane = lax.broadcasted_iota(jnp.int32, (1, C), 1) // SEQ_S

    o, qd, kd, v = _ret_intra(z, cos_ref[...], sin_ref[...], k)
    dch = k["dch"][...]
    kdt = kd.T
    accs = [jnp.zeros((C, RET_DV), F32) for _ in range(NH)]
    for n in range(NSEQ):
        rm = seq_of_row == n
        lm = seq_of_lane == n
        for hh in range(NH):
            sl = slice(hh * RET_DK, (hh + 1) * RET_DK)
            s = sret_in[n, hh]
            accs[hh] = jnp.where(rm, _mm(qd[:, sl], s), accs[hh])
            sret_out[n, hh] = s * dch[:, sl] + _mm(jnp.where(lm, kdt[sl], 0.0), v[:, sl])
    o = o + jnp.concatenate(accs, axis=1)
    y_r = _ret_out(o, z[:, Z_RG:Z_RG + 256], k)

    f = _gdn_front(z, k)
    g = _gdn_back(f, _gdn_inverse([f["ac"]], k, nlev)[0])
    rowi2 = lax.broadcasted_iota(jnp.int32, (2 * C, 1), 0)
    seq_of_row2 = (rowi2 % C) // SEQ_S
    lhs = [jnp.concatenate([g["solk"][hh * C:(hh + 1) * C], g["qst"][hh * C:(hh + 1) * C]], axis=0)
           for hh in range(NH)]
    accs = [jnp.zeros((2 * C, GDN_DV), F32) for _ in range(NH)]
    for n in range(NSEQ):
        rm2 = seq_of_row2 == n
        for hh in range(NH):
            accs[hh] = jnp.where(rm2, _mm(lhs[hh], sgdn_in[n, hh]), accs[hh])
    us = [g["solv"][hh * C:(hh + 1) * C] - accs[hh][:C] for hh in range(NH)]
    ost = (g["ebcol"] * jnp.concatenate([accs[hh][C:] for hh in range(NH)], axis=0)
           + _mm(g["qk"], jnp.concatenate(us, axis=0)))
    kdts = [g["kdec"][hh * C:(hh + 1) * C].T for hh in range(NH)]
    for n in range(NSEQ):
        lm = seq_of_lane == n
        for hh in range(NH):
            scale = g["eblast"][hh * C + n * SEQ_S:hh * C + n * SEQ_S + 1]
            sgdn_out[n, hh] = scale * sgdn_in[n, hh] + _mm(jnp.where(lm, kdts[hh], 0.0), us[hh])
    y_d = _gdn_out(ost, z[:, Z_DG:Z_DG + 512], k)

    o, qe, ke, v, b, blast = _gla_intra(z, k, nlev)
    ket = ke.T
    hi, mid = _split2(blast * (1.0 / SEQ_S))
    blt2 = lax.dot_general(jnp.concatenate([hi, mid], axis=1), k["seqsel"][...], TN, preferred_element_type=F32)
    eblt = jnp.exp(blt2[:NH * GLA_DK] + blt2[NH * GLA_DK:])
    accs = [jnp.zeros((C, GLA_DV), F32) for _ in range(NH)]
    for n in range(NSEQ):
        rm = seq_of_row == n
        lm = seq_of_lane == n
        for hh in range(NH):
            sk = slice(hh * GLA_DK, (hh + 1) * GLA_DK)
            sv = slice(hh * GLA_DV, (hh + 1) * GLA_DV)
            s = sgla_in[n, hh]
            accs[hh] = jnp.where(rm, _mm(qe[:, sk], s), accs[hh])
            sgla_out[n, hh] = s * eblt[sk, n:n + 1] + _mm(jnp.where(lm, ket[sk], 0.0), v[:, sv])
    o = o + jnp.concatenate(accs, axis=1)
    y_l = _gla_out(o, z[:, Z_LG:Z_LG + 256], k)

    y = jnp.concatenate([y_r, y_d, y_l], axis=1).astype(BF16)
    o_ref[...] = x + jnp.dot(y, wout_ref[...], preferred_element_type=F32)


def _mixer_sample(x, l, nw, win, wout, cos, sin, sret, sgdn, sgla, cbuf, acc_ret, acc_gdn, acc_gla,
                  params, consts, *, row0):
    nb = sret.shape[1]
    n = nb * SEQ_S
    nlev = len(_levels(SEQ_S))
    blk0 = row0 // C

    def st_spec(dk, dv):
        return pl.BlockSpec((None, NSEQ, NH, dk, dv), lambda i: (l, i, 0, 0, 0))

    st_specs = [st_spec(RET_DK, RET_DV), st_spec(GDN_DK, GDN_DV), st_spec(GLA_DK, GLA_DV)]
    any_spec = pl.BlockSpec(memory_space=pl.ANY)
    in_specs = [
        pl.BlockSpec((C, D_MODEL), lambda i: (blk0 + i, 0)),
        _layer_resident(nw, l), _layer_resident(win, l), _layer_resident(wout, l), _resident(cos), _resident(sin),
    ] + st_specs + [pl.BlockSpec((C, CONV_DIM), lambda i: (i, 0))] + [
        _layer_resident(params[nm], l) for nm in _PARAM_NAMES] + [
        _resident(consts[nm]) for nm in _CONST_NAMES] + [any_spec] * 3
    extras = [params[nm] for nm in _PARAM_NAMES] + [consts[nm] for nm in _CONST_NAMES]
    n_in = len(in_specs)
    out_shape = (
        jax.ShapeDtypeStruct(x.shape, F32),
        jax.ShapeDtypeStruct(acc_ret.shape, F32),
        jax.ShapeDtypeStruct(acc_gdn.shape, F32),
        jax.ShapeDtypeStruct(acc_gla.shape, F32),
        jax.ShapeDtypeStruct((n, CONV_DIM), F32),
    )
    out_specs = (pl.BlockSpec((C, D_MODEL), lambda i: (blk0 + i, 0)),) + tuple(st_specs) + (
        pl.BlockSpec((C, CONV_DIM), lambda i: (i, 0)),)
    return pl.pallas_call(
        functools.partial(_mixer_sample_kernel, nlev=nlev),
        grid=(n // C,),
        in_specs=in_specs,
        out_specs=out_specs,
        out_shape=out_shape,
        input_output_aliases={0: 0, n_in - 3: 1, n_in - 2: 2, n_in - 1: 3},
        compiler_params=pltpu.CompilerParams(
            dimension_semantics=("parallel",), vmem_limit_bytes=VMEM_LIMIT),
        name="mixer_sample",
    )(x, nw, win, wout, cos, sin, sret, sgdn, sgla, cbuf, *extras, acc_ret, acc_gdn, acc_gla)


def _permute_w_in(w):
    pad = jnp.zeros(w.shape[:2] + (NZ - Z_SM - 24,), w.dtype)
    return jnp.concatenate(
        [w[..., 0:2560], w[..., 2568:3080], w[..., 3080:3592], w[..., 3608:3864], w[..., 2560:2568],
         w[..., 3592:3608], pad], axis=-1).astype(BF16)


def _rows(v, width=None):
    v = v.astype(F32)[:, None, :]
    if width is not None and v.shape[-1] < width:
        v = jnp.pad(v, ((0, 0), (0, 0), (0, width - v.shape[-1])))
    return v


def kernel(x_prompt, x_sample, state_ret, state_gdn, state_gdn_conv, state_gla, norm_ffn1, ffn1_w1, ffn1_w3, ffn1_w2, norm_mix, w_in, ret_gn_w, ret_gn_b, gdn_conv_w, gdn_A_log, gdn_dt_bias, gdn_norm_w, gla_gate_w, gla_gate_b, gla_norm_w, w_out, norm_ffn2, ffn2_w1, ffn2_w3, ffn2_w2, norm_final):
    bp, tp, _ = x_prompt.shape
    bs, ts, _ = x_sample.shape
    n_prompt, n_sample = bp * tp, bs * ts
    assert ts == SEQ_S and tp % MIX_TC == 0 and n_sample % C == 0 and n_prompt % C == 0

    consts_p = dict(_chunk_consts(C), **_ret_consts(C))
    consts_s = dict(_chunk_consts(SEQ_S), **_ret_consts(SEQ_S))
    cos_p, sin_p = _rope_tables(jnp.arange(tp, dtype=jnp.int32))
    cos_s, sin_s = _rope_tables(PAST_LEN + (jnp.arange(C, dtype=jnp.int32) % SEQ_S))

    params = dict(
        alog=_rows(gdn_A_log, LANES), dtb=_rows(gdn_dt_bias, LANES),
        gwp=jnp.pad(gla_gate_w, ((0, 0), (SM_LLR, LANES - SM_LLR - GLA_RANK), (0, 0))).astype(BF16),
        gla_gate_b=_rows(gla_gate_b), ret_gn_w=_rows(ret_gn_w), ret_gn_b=_rows(ret_gn_b),
        gdn_norm_w=_rows(gdn_norm_w), gla_norm_w=_rows(jnp.tile(gla_norm_w, (1, NH))),
        conv_w=gdn_conv_w.astype(F32),
    )
    win = _permute_w_in(w_in)
    wout = w_out.astype(BF16)
    f1 = (_rows(norm_ffn1), ffn1_w1.astype(BF16), ffn1_w3.astype(BF16), ffn1_w2.astype(BF16))
    f2 = (_rows(norm_ffn2), ffn2_w1.astype(BF16), ffn2_w3.astype(BF16), ffn2_w2.astype(BF16))
    nw = _rows(norm_mix)
    fw = norm_final.astype(F32)[None, :]
    sret_in, sgdn_in, sgla_in = state_ret.astype(F32), state_gdn.astype(F32), state_gla.astype(F32)
    cbuf = jnp.pad(state_gdn_conv.astype(F32), ((0, 0), (0, 0), (0, 1), (0, 0))).reshape(DEPTH, n_sample, CONV_DIM)
    acc_ret, acc_gdn, acc_gla = (jnp.zeros(s.shape, F32) for s in (sret_in, sgdn_in, sgla_in))

    ffn = functools.partial(_ffn, n_prompt=n_prompt, n_sample=n_sample)
    xs = (x_prompt.reshape(n_prompt, D_MODEL), x_sample.reshape(n_sample, D_MODEL))
    outs_p = [[] for _ in range(4)]
    conv_s = []
    for l in range(DEPTH):
        x = ffn(xs, l, *f1, fw, first=(l == 0), last=False)
        x, sret, sgdn, sgla, conv = _mixer_prompt(x, l, nw, win, wout, cos_p, sin_p, params, consts_p,
                                                  bsz=bp, tlen=tp)
        outs_p[0].append(jnp.stack([sret[:, h * 64:(h + 1) * 64, h * 64:(h + 1) * 64] for h in range(NH)], axis=1))
        outs_p[1].append(sgdn)
        outs_p[2].append(conv)
        outs_p[3].append(jnp.stack([sgla[:, h * 32:(h + 1) * 32, h * 64:(h + 1) * 64] for h in range(NH)], axis=1))
        x, acc_ret, acc_gdn, acc_gla, cout = _mixer_sample(
            x, l, nw, win, wout, cos_s, sin_s, sret_in, sgdn_in, sgla_in, cbuf[l], acc_ret, acc_gdn, acc_gla,
            params, consts_s, row0=n_prompt)
        conv_s.append(cout.reshape(bs, ts, CONV_DIM)[:, :CONV_W - 1])
        xs = (ffn((x,), l, *f2, fw, first=False, last=(l == DEPTH - 1)),)

    y_prompt, y_sample = xs[0]
    dts = (state_ret.dtype, state_gdn.dtype, state_gdn_conv.dtype, state_gla.dtype)
    sp = [jnp.stack(o).astype(d) for o, d in zip(outs_p, dts)]
    return (y_prompt.reshape(bp, tp, D_MODEL), y_sample.reshape(bs, ts, D_MODEL), sp[0], sp[1], sp[2], sp[3],
            acc_ret.astype(dts[0]), acc_gdn.astype(dts[1]), jnp.stack(conv_s).astype(dts[2]), acc_gla.astype(dts[3]))
```

```python
import functools

import numpy as np
import jax
import jax.numpy as jnp
from jax import lax
from jax.experimental import pallas as pl
from jax.experimental.pallas import tpu as pltpu

F32 = jnp.float32
BF16 = jnp.bfloat16

D_MODEL = 1024
DEPTH = 4
PAST_LEN = 16384
NH = 4
RET_DK = 64
RET_DV = 64
GDN_DK = 128
GDN_DV = 128
GLA_DK = 32
GLA_DV = 64
GLA_RANK = 16
GLA_GATE_NORM = 16.0
CONV_W = 4
CONV_DIM = NH * (2 * GDN_DK + GDN_DV)
D_FF = 2816
ROPE_BASE = 10000.0
EPS = 1e-6
GN_EPS = 1e-5

C = 64
R = NH * C
LANES = 128

Z_RQ, Z_RK, Z_RV, Z_RG = 0, 256, 512, 768
Z_CONV = 1024
Z_DG = 2560
Z_LQ, Z_LK, Z_LV, Z_LG = 3072, 3200, 3328, 3584
Z_SM = 3840
NZ = 3968
SM_DA, SM_DB, SM_LLR = 0, 4, 8

NN = (((1,), (0,)), ((), ()))
NT = (((1,), (1,)), ((), ()))
TN = (((0,), (0,)), ((), ()))

VMEM_LIMIT = 56 * 1024 * 1024
FFN_TM = 512
FFN_TF = 256
MIX_TC = 512


def _mm(a, b, dims=NN):
    return lax.dot_general(a.astype(BF16), b.astype(BF16), dims, preferred_element_type=F32)


def _split2(x):
    hi = x.astype(BF16)
    return hi, (x - hi.astype(F32)).astype(BF16)


def _mm_data_const(x, cb):
    hi, mid = _split2(x)
    m = x.shape[0]
    r = lax.dot_general(jnp.concatenate([hi, mid], axis=0), cb, NN, preferred_element_type=F32)
    return r[:m] + r[m:]


def _mm_const_data(cb, x):
    hi, mid = _split2(x)
    n = x.shape[1]
    r = lax.dot_general(cb, jnp.concatenate([hi, mid], axis=1), NN, preferred_element_type=F32)
    return r[:, :n] + r[:, n:]


def _silu(x):
    return x * jax.nn.sigmoid(x)


def _softplus(x):
    return jnp.maximum(x, 0.0) + jnp.log1p(jnp.exp(-jnp.abs(x)))


def _log_sigmoid(x):
    return jnp.minimum(x, 0.0) - jnp.log1p(jnp.exp(-jnp.abs(x)))


def _rms_rows(x, w):
    ms = jnp.mean(x * x, axis=-1, keepdims=True)
    return x * lax.rsqrt(ms + EPS) * w


def _stack_bd(x, width):
    lane = lax.broadcasted_iota(jnp.int32, x.shape, 1) // width
    zero = jnp.zeros_like(x)
    return jnp.concatenate([jnp.where(lane == h, x, zero) for h in range(NH)], axis=0)


def _col_to_compact(col):
    lane = lax.broadcasted_iota(jnp.int32, (C, R), 1) // C
    out = jnp.zeros((C, R), F32)
    for h in range(NH):
        out = jnp.where(lane == h, col[h * C:(h + 1) * C], out)
    return out


def _ffn_kernel(*refs, first, last, npb, tf):
    n_x = 2 if first else 1
    xrefs, (nw_ref, w1_ref, w3_ref, w2_ref, fw_ref) = refs[:n_x], refs[n_x:n_x + 5]
    n_o = 2 if last else 1
    orefs = refs[n_x + 5:n_x + 5 + n_o]
    h_ref, acc_ref = refs[n_x + 5 + n_o:]
    i = pl.program_id(0)
    x = jnp.where(i < npb, xrefs[0][...], xrefs[1][...]) if first else xrefs[0][...]
    h_ref[...] = _rms_rows(x, nw_ref[...]).astype(BF16)
    for j in range(D_FF // tf):
        h = h_ref[...]
        a = jnp.dot(h, w1_ref[:, j * tf:(j + 1) * tf], preferred_element_type=F32)
        g = jnp.dot(h, w3_ref[:, j * tf:(j + 1) * tf], preferred_element_type=F32)
        p = jnp.dot((_silu(a) * g).astype(BF16), w2_ref[j * tf:(j + 1) * tf, :], preferred_element_type=F32)
        if j == 0:
            acc_ref[...] = p
        else:
            acc_ref[...] += p
    y = x + 0.5 * acc_ref[...]
    if last:
        y = _rms_rows(y, fw_ref[...])

        @pl.when(i < npb)
        def _():
            orefs[0][...] = y

        @pl.when(i >= npb)
        def _():
            orefs[1][...] = y
    else:
        orefs[0][...] = y


def _resident(a):
    nd = a.ndim
    return pl.BlockSpec(a.shape, lambda *_, _n=nd: (0,) * _n, pipeline_mode=pl.Buffered(1))


def _layer_resident(a, l):
    nd = a.ndim - 1
    return pl.BlockSpec((None,) + a.shape[1:], lambda *_, _n=nd: (l,) + (0,) * _n, pipeline_mode=pl.Buffered(1))


def _ffn(xs, l, nw, w1, w3, w2, fw, *, first, last, n_prompt, n_sample, tm=FFN_TM, tf=FFN_TF):
    assert n_sample == tm and n_prompt % tm == 0
    npb = n_prompt // tm
    prompt_blk = pl.BlockSpec((tm, D_MODEL), lambda i: (jnp.minimum(i, npb - 1), 0))
    sample_blk = pl.BlockSpec((tm, D_MODEL), lambda i: (0, 0))
    unified_blk = pl.BlockSpec((tm, D_MODEL), lambda i: (i, 0))
    if last:
        out_specs = (prompt_blk, sample_blk)
        out_shape = (jax.ShapeDtypeStruct((n_prompt, D_MODEL), F32), jax.ShapeDtypeStruct((n_sample, D_MODEL), F32))
    else:
        out_specs = unified_blk
        out_shape = jax.ShapeDtypeStruct((n_prompt + n_sample, D_MODEL), F32)
    return pl.pallas_call(
        functools.partial(_ffn_kernel, first=first, last=last, npb=npb, tf=tf),
        grid=(npb + 1,),
        in_specs=([prompt_blk, sample_blk] if first else [unified_blk]) + [
            _layer_resident(nw, l), _layer_resident(w1, l), _layer_resident(w3, l), _layer_resident(w2, l),
            _resident(fw)],
        out_specs=out_specs,
        out_shape=out_shape,
        scratch_shapes=[pltpu.VMEM((tm, D_MODEL), BF16), pltpu.VMEM((tm, D_MODEL), F32)],
        compiler_params=pltpu.CompilerParams(
            dimension_semantics=("arbitrary",), vmem_limit_bytes=VMEM_LIMIT),
        name="ffn",
    )(*xs, nw, w1, w3, w2, fw)


def _levels(L):
    return [s for s in (1, 2, 4, 8, 16, 32) if s < L]


def _chunk_consts(L):
    i = np.arange(C)
    sid, p = i // L, i % L
    same = sid[:, None] == sid[None, :]
    causal = same & (i[:, None] >= i[None, :])
    strict = same & (i[:, None] > i[None, :])
    lv = _levels(L)
    cum, pm, lm = [causal, same], [np.eye(C, dtype=bool)], []
    for s in lv:
        blk = p // s
        inblk = same & (blk[:, None] == blk[None, :])
        odd = (blk % 2 == 1)[:, None]
        cum.append(inblk & np.where(odd, i[None, :] <= i[:, None], i[None, :] > i[:, None]))
        pm.append(same & odd & (blk[None, :] == blk[:, None] - 1))
        lm.append(strict & ((p[:, None] // (2 * s)) == (p[None, :] // (2 * s))) & (blk[:, None] != blk[None, :]))
    r = np.arange(R)
    bdfull = (r[:, None] // C) == (r[None, :] // C)
    wide = lambda m: np.tile(m, (1, NH))
    f = lambda m: jnp.asarray(np.asarray(m, dtype=np.float32))
    gla_bd = (np.arange(NH * GLA_DK)[:, None] // GLA_DK) == (np.arange(NH * GLA_DV)[None, :] // GLA_DV)
    seqsel = np.zeros((C, LANES), np.float32)
    seqsel[i, sid] = 1.0
    return dict(
        cumg=f(np.concatenate(cum, axis=0)).astype(BF16),
        cumd=f(np.concatenate(cum[:2], axis=0)).astype(BF16),
        pmc=f(np.stack([wide(m) for m in pm])),
        lmc=f(np.stack([wide(m) for m in lm])),
        causal4=f(wide(causal)), eye4=f(wide(np.eye(C, dtype=bool))),
        g64=(f(bdfull) * (1.0 / 64.0)).astype(BF16), bdfull=f(bdfull), glabd=f(gla_bd),
        seqsel=jnp.asarray(seqsel).astype(BF16),
    )


def _ret_consts(L):
    log_gamma = jnp.log(1.0 - 2.0 ** (-5.0 - jnp.arange(NH, dtype=F32)))
    i = np.arange(C)
    sid, p = i // L, (i % L).astype(np.float32)
    causal = (sid[:, None] == sid[None, :]) & (i[:, None] >= i[None, :])
    diff = jnp.asarray(np.where(causal, p[:, None] - p[None, :], 0.0).astype(np.float32))
    dec = jnp.where(causal[None], jnp.exp(diff[None] * log_gamma[:, None, None]), 0.0)
    lg_l = jnp.repeat(log_gamma, RET_DV)[None, :]
    pj = jnp.asarray(p)[:, None]
    return dict(
        decc=jnp.concatenate([dec[h] for h in range(NH)], axis=1),
        dq=jnp.exp((pj + 1.0) * lg_l),
        dk=jnp.exp((L - 1.0 - pj) * lg_l),
        dch=jnp.exp(L * lg_l),
    )


def _rope_tables(pos):
    half = RET_DK // 2
    inv = ROPE_BASE ** (-jnp.arange(half, dtype=F32) / half)
    ang = pos.astype(F32)[:, None] * inv[None, :]
    cos, sin = jnp.cos(ang), jnp.sin(ang)
    return (jnp.tile(jnp.concatenate([cos, cos], axis=1), (1, NH)),
            jnp.tile(jnp.concatenate([-sin, sin], axis=1), (1, NH)))


def _rotary(x, cos, sin):
    lane = lax.broadcasted_iota(jnp.int32, x.shape, 1)
    swapped = jnp.where((lane % RET_DK) < RET_DK // 2,
                        pltpu.roll(x, x.shape[1] - RET_DK // 2, 1), pltpu.roll(x, RET_DK // 2, 1))
    return x * cos + swapped * sin


def _ret_intra(z, cos, sin, k):
    q = _rotary(z[:, Z_RQ:Z_RQ + 256], cos, sin)
    kk = _rotary(z[:, Z_RK:Z_RK + 256], cos, sin) * (RET_DK ** -0.5)
    v = z[:, Z_RV:Z_RV + 256]
    sc = _mm(q, _stack_bd(kk.astype(BF16), RET_DK), NT) * k["decc"][...]
    o = _mm(sc, _stack_bd(v.astype(BF16), RET_DV))
    return o, q * k["dq"][...], kk * k["dk"][...], v


def _ret_out(o, rg, k):
    g = k["g64"][...]
    d = o - _mm_data_const(o, g)
    var = _mm_data_const(d * d, g)
    return _silu(rg) * (d * lax.rsqrt(var + GN_EPS) * k["ret_gn_w"][...] + k["ret_gn_b"][...])


def _gla_intra(z, k, nlev):
    sm = z[:, Z_SM:Z_SM + LANES]
    gk = _log_sigmoid(_mm(sm, k["gwp"][...]) + k["gla_gate_b"][...]) * (1.0 / GLA_GATE_NORM)
    q = z[:, Z_LQ:Z_LQ + 128] * (GLA_DK ** -0.5)
    kk = z[:, Z_LK:Z_LK + 128]
    v = z[:, Z_LV:Z_LV + 256]
    cs = _mm_const_data(k["cumg"][...], gk)
    b, blast = cs[:C], cs[C:2 * C]
    sc = _mm(q, _stack_bd(kk.astype(BF16), GLA_DK), NT) * k["pmc"][0]
    for li in range(nlev):
        e = jnp.exp(cs[(2 + li) * C:(3 + li) * C])
        sc = sc + _mm(q * e, _stack_bd((kk * e).astype(BF16), GLA_DK), NT) * k["pmc"][li + 1]
    o = _mm(sc, _stack_bd(v.astype(BF16), GLA_DV))
    return o, q * jnp.exp(b), kk * jnp.exp(blast - b), v, b, blast


def _gla_out(o, lg, k):
    ms = _mm_data_const(o * o, k["g64"][...])
    return o * lax.rsqrt(ms + EPS) * k["gla_norm_w"][...] * _silu(lg)


def _lane_col(blk, lane0):
    lane = lax.broadcasted_iota(jnp.int32, blk.shape, 1)
    st = jnp.concatenate([jnp.where(lane == lane0 + h, blk, 0.0) for h in range(NH)], axis=0)
    return jnp.sum(st, axis=-1, keepdims=True)


def _gdn_front(conv, sm, k):
    def l2n(x):
        return x * lax.rsqrt(jnp.sum(x * x, axis=-1, keepdims=True) + EPS)

    qs = [l2n(conv[h]) * (GDN_DK ** -0.5) for h in range(NH)]
    ks = [l2n(conv[NH + h]) for h in range(NH)]
    vst = jnp.concatenate(conv[2 * NH:], axis=0)
    qst = jnp.concatenate(qs, axis=0)
    kst = jnp.concatenate(ks, axis=0)
    kq = []
    for p in range(NH // 2):
        kp = jnp.concatenate(ks[2 * p:2 * p + 2], axis=1).astype(BF16)
        lane = lax.broadcasted_iota(jnp.int32, kp.shape, 1) // GDN_DK
        kbd = jnp.concatenate([jnp.where(lane == hl, kp, jnp.zeros_like(kp)) for hl in range(2)], axis=0)
        lhs = jnp.concatenate([kp, jnp.concatenate(qs[2 * p:2 * p + 2], axis=1).astype(BF16)], axis=0)
        kq.append(lax.dot_general(lhs, kbd, NT, preferred_element_type=F32))
    kq = jnp.concatenate(kq, axis=1)
    gd = -jnp.exp(k["alog"][...]) * _softplus(sm + k["dtb"][...])
    beta = jax.nn.sigmoid(sm)
    cs = _mm_const_data(k["cumd"][...], gd)
    bcol = _lane_col(cs[:C], SM_DA)
    blcol = _lane_col(cs[C:], SM_DA)
    betacol = _lane_col(beta, SM_DB)
    bcolc = _col_to_compact(bcol)
    browc = jnp.sum(bcolc * k["eye4"][...], axis=0, keepdims=True)
    causal = k["causal4"][...] > 0.5
    decc = jnp.where(causal, jnp.exp(jnp.where(causal, bcolc - browc, 0.0)), 0.0)
    ac = kq[:C] * decc * _col_to_compact(betacol)
    qkc = kq[C:] * decc
    return dict(ac=ac, qkc=qkc, qst=qst, kst=kst, vst=vst, bcol=bcol, blcol=blcol, betacol=betacol)


def _gdn_inverse(acs, k, nlev):
    xs = [k["eye4"][...] - a * k["lmc"][0] for a in acs]
    for li in range(1, nlev):
        ms = [a * k["lmc"][li] for a in acs]
        ys = [_mm(m, _stack_bd(x.astype(BF16), C)) for m, x in zip(ms, xs)]
        xs = [x - _mm(x, _stack_bd(y.astype(BF16), C)) for x, y in zip(xs, ys)]
    return xs


def _gdn_back(f, xc):
    kb = f["kst"] * f["betacol"]
    rhs = jnp.concatenate([f["vst"] * f["betacol"], kb * jnp.exp(f["bcol"])], axis=1)
    sol = _mm(_stack_bd(xc.astype(BF16), C), rhs)
    return dict(qst=f["qst"], solv=sol[:, :GDN_DV], solk=sol[:, GDN_DV:],
                qk=_stack_bd(f["qkc"].astype(BF16), C), ebcol=jnp.exp(f["bcol"]),
                kdec=f["kst"] * jnp.exp(f["blcol"] - f["bcol"]), eblast=jnp.exp(f["blcol"]))


def _gdn_out(o, dg, k):
    w = k["gdn_norm_w"][...]
    y = jnp.concatenate([_rms_rows(o[:, h * GDN_DV:(h + 1) * GDN_DV], w) for h in range(NH)], axis=1)
    return y * _silu(dg)


_PARAM_NAMES = ("alog", "dtb", "gwp", "gla_gate_b", "ret_gn_w", "ret_gn_b", "gdn_norm_w", "gla_norm_w", "conv_w")
_CONST_NAMES = ("cumg", "cumd", "pmc", "lmc", "causal4", "eye4", "g64", "bdfull", "glabd", "seqsel",
                "decc", "dq", "dk", "dch")


def _mixer_prompt_kernel(*refs, tc, nlev):
    n_in = 6 + len(_PARAM_NAMES) + len(_CONST_NAMES)
    x_ref, nw_ref, win_ref, wout_ref, cos_ref, sin_ref = refs[:6]
    k = dict(zip(_PARAM_NAMES + _CONST_NAMES, refs[6:n_in]))
    o_ref, sret_ref, sgdn_ref, sgla_ref, conv_ref = refs[n_in:n_in + 5]
    z_ref, xp_ref, cv_ref, y_ref = refs[n_in + 5:]
    t = pl.program_id(1)
    nch = tc // C
    ng = CONV_DIM // LANES

    @pl.when(t == 0)
    def _():
        sret_ref[...] = jnp.zeros_like(sret_ref)
        sgdn_ref[...] = jnp.zeros_like(sgdn_ref)
        sgla_ref[...] = jnp.zeros_like(sgla_ref)
        xp_ref[:, pl.ds(0, 8), :] = jnp.zeros((ng, 8, LANES), F32)

    h = _rms_rows(x_ref[...], nw_ref[...]).astype(BF16)
    zc = jnp.dot(h, win_ref[:, Z_CONV:Z_CONV + CONV_DIM], preferred_element_type=F32)
    for g in range(ng):
        xp_ref[g, pl.ds(8, tc), :] = zc[:, g * LANES:(g + 1) * LANES]
    z_ref[:, :Z_CONV] = jnp.dot(h, win_ref[:, :Z_CONV], preferred_element_type=F32)
    z_ref[:, Z_DG:] = jnp.dot(h, win_ref[:, Z_DG:], preferred_element_type=F32)

    cw = k["conv_w"]
    nb = tc // 8
    for g in range(ng):
        wg = [cw[i:i + 1, g * LANES:(g + 1) * LANES] for i in range(CONV_W)]
        taps = {s: xp_ref[g, pl.ds(s, nb, stride=8), :] for s in range(5, 5 + 8 + CONV_W - 1)}
        for j in range(8):
            acc = taps[j + 8] * wg[3]
            for i in range(CONV_W - 1):
                acc = acc + taps[j + 5 + i] * wg[i]
            cv_ref[g, pl.ds(j, nb, stride=8), :] = _silu(acc)
        xp_ref[g, pl.ds(0, 8), :] = xp_ref[g, pl.ds(tc, 8), :]

    zs = [z_ref[pl.ds(c * C, C), :] for c in range(nch)]
    fronts = [_gdn_front([cv_ref[g, pl.ds(c * C, C), :] for g in range(ng)], z[:, Z_SM:Z_SM + LANES], k)
              for c, z in enumerate(zs)]
    xcs = _gdn_inverse([f["ac"] for f in fronts], k, nlev)
    gs = [_gdn_back(f, xc) for f, xc in zip(fronts, xcs)]
    rets = [_ret_intra(z, cos_ref[pl.ds(c * C, C), :], sin_ref[pl.ds(c * C, C), :], k) for c, z in enumerate(zs)]
    glas = [_gla_intra(z, k, nlev) for z in zs]

    for c in range(nch):
        rows = pl.ds(c * C, C)

        o, qd, kd, v = rets[c]
        s = sret_ref[0]
        z_ref[rows, Z_RQ:Z_RQ + 256] = o + _mm(qd, s)
        sret_ref[0] = s * k["dch"][...] + _mm(kd, v, TN) * k["bdfull"][...]

        g = gs[c]
        us, qss = [], []
        for hh in range(NH):
            lhs = jnp.concatenate([g["solk"][hh * C:(hh + 1) * C], g["qst"][hh * C:(hh + 1) * C]], axis=0)
            r = _mm(lhs, sgdn_ref[0, hh])
            us.append(g["solv"][hh * C:(hh + 1) * C] - r[:C])
            qss.append(r[C:])
        ost = g["ebcol"] * jnp.concatenate(qss, axis=0) + _mm(g["qk"], jnp.concatenate(us, axis=0))
        for hh in range(NH):
            sl = slice(hh * C, (hh + 1) * C)
            sgdn_ref[0, hh] = (g["eblast"][hh * C:hh * C + 1] * sgdn_ref[0, hh]
                               + _mm(g["kdec"][sl], us[hh], TN))
        z_ref[rows, Z_CONV:Z_CONV + NH * GDN_DV] = jnp.concatenate(
            [ost[hh * C:(hh + 1) * C] for hh in range(NH)], axis=1)

        o, qe, ke, v, b, _ = glas[c]
        s = sgla_ref[0]
        z_ref[rows, Z_LQ:Z_LQ + 256] = o + _mm(qe, s)
        escale = jnp.exp(b[C - 8:, :].T[:, 7:8])
        sgla_ref[0] = s * escale + _mm(ke, v, TN) * k["glabd"][...]

    y_r = _ret_out(z_ref[:, Z_RQ:Z_RQ + 256], z_ref[:, Z_RG:Z_RG + 256], k)
    y_d = _gdn_out(z_ref[:, Z_CONV:Z_CONV + NH * GDN_DV], z_ref[:, Z_DG:Z_DG + 512], k)
    y_l = _gla_out(z_ref[:, Z_LQ:Z_LQ + 256], z_ref[:, Z_LG:Z_LG + 256], k)
    y_ref[...] = jnp.concatenate([y_r, y_d, y_l], axis=1).astype(BF16)

    o_ref[...] = x_ref[...] + jnp.dot(y_ref[...], wout_ref[...], preferred_element_type=F32)

    @pl.when(t == pl.num_programs(1) - 1)
    def _():
        for g in range(ng):
            conv_ref[0, :, g * LANES:(g + 1) * LANES] = xp_ref[g, pl.ds(5, 3), :]


def _mixer_prompt(x, l, nw, win, wout, cos, sin, params, consts, *, bsz, tlen, tc=MIX_TC):
    nlev = len(_levels(C))
    nt = tlen // tc
    in_specs = [
        pl.BlockSpec((tc, D_MODEL), lambda b, t: (b * nt + t, 0)),
        _layer_resident(nw, l), _layer_resident(win, l), _layer_resident(wout, l),
        pl.BlockSpec((tc, 256), lambda b, t: (t, 0)),
        pl.BlockSpec((tc, 256), lambda b, t: (t, 0)),
    ] + [_layer_resident(params[n], l) for n in _PARAM_NAMES] + [_resident(consts[n]) for n in _CONST_NAMES]
    extras = [params[n] for n in _PARAM_NAMES] + [consts[n] for n in _CONST_NAMES]
    out_shape = (
        jax.ShapeDtypeStruct(x.shape, F32),
        jax.ShapeDtypeStruct((bsz, 256, 256), F32),
        jax.ShapeDtypeStruct((bsz, NH, GDN_DK, GDN_DV), F32),
        jax.ShapeDtypeStruct((bsz, NH * GLA_DK, NH * GLA_DV), F32),
        jax.ShapeDtypeStruct((bsz, CONV_W - 1, CONV_DIM), F32),
    )
    out_specs = (
        pl.BlockSpec((tc, D_MODEL), lambda b, t: (b * nt + t, 0)),
        pl.BlockSpec((1, 256, 256), lambda b, t: (b, 0, 0)),
        pl.BlockSpec((1, NH, GDN_DK, GDN_DV), lambda b, t: (b, 0, 0, 0)),
        pl.BlockSpec((1, NH * GLA_DK, NH * GLA_DV), lambda b, t: (b, 0, 0)),
        pl.BlockSpec((1, CONV_W - 1, CONV_DIM), lambda b, t: (b, 0, 0)),
    )
    return pl.pallas_call(
        functools.partial(_mixer_prompt_kernel, tc=tc, nlev=nlev),
        grid=(bsz, nt),
        in_specs=in_specs,
        out_specs=out_specs,
        out_shape=out_shape,
        input_output_aliases={0: 0},
        scratch_shapes=[
            pltpu.VMEM((tc, NZ), F32),
            pltpu.VMEM((CONV_DIM // LANES, tc + 8, LANES), F32),
            pltpu.VMEM((CONV_DIM // LANES, tc, LANES), F32),
            pltpu.VMEM((tc, D_MODEL), BF16),
        ],
        compiler_params=pltpu.CompilerParams(
            dimension_semantics=("parallel", "arbitrary"), vmem_limit_bytes=VMEM_LIMIT),
        name="mixer_prompt",
    )(x, nw, win, wout, cos, sin, *extras)


SEQ_S = 4
NSEQ = C // SEQ_S


def _mixer_sample_kernel(*refs, nlev):
    n_in = 10 + len(_PARAM_NAMES) + len(_CONST_NAMES)
    (x_ref, nw_ref, win_ref, wout_ref, cos_ref, sin_ref,
     sret_in, sgdn_in, sgla_in, cbuf_ref) = refs[:10]
    k = dict(zip(_PARAM_NAMES + _CONST_NAMES, refs[10:n_in]))
    o_ref, sret_out, sgdn_out, sgla_out, cout_ref = refs[n_in + 3:n_in + 8]

    x = x_ref[...]
    h = _rms_rows(x, nw_ref[...]).astype(BF16)
    z = jnp.dot(h, win_ref[...], preferred_element_type=F32)

    rowi = lax.broadcasted_iota(jnp.int32, (C, 1), 0)
    tpos = rowi % SEQ_S

    xc = z[:, Z_CONV:Z_CONV + CONV_DIM]
    cb = cbuf_ref[...]
    cw = k["conv_w"]
    acc = xc * cw[3:4, :]
    for i in range(CONV_W - 1):
        cur = pltpu.roll(xc, 3 - i, 0)
        old = cb if i == 0 else pltpu.roll(cb, C - i, 0)
        acc = acc + jnp.where(tpos + i >= 3, cur, old) * cw[i:i + 1, :]
    cout_ref[...] = pltpu.roll(xc, C - 1, 0)
    conv = _silu(acc)

    seq_of_row = rowi // SEQ_S
    seq_of_lane = lax.broadcasted_iota(jnp.int32, (1, C), 1) // SEQ_S

    o, qd, kd, v = _ret_intra(z, cos_ref[...], sin_ref[...], k)
    dch = k["dch"][...]
    kdt = kd.T
    accs = [jnp.zeros((C, RET_DV), F32) for _ in range(NH)]
    for n in range(NSEQ):
        rm = seq_of_row == n
        lm = seq_of_lane == n
        for hh in range(NH):
            sl = slice(hh * RET_DK, (hh + 1) * RET_DK)
            s = sret_in[n, hh]
            accs[hh] = jnp.where(rm, _mm(qd[:, sl], s), accs[hh])
            sret_out[n, hh] = s * dch[:, sl] + _mm(jnp.where(lm, kdt[sl], 0.0), v[:, sl])
    o = o + jnp.concatenate(accs, axis=1)
    y_r = _ret_out(o, z[:, Z_RG:Z_RG + 256], k)

    f = _gdn_front([conv[:, gi * LANES:(gi + 1) * LANES] for gi in range(CONV_DIM // LANES)],
                   z[:, Z_SM:Z_SM + LANES], k)
    g = _gdn_back(f, _gdn_inverse([f["ac"]], k, nlev)[0])
    rowi2 = lax.broadcasted_iota(jnp.int32, (2 * C, 1), 0)
    seq_of_row2 = (rowi2 % C) // SEQ_S
    lhs = [jnp.concatenate([g["solk"][hh * C:(hh + 1) * C], g["qst"][hh * C:(hh + 1) * C]], axis=0)
           for hh in range(NH)]
    accs = [jnp.zeros((2 * C, GDN_DV), F32) for _ in range(NH)]
    for n in range(NSEQ):
        rm2 = seq_of_row2 == n
        for hh in range(NH):
            accs[hh] = jnp.where(rm2, _mm(lhs[hh], sgdn_in[n, hh]), accs[hh])
    us = [g["solv"][hh * C:(hh + 1) * C] - accs[hh][:C] for hh in range(NH)]
    ost = (g["ebcol"] * jnp.concatenate([accs[hh][C:] for hh in range(NH)], axis=0)
           + _mm(g["qk"], jnp.concatenate(us, axis=0)))
    kdts = [g["kdec"][hh * C:(hh + 1) * C].T for hh in range(NH)]
    for n in range(NSEQ):
        lm = seq_of_lane == n
        for hh in range(NH):
            scale = g["eblast"][hh * C + n * SEQ_S:hh * C + n * SEQ_S + 1]
            sgdn_out[n, hh] = scale * sgdn_in[n, hh] + _mm(jnp.where(lm, kdts[hh], 0.0), us[hh])
    y_d = _gdn_out(jnp.concatenate([ost[hh * C:(hh + 1) * C] for hh in range(NH)], axis=1),
                   z[:, Z_DG:Z_DG + 512], k)

    o, qe, ke, v, b, blast = _gla_intra(z, k, nlev)
    ket = ke.T
    hi, mid = _split2(blast * (1.0 / SEQ_S))
    blt2 = lax.dot_general(jnp.concatenate([hi, mid], axis=1), k["seqsel"][...], TN, preferred_element_type=F32)
    eblt = jnp.exp(blt2[:NH * GLA_DK] + blt2[NH * GLA_DK:])
    accs = [jnp.zeros((C, GLA_DV), F32) for _ in range(NH)]
    for n in range(NSEQ):
        rm = seq_of_row == n
        lm = seq_of_lane == n
        for hh in range(NH):
            sk = slice(hh * GLA_DK, (hh + 1) * GLA_DK)
            sv = slice(hh * GLA_DV, (hh + 1) * GLA_DV)
            s = sgla_in[n, hh]
            accs[hh] = jnp.where(rm, _mm(qe[:, sk], s), accs[hh])
            sgla_out[n, hh] = s * eblt[sk, n:n + 1] + _mm(jnp.where(lm, ket[sk], 0.0), v[:, sv])
    o = o + jnp.concatenate(accs, axis=1)
    y_l = _gla_out(o, z[:, Z_LG:Z_LG + 256], k)

    y = jnp.concatenate([y_r, y_d, y_l], axis=1).astype(BF16)
    o_ref[...] = x + jnp.dot(y, wout_ref[...], preferred_element_type=F32)


def _mixer_sample(x, l, nw, win, wout, cos, sin, sret, sgdn, sgla, cbuf, acc_ret, acc_gdn, acc_gla,
                  params, consts, *, row0):
    nb = sret.shape[1]
    n = nb * SEQ_S
    nlev = len(_levels(SEQ_S))
    blk0 = row0 // C

    def st_spec(dk, dv):
        return pl.BlockSpec((None, NSEQ, NH, dk, dv), lambda i: (l, i, 0, 0, 0))

    st_specs = [st_spec(RET_DK, RET_DV), st_spec(GDN_DK, GDN_DV), st_spec(GLA_DK, GLA_DV)]
    any_spec = pl.BlockSpec(memory_space=pl.ANY)
    in_specs = [
        pl.BlockSpec((C, D_MODEL), lambda i: (blk0 + i, 0)),
        _layer_resident(nw, l), _layer_resident(win, l), _layer_resident(wout, l), _resident(cos), _resident(sin),
    ] + st_specs + [pl.BlockSpec((C, CONV_DIM), lambda i: (i, 0))] + [
        _layer_resident(params[nm], l) for nm in _PARAM_NAMES] + [
        _resident(consts[nm]) for nm in _CONST_NAMES] + [any_spec] * 3
    extras = [params[nm] for nm in _PARAM_NAMES] + [consts[nm] for nm in _CONST_NAMES]
    n_in = len(in_specs)
    out_shape = (
        jax.ShapeDtypeStruct(x.shape, F32),
        jax.ShapeDtypeStruct(acc_ret.shape, F32),
        jax.ShapeDtypeStruct(acc_gdn.shape, F32),
        jax.ShapeDtypeStruct(acc_gla.shape, F32),
        jax.ShapeDtypeStruct((n, CONV_DIM), F32),
    )
    out_specs = (pl.BlockSpec((C, D_MODEL), lambda i: (blk0 + i, 0)),) + tuple(st_specs) + (
        pl.BlockSpec((C, CONV_DIM), lambda i: (i, 0)),)
    return pl.pallas_call(
        functools.partial(_mixer_sample_kernel, nlev=nlev),
        grid=(n // C,),
        in_specs=in_specs,
        out_specs=out_specs,
        out_shape=out_shape,
        input_output_aliases={0: 0, n_in - 3: 1, n_in - 2: 2, n_in - 1: 3},
        compiler_params=pltpu.CompilerParams(
            dimension_semantics=("parallel",), vmem_limit_bytes=VMEM_LIMIT),
        name="mixer_sample",
    )(x, nw, win, wout, cos, sin, sret, sgdn, sgla, cbuf, *extras, acc_ret, acc_gdn, acc_gla)


def _permute_w_in(w):
    pad = jnp.zeros(w.shape[:2] + (NZ - Z_SM - 24,), w.dtype)
    return jnp.concatenate(
        [w[..., 0:2560], w[..., 2568:3080], w[..., 3080:3592], w[..., 3608:3864], w[..., 2560:2568],
         w[..., 3592:3608], pad], axis=-1).astype(BF16)


def _rows(v, width=None):
    v = v.astype(F32)[:, None, :]
    if width is not None and v.shape[-1] < width:
        v = jnp.pad(v, ((0, 0), (0, 0), (0, width - v.shape[-1])))
    return v


def kernel(x_prompt, x_sample, state_ret, state_gdn, state_gdn_conv, state_gla, norm_ffn1, ffn1_w1, ffn1_w3, ffn1_w2, norm_mix, w_in, ret_gn_w, ret_gn_b, gdn_conv_w, gdn_A_log, gdn_dt_bias, gdn_norm_w, gla_gate_w, gla_gate_b, gla_norm_w, w_out, norm_ffn2, ffn2_w1, ffn2_w3, ffn2_w2, norm_final):
    bp, tp, _ = x_prompt.shape
    bs, ts, _ = x_sample.shape
    n_prompt, n_sample = bp * tp, bs * ts
    assert ts == SEQ_S and tp % MIX_TC == 0 and n_sample % C == 0 and n_prompt % C == 0

    consts_p = dict(_chunk_consts(C), **_ret_consts(C))
    consts_s = dict(_chunk_consts(SEQ_S), **_ret_consts(SEQ_S))
    cos_p, sin_p = _rope_tables(jnp.arange(tp, dtype=jnp.int32))
    cos_s, sin_s = _rope_tables(PAST_LEN + (jnp.arange(C, dtype=jnp.int32) % SEQ_S))

    params = dict(
        alog=_rows(gdn_A_log, LANES), dtb=_rows(gdn_dt_bias, LANES),
        gwp=jnp.pad(gla_gate_w, ((0, 0), (SM_LLR, LANES - SM_LLR - GLA_RANK), (0, 0))).astype(BF16),
        gla_gate_b=_rows(gla_gate_b), ret_gn_w=_rows(ret_gn_w), ret_gn_b=_rows(ret_gn_b),
        gdn_norm_w=_rows(gdn_norm_w), gla_norm_w=_rows(jnp.tile(gla_norm_w, (1, NH))),
        conv_w=gdn_conv_w.astype(F32),
    )
    win = _permute_w_in(w_in)
    wout = w_out.astype(BF16)
    f1 = (_rows(norm_ffn1), ffn1_w1.astype(BF16), ffn1_w3.astype(BF16), ffn1_w2.astype(BF16))
    f2 = (_rows(norm_ffn2), ffn2_w1.astype(BF16), ffn2_w3.astype(BF16), ffn2_w2.astype(BF16))
    nw = _rows(norm_mix)
    fw = norm_final.astype(F32)[None, :]
    sret_in, sgdn_in, sgla_in = state_ret.astype(F32), state_gdn.astype(F32), state_gla.astype(F32)
    cbuf = jnp.pad(state_gdn_conv.astype(F32), ((0, 0), (0, 0), (0, 1), (0, 0))).reshape(DEPTH, n_sample, CONV_DIM)
    acc_ret, acc_gdn, acc_gla = (jnp.zeros(s.shape, F32) for s in (sret_in, sgdn_in, sgla_in))

    ffn = functools.partial(_ffn, n_prompt=n_prompt, n_sample=n_sample)
    xs = (x_prompt.reshape(n_prompt, D_MODEL), x_sample.reshape(n_sample, D_MODEL))
    outs_p = [[] for _ in range(4)]
    conv_s = []
    for l in range(DEPTH):
        x = ffn(xs, l, *f1, fw, first=(l == 0), last=False)
        x, sret, sgdn, sgla, conv = _mixer_prompt(x, l, nw, win, wout, cos_p, sin_p, params, consts_p,
                                                  bsz=bp, tlen=tp)
        outs_p[0].append(jnp.stack([sret[:, h * 64:(h + 1) * 64, h * 64:(h + 1) * 64] for h in range(NH)], axis=1))
        outs_p[1].append(sgdn)
        outs_p[2].append(conv)
        outs_p[3].append(jnp.stack([sgla[:, h * 32:(h + 1) * 32, h * 64:(h + 1) * 64] for h in range(NH)], axis=1))
        x, acc_ret, acc_gdn, acc_gla, cout = _mixer_sample(
            x, l, nw, win, wout, cos_s, sin_s, sret_in, sgdn_in, sgla_in, cbuf[l], acc_ret, acc_gdn, acc_gla,
            params, consts_s, row0=n_prompt)
        conv_s.append(cout.reshape(bs, ts, CONV_DIM)[:, :CONV_W - 1])
        xs = (ffn((x,), l, *f2, fw, first=False, last=(l == DEPTH - 1)),)

    y_prompt, y_sample = xs[0]
    dts = (state_ret.dtype, state_gdn.dtype, state_gdn_conv.dtype, state_gla.dtype)
    sp = [jnp.stack(o).astype(d) for o, d in zip(outs_p, dts)]
    return (y_prompt.reshape(bp, tp, D_MODEL), y_sample.reshape(bs, ts, D_MODEL), sp[0], sp[1], sp[2], sp[3],
            acc_ret.astype(dts[0]), acc_gdn.astype(dts[1]), jnp.stack(conv_s).astype(dts[2]), acc_gla.astype(dts[3]))
```

```python
import functools

import numpy as np
import jax
import jax.numpy as jnp
from jax import lax
from jax.experimental import pallas as pl
from jax.experimental.pallas import tpu as pltpu

F32 = jnp.float32
BF16 = jnp.bfloat16

D_MODEL = 1024
DEPTH = 4
PAST_LEN = 16384
NH = 4
RET_DK = 64
RET_DV = 64
GDN_DK = 128
GDN_DV = 128
GLA_DK = 32
GLA_DV = 64
GLA_RANK = 16
GLA_GATE_NORM = 16.0
CONV_W = 4
CONV_DIM = NH * (2 * GDN_DK + GDN_DV)
D_FF = 2816
ROPE_BASE = 10000.0
EPS = 1e-6
GN_EPS = 1e-5

C = 64
R = NH * C
LANES = 128

Z_RQ, Z_RK, Z_RV, Z_RG = 0, 256, 512, 768
Z_CONV = 1024
Z_DG = 2560
Z_LQ, Z_LK, Z_LV, Z_LG = 3072, 3200, 3328, 3584
Z_SM = 3840
NZ = 3968
SM_DA, SM_DB, SM_LLR = 0, 4, 8

NN = (((1,), (0,)), ((), ()))
NT = (((1,), (1,)), ((), ()))
TN = (((0,), (0,)), ((), ()))

VMEM_LIMIT = 56 * 1024 * 1024
FFN_TM = 512
FFN_TF = 256
MIX_TC = 512


def _mm(a, b, dims=NN):
    return lax.dot_general(a.astype(BF16), b.astype(BF16), dims, preferred_element_type=F32)


def _proj(h, w_rows):
    return lax.dot_general(h, w_rows, NT, preferred_element_type=F32)


def _split2(x):
    hi = x.astype(BF16)
    return hi, (x - hi.astype(F32)).astype(BF16)


def _mm_data_const(x, cb):
    hi, mid = _split2(x)
    m = x.shape[0]
    r = lax.dot_general(jnp.concatenate([hi, mid], axis=0), cb, NN, preferred_element_type=F32)
    return r[:m] + r[m:]


def _mm_const_data(cb, x):
    hi, mid = _split2(x)
    n = x.shape[1]
    r = lax.dot_general(cb, jnp.concatenate([hi, mid], axis=1), NN, preferred_element_type=F32)
    return r[:, :n] + r[:, n:]


def _silu(x):
    return x * jax.nn.sigmoid(x)


def _softplus(x):
    return jnp.maximum(x, 0.0) + jnp.log1p(jnp.exp(-jnp.abs(x)))


def _log_sigmoid(x):
    return jnp.minimum(x, 0.0) - jnp.log1p(jnp.exp(-jnp.abs(x)))


def _rms_rows(x, w):
    ms = jnp.mean(x * x, axis=-1, keepdims=True)
    return x * lax.rsqrt(ms + EPS) * w


def _stack_bd(x, width):
    lane = lax.broadcasted_iota(jnp.int32, x.shape, 1) // width
    zero = jnp.zeros_like(x)
    return jnp.concatenate([jnp.where(lane == h, x, zero) for h in range(NH)], axis=0)


def _col_to_compact(col):
    lane = lax.broadcasted_iota(jnp.int32, (C, R), 1) // C
    out = jnp.zeros((C, R), F32)
    for h in range(NH):
        out = jnp.where(lane == h, col[h * C:(h + 1) * C], out)
    return out


def _ffn_kernel(*refs, first, last, npb, tf):
    n_x = 2 if first else 1
    xrefs, (nw_ref, w1_ref, w3_ref, w2_ref, fw_ref) = refs[:n_x], refs[n_x:n_x + 5]
    n_o = 2 if last else 1
    orefs = refs[n_x + 5:n_x + 5 + n_o]
    h_ref, acc_ref = refs[n_x + 5 + n_o:]
    i = pl.program_id(0)
    x = jnp.where(i < npb, xrefs[0][...], xrefs[1][...]) if first else xrefs[0][...]
    h_ref[...] = _rms_rows(x, nw_ref[...]).astype(BF16)
    for j in range(D_FF // tf):
        h = h_ref[...]
        a = jnp.dot(h, w1_ref[:, j * tf:(j + 1) * tf].astype(BF16), preferred_element_type=F32)
        g = jnp.dot(h, w3_ref[:, j * tf:(j + 1) * tf].astype(BF16), preferred_element_type=F32)
        p = jnp.dot((_silu(a) * g).astype(BF16), w2_ref[j * tf:(j + 1) * tf, :].astype(BF16),
                    preferred_element_type=F32)
        if j == 0:
            acc_ref[...] = p
        else:
            acc_ref[...] += p
    y = x + 0.5 * acc_ref[...]
    if last:
        y = _rms_rows(y, fw_ref[...])

        @pl.when(i < npb)
        def _():
            orefs[0][...] = y

        @pl.when(i >= npb)
        def _():
            orefs[1][...] = y
    else:
        orefs[0][...] = y


def _resident(a):
    nd = a.ndim
    return pl.BlockSpec(a.shape, lambda *_, _n=nd: (0,) * _n, pipeline_mode=pl.Buffered(1))


def _layer_resident(a, l):
    nd = a.ndim - 1
    return pl.BlockSpec((None,) + a.shape[1:], lambda *_, _n=nd: (l,) + (0,) * _n, pipeline_mode=pl.Buffered(1))


def _ffn(xs, l, nw, w1, w3, w2, fw, *, first, last, n_prompt, n_sample, tm=FFN_TM, tf=FFN_TF):
    assert n_sample == tm and n_prompt % tm == 0
    npb = n_prompt // tm
    prompt_blk = pl.BlockSpec((tm, D_MODEL), lambda i: (jnp.minimum(i, npb - 1), 0))
    sample_blk = pl.BlockSpec((tm, D_MODEL), lambda i: (0, 0))
    unified_blk = pl.BlockSpec((tm, D_MODEL), lambda i: (i, 0))
    if last:
        out_specs = (prompt_blk, sample_blk)
        out_shape = (jax.ShapeDtypeStruct((n_prompt, D_MODEL), F32), jax.ShapeDtypeStruct((n_sample, D_MODEL), F32))
    else:
        out_specs = unified_blk
        out_shape = jax.ShapeDtypeStruct((n_prompt + n_sample, D_MODEL), F32)
    return pl.pallas_call(
        functools.partial(_ffn_kernel, first=first, last=last, npb=npb, tf=tf),
        grid=(npb + 1,),
        in_specs=([prompt_blk, sample_blk] if first else [unified_blk]) + [
            _layer_resident(nw, l), _layer_resident(w1, l), _layer_resident(w3, l), _layer_resident(w2, l),
            _resident(fw)],
        out_specs=out_specs,
        out_shape=out_shape,
        scratch_shapes=[pltpu.VMEM((tm, D_MODEL), BF16), pltpu.VMEM((tm, D_MODEL), F32)],
        compiler_params=pltpu.CompilerParams(
            dimension_semantics=("arbitrary",), vmem_limit_bytes=VMEM_LIMIT),
        name="ffn",
    )(*xs, nw, w1, w3, w2, fw)


def _levels(L):
    return [s for s in (1, 2, 4, 8, 16, 32) if s < L]


def _chunk_consts(L):
    i = np.arange(C)
    sid, p = i // L, i % L
    same = sid[:, None] == sid[None, :]
    causal = same & (i[:, None] >= i[None, :])
    strict = same & (i[:, None] > i[None, :])
    lv = _levels(L)
    cum, pm, lm = [causal, same], [np.eye(C, dtype=bool)], []
    for s in lv:
        blk = p // s
        inblk = same & (blk[:, None] == blk[None, :])
        odd = (blk % 2 == 1)[:, None]
        cum.append(inblk & np.where(odd, i[None, :] <= i[:, None], i[None, :] > i[:, None]))
        pm.append(same & odd & (blk[None, :] == blk[:, None] - 1))
        lm.append(strict & ((p[:, None] // (2 * s)) == (p[None, :] // (2 * s))) & (blk[:, None] != blk[None, :]))
    r = np.arange(R)
    bdfull = (r[:, None] // C) == (r[None, :] // C)
    wide = lambda m: np.tile(m, (1, NH))
    f = lambda m: jnp.asarray(np.asarray(m, dtype=np.float32))
    gla_bd = (np.arange(NH * GLA_DK)[:, None] // GLA_DK) == (np.arange(NH * GLA_DV)[None, :] // GLA_DV)
    seqsel = np.zeros((C, LANES), np.float32)
    seqsel[i, sid] = 1.0
    return dict(
        cumg=f(np.concatenate(cum, axis=0)).astype(BF16),
        cumd=f(np.concatenate(cum[:2], axis=0)).astype(BF16),
        pmc=f(np.stack([wide(m) for m in pm])),
        lmc=f(np.stack([wide(m) for m in lm])),
        causal4=f(wide(causal)), eye4=f(wide(np.eye(C, dtype=bool))),
        g64=(f(bdfull) * (1.0 / 64.0)).astype(BF16), bdfull=f(bdfull), glabd=f(gla_bd),
        seqsel=jnp.asarray(seqsel).astype(BF16),
    )


def _ret_consts(L):
    log_gamma = jnp.log(1.0 - 2.0 ** (-5.0 - jnp.arange(NH, dtype=F32)))
    i = np.arange(C)
    sid, p = i // L, (i % L).astype(np.float32)
    causal = (sid[:, None] == sid[None, :]) & (i[:, None] >= i[None, :])
    diff = jnp.asarray(np.where(causal, p[:, None] - p[None, :], 0.0).astype(np.float32))
    dec = jnp.where(causal[None], jnp.exp(diff[None] * log_gamma[:, None, None]), 0.0)
    lg_l = jnp.repeat(log_gamma, RET_DV)[None, :]
    pj = jnp.asarray(p)[:, None]
    return dict(
        decc=jnp.concatenate([dec[h] for h in range(NH)], axis=1),
        dq=jnp.exp((pj + 1.0) * lg_l),
        dk=jnp.exp((L - 1.0 - pj) * lg_l),
        dch=jnp.exp(L * lg_l),
    )


def _rope_tables(pos):
    half = RET_DK // 2
    inv = ROPE_BASE ** (-jnp.arange(half, dtype=F32) / half)
    ang = pos.astype(F32)[:, None] * inv[None, :]
    cos, sin = jnp.cos(ang), jnp.sin(ang)
    return (jnp.tile(jnp.concatenate([cos, cos], axis=1), (1, NH)),
            jnp.tile(jnp.concatenate([-sin, sin], axis=1), (1, NH)))


def _rotary(x, cos, sin):
    lane = lax.broadcasted_iota(jnp.int32, x.shape, 1)
    swapped = jnp.where((lane % RET_DK) < RET_DK // 2,
                        pltpu.roll(x, x.shape[1] - RET_DK // 2, 1), pltpu.roll(x, RET_DK // 2, 1))
    return x * cos + swapped * sin


def _ret_intra(z, cos, sin, k):
    q = _rotary(z[:, Z_RQ:Z_RQ + 256], cos, sin)
    kk = _rotary(z[:, Z_RK:Z_RK + 256], cos, sin) * (RET_DK ** -0.5)
    v = z[:, Z_RV:Z_RV + 256]
    sc = _mm(q, _stack_bd(kk.astype(BF16), RET_DK), NT) * k["decc"][...]
    o = _mm(sc, _stack_bd(v.astype(BF16), RET_DV))
    return o, q * k["dq"][...], kk * k["dk"][...], v


def _ret_out(o, rg, k):
    g = k["g64"][...]
    d = o - _mm_data_const(o, g)
    var = _mm_data_const(d * d, g)
    return _silu(rg) * (d * lax.rsqrt(var + GN_EPS) * k["ret_gn_w"][...] + k["ret_gn_b"][...])


def _gla_intra(z, k, nlev):
    sm = z[:, Z_SM:Z_SM + LANES]
    gk = _log_sigmoid(_mm(sm, k["gwp"][...]) + k["gla_gate_b"][...]) * (1.0 / GLA_GATE_NORM)
    q = z[:, Z_LQ:Z_LQ + 128] * (GLA_DK ** -0.5)
    kk = z[:, Z_LK:Z_LK + 128]
    v = z[:, Z_LV:Z_LV + 256]
    cs = _mm_const_data(k["cumg"][...], gk)
    b, blast = cs[:C], cs[C:2 * C]
    sc = _mm(q, _stack_bd(kk.astype(BF16), GLA_DK), NT) * k["pmc"][0]
    for li in range(nlev):
        e = jnp.exp(cs[(2 + li) * C:(3 + li) * C])
        sc = sc + _mm(q * e, _stack_bd((kk * e).astype(BF16), GLA_DK), NT) * k["pmc"][li + 1]
    o = _mm(sc, _stack_bd(v.astype(BF16), GLA_DV))
    return o, q * jnp.exp(b), kk * jnp.exp(blast - b), v, b, blast


def _gla_out(o, lg, k):
    ms = _mm_data_const(o * o, k["g64"][...])
    return o * lax.rsqrt(ms + EPS) * k["gla_norm_w"][...] * _silu(lg)


def _lane_col(blk, lane0):
    lane = lax.broadcasted_iota(jnp.int32, blk.shape, 1)
    st = jnp.concatenate([jnp.where(lane == lane0 + h, blk, 0.0) for h in range(NH)], axis=0)
    return jnp.sum(st, axis=-1, keepdims=True)


def _gdn_front(conv, sm, k):
    def l2n(x):
        return x * lax.rsqrt(jnp.sum(x * x, axis=-1, keepdims=True) + EPS)

    qs = [l2n(conv[h]) * (GDN_DK ** -0.5) for h in range(NH)]
    ks = [l2n(conv[NH + h]) for h in range(NH)]
    vst = jnp.concatenate(conv[2 * NH:], axis=0)
    qst = jnp.concatenate(qs, axis=0)
    kst = jnp.concatenate(ks, axis=0)
    kq = []
    for p in range(NH // 2):
        kp = jnp.concatenate(ks[2 * p:2 * p + 2], axis=1).astype(BF16)
        lane = lax.broadcasted_iota(jnp.int32, kp.shape, 1) // GDN_DK
        kbd = jnp.concatenate([jnp.where(lane == hl, kp, jnp.zeros_like(kp)) for hl in range(2)], axis=0)
        lhs = jnp.concatenate([kp, jnp.concatenate(qs[2 * p:2 * p + 2], axis=1).astype(BF16)], axis=0)
        kq.append(lax.dot_general(lhs, kbd, NT, preferred_element_type=F32))
    kq = jnp.concatenate(kq, axis=1)
    gd = -jnp.exp(k["alog"][...]) * _softplus(sm + k["dtb"][...])
    beta = jax.nn.sigmoid(sm)
    cs = _mm_const_data(k["cumd"][...], gd)
    bcol = _lane_col(cs[:C], SM_DA)
    blcol = _lane_col(cs[C:], SM_DA)
    betacol = _lane_col(beta, SM_DB)
    bcolc = _col_to_compact(bcol)
    browc = jnp.sum(bcolc * k["eye4"][...], axis=0, keepdims=True)
    causal = k["causal4"][...] > 0.5
    decc = jnp.where(causal, jnp.exp(jnp.where(causal, bcolc - browc, 0.0)), 0.0)
    ac = kq[:C] * decc * _col_to_compact(betacol)
    qkc = kq[C:] * decc
    return dict(ac=ac, qkc=qkc, qst=qst, kst=kst, vst=vst, bcol=bcol, blcol=blcol, betacol=betacol)


def _gdn_inverse(acs, k, nlev):
    xs = [k["eye4"][...] - a * k["lmc"][0] for a in acs]
    for li in range(1, nlev):
        ms = [a * k["lmc"][li] for a in acs]
        ys = [_mm(m, _stack_bd(x.astype(BF16), C)) for m, x in zip(ms, xs)]
        xs = [x - _mm(x, _stack_bd(y.astype(BF16), C)) for x, y in zip(xs, ys)]
    return xs


def _gdn_back(f, xc):
    kb = f["kst"] * f["betacol"]
    rhs = jnp.concatenate([f["vst"] * f["betacol"], kb * jnp.exp(f["bcol"])], axis=1)
    sol = _mm(_stack_bd(xc.astype(BF16), C), rhs)
    return dict(qst=f["qst"], solv=sol[:, :GDN_DV], solk=sol[:, GDN_DV:],
                qk=_stack_bd(f["qkc"].astype(BF16), C), ebcol=jnp.exp(f["bcol"]),
                kdec=f["kst"] * jnp.exp(f["blcol"] - f["bcol"]), eblast=jnp.exp(f["blcol"]))


def _gdn_out(o, dg, k):
    w = k["gdn_norm_w"][...]
    y = jnp.concatenate([_rms_rows(o[:, h * GDN_DV:(h + 1) * GDN_DV], w) for h in range(NH)], axis=1)
    return y * _silu(dg)


_PARAM_NAMES = ("alog", "dtb", "gwp", "gla_gate_b", "ret_gn_w", "ret_gn_b", "gdn_norm_w", "gla_norm_w", "conv_w")
_CONST_NAMES = ("cumg", "cumd", "pmc", "lmc", "causal4", "eye4", "g64", "bdfull", "glabd", "seqsel",
                "decc", "dq", "dk", "dch")


def _mixer_prompt_kernel(*refs, tc, nlev):
    n_in = 6 + len(_PARAM_NAMES) + len(_CONST_NAMES)
    x_ref, nw_ref, win_ref, wout_ref, cos_ref, sin_ref = refs[:6]
    k = dict(zip(_PARAM_NAMES + _CONST_NAMES, refs[6:n_in]))
    o_ref, sret_ref, sgdn_ref, sgla_ref, conv_ref = refs[n_in:n_in + 5]
    z_ref, xp_ref, cv_ref, y_ref = refs[n_in + 5:]
    t = pl.program_id(1)
    nch = tc // C
    ng = CONV_DIM // LANES

    @pl.when(t == 0)
    def _():
        sret_ref[...] = jnp.zeros_like(sret_ref)
        sgdn_ref[...] = jnp.zeros_like(sgdn_ref)
        sgla_ref[...] = jnp.zeros_like(sgla_ref)
        xp_ref[:, pl.ds(0, 8), :] = jnp.zeros((ng, 8, LANES), F32)

    h = _rms_rows(x_ref[...], nw_ref[...]).astype(BF16)
    zc = _proj(h, win_ref[Z_CONV:Z_CONV + CONV_DIM, :])
    for g in range(ng):
        xp_ref[g, pl.ds(8, tc), :] = zc[:, g * LANES:(g + 1) * LANES]
    z_ref[:, :Z_CONV] = _proj(h, win_ref[:Z_CONV, :])
    z_ref[:, Z_DG:] = _proj(h, win_ref[Z_DG:, :])

    cw = k["conv_w"]
    nb = tc // 8
    for g in range(ng):
        wg = [cw[i:i + 1, g * LANES:(g + 1) * LANES] for i in range(CONV_W)]
        taps = {s: xp_ref[g, pl.ds(s, nb, stride=8), :] for s in range(5, 5 + 8 + CONV_W - 1)}
        for j in range(8):
            acc = taps[j + 8] * wg[3]
            for i in range(CONV_W - 1):
                acc = acc + taps[j + 5 + i] * wg[i]
            cv_ref[g, pl.ds(j, nb, stride=8), :] = _silu(acc)
        xp_ref[g, pl.ds(0, 8), :] = xp_ref[g, pl.ds(tc, 8), :]

    zs = [z_ref[pl.ds(c * C, C), :] for c in range(nch)]
    fronts = [_gdn_front([cv_ref[g, pl.ds(c * C, C), :] for g in range(ng)], z[:, Z_SM:Z_SM + LANES], k)
              for c, z in enumerate(zs)]
    xcs = _gdn_inverse([f["ac"] for f in fronts], k, nlev)
    gs = [_gdn_back(f, xc) for f, xc in zip(fronts, xcs)]
    rets = [_ret_intra(z, cos_ref[pl.ds(c * C, C), :], sin_ref[pl.ds(c * C, C), :], k) for c, z in enumerate(zs)]
    glas = [_gla_intra(z, k, nlev) for z in zs]

    for c in range(nch):
        rows = pl.ds(c * C, C)

        o, qd, kd, v = rets[c]
        s = sret_ref[0]
        z_ref[rows, Z_RQ:Z_RQ + 256] = o + _mm(qd, s)
        sret_ref[0] = s * k["dch"][...] + _mm(kd, v, TN) * k["bdfull"][...]

        g = gs[c]
        us, qss = [], []
        for hh in range(NH):
            lhs = jnp.concatenate([g["solk"][hh * C:(hh + 1) * C], g["qst"][hh * C:(hh + 1) * C]], axis=0)
            r = _mm(lhs, sgdn_ref[0, hh])
            us.append(g["solv"][hh * C:(hh + 1) * C] - r[:C])
            qss.append(r[C:])
        ost = g["ebcol"] * jnp.concatenate(qss, axis=0) + _mm(g["qk"], jnp.concatenate(us, axis=0))
        for hh in range(NH):
            sl = slice(hh * C, (hh + 1) * C)
            sgdn_ref[0, hh] = (g["eblast"][hh * C:hh * C + 1] * sgdn_ref[0, hh]
                               + _mm(g["kdec"][sl], us[hh], TN))
        z_ref[rows, Z_CONV:Z_CONV + NH * GDN_DV] = jnp.concatenate(
            [ost[hh * C:(hh + 1) * C] for hh in range(NH)], axis=1)

        o, qe, ke, v, b, _ = glas[c]
        s = sgla_ref[0]
        z_ref[rows, Z_LQ:Z_LQ + 256] = o + _mm(qe, s)
        escale = jnp.exp(b[C - 8:, :].T[:, 7:8])
        sgla_ref[0] = s * escale + _mm(ke, v, TN) * k["glabd"][...]

    y_r = _ret_out(z_ref[:, Z_RQ:Z_RQ + 256], z_ref[:, Z_RG:Z_RG + 256], k)
    y_d = _gdn_out(z_ref[:, Z_CONV:Z_CONV + NH * GDN_DV], z_ref[:, Z_DG:Z_DG + 512], k)
    y_l = _gla_out(z_ref[:, Z_LQ:Z_LQ + 256], z_ref[:, Z_LG:Z_LG + 256], k)
    y_ref[...] = jnp.concatenate([y_r, y_d, y_l], axis=1).astype(BF16)

    o_ref[...] = x_ref[...] + jnp.dot(y_ref[...], wout_ref[...].astype(BF16), preferred_element_type=F32)

    @pl.when(t == pl.num_programs(1) - 1)
    def _():
        for g in range(ng):
            conv_ref[0, :, g * LANES:(g + 1) * LANES] = xp_ref[g, pl.ds(5, 3), :]


def _mixer_prompt(x, l, nw, win, wout, cos, sin, params, consts, *, bsz, tlen, tc=MIX_TC):
    nlev = len(_levels(C))
    nt = tlen // tc
    in_specs = [
        pl.BlockSpec((tc, D_MODEL), lambda b, t: (b * nt + t, 0)),
        _layer_resident(nw, l), _layer_resident(win, l), _layer_resident(wout, l),
        pl.BlockSpec((tc, 256), lambda b, t: (t, 0)),
        pl.BlockSpec((tc, 256), lambda b, t: (t, 0)),
    ] + [_layer_resident(params[n], l) for n in _PARAM_NAMES] + [_resident(consts[n]) for n in _CONST_NAMES]
    extras = [params[n] for n in _PARAM_NAMES] + [consts[n] for n in _CONST_NAMES]
    out_shape = (
        jax.ShapeDtypeStruct(x.shape, F32),
        jax.ShapeDtypeStruct((bsz, 256, 256), F32),
        jax.ShapeDtypeStruct((bsz, NH, GDN_DK, GDN_DV), F32),
        jax.ShapeDtypeStruct((bsz, NH * GLA_DK, NH * GLA_DV), F32),
        jax.ShapeDtypeStruct((bsz, CONV_W - 1, CONV_DIM), F32),
    )
    out_specs = (
        pl.BlockSpec((tc, D_MODEL), lambda b, t: (b * nt + t, 0)),
        pl.BlockSpec((1, 256, 256), lambda b, t: (b, 0, 0)),
        pl.BlockSpec((1, NH, GDN_DK, GDN_DV), lambda b, t: (b, 0, 0, 0)),
        pl.BlockSpec((1, NH * GLA_DK, NH * GLA_DV), lambda b, t: (b, 0, 0)),
        pl.BlockSpec((1, CONV_W - 1, CONV_DIM), lambda b, t: (b, 0, 0)),
    )
    return pl.pallas_call(
        functools.partial(_mixer_prompt_kernel, tc=tc, nlev=nlev),
        grid=(bsz, nt),
        in_specs=in_specs,
        out_specs=out_specs,
        out_shape=out_shape,
        input_output_aliases={0: 0},
        scratch_shapes=[
            pltpu.VMEM((tc, NZ), F32),
            pltpu.VMEM((CONV_DIM // LANES, tc + 8, LANES), F32),
            pltpu.VMEM((CONV_DIM // LANES, tc, LANES), F32),
            pltpu.VMEM((tc, D_MODEL), BF16),
        ],
        compiler_params=pltpu.CompilerParams(
            dimension_semantics=("parallel", "arbitrary"), vmem_limit_bytes=VMEM_LIMIT),
        name="mixer_prompt",
    )(x, nw, win, wout, cos, sin, *extras)


SEQ_S = 4
NSEQ = C // SEQ_S


def _inproj_kernel(x_ref, nw_ref, win_ref, z_ref):
    z_ref[...] = _proj(_rms_rows(x_ref[...], nw_ref[...]).astype(BF16), win_ref[...])


def _sample_inproj(x, l, nw, win, *, row0, n):
    return pl.pallas_call(
        _inproj_kernel,
        grid=(1,),
        in_specs=[pl.BlockSpec((n, D_MODEL), lambda i: (row0 // n, 0)), _layer_resident(nw, l),
                  _layer_resident(win, l)],
        out_specs=pl.BlockSpec((n, NZ), lambda i: (0, 0)),
        out_shape=jax.ShapeDtypeStruct((n, NZ), F32),
        compiler_params=pltpu.CompilerParams(
            dimension_semantics=("arbitrary",), vmem_limit_bytes=VMEM_LIMIT),
        name="sample_inproj",
    )(x, nw, win)


def _mixer_sample_kernel(*refs, nlev):
    n_in = 9 + len(_PARAM_NAMES) + len(_CONST_NAMES)
    (x_ref, zin_ref, wout_ref, cos_ref, sin_ref,
     sret_in, sgdn_in, sgla_in, cbuf_ref) = refs[:9]
    k = dict(zip(_PARAM_NAMES + _CONST_NAMES, refs[9:n_in]))
    o_ref, sret_out, sgdn_out, sgla_out, cout_ref = refs[n_in + 3:n_in + 8]

    x = x_ref[...]
    z = zin_ref[...]

    rowi = lax.broadcasted_iota(jnp.int32, (C, 1), 0)
    tpos = rowi % SEQ_S

    xc = z[:, Z_CONV:Z_CONV + CONV_DIM]
    cb = cbuf_ref[...]
    cw = k["conv_w"]
    acc = xc * cw[3:4, :]
    for i in range(CONV_W - 1):
        cur = pltpu.roll(xc, 3 - i, 0)
        old = cb if i == 0 else pltpu.roll(cb, C - i, 0)
        acc = acc + jnp.where(tpos + i >= 3, cur, old) * cw[i:i + 1, :]
    cout_ref[...] = pltpu.roll(xc, C - 1, 0)
    conv = _silu(acc)

    seq_of_row = rowi // SEQ_S
    seq_of_lane = lax.broadcasted_iota(jnp.int32, (1, C), 1) // SEQ_S

    o, qd, kd, v = _ret_intra(z, cos_ref[...], sin_ref[...], k)
    dch = k["dch"][...]
    kdt = kd.T
    accs = [jnp.zeros((C, RET_DV), F32) for _ in range(NH)]
    for n in range(NSEQ):
        rm = seq_of_row == n
        lm = seq_of_lane == n
        for hh in range(NH):
            sl = slice(hh * RET_DK, (hh + 1) * RET_DK)
            s = sret_in[n, hh]
            accs[hh] = jnp.where(rm, _mm(qd[:, sl], s), accs[hh])
            sret_out[n, hh] = s * dch[:, sl] + _mm(jnp.where(lm, kdt[sl], 0.0), v[:, sl])
    o = o + jnp.concatenate(accs, axis=1)
    y_r = _ret_out(o, z[:, Z_RG:Z_RG + 256], k)

    f = _gdn_front([conv[:, gi * LANES:(gi + 1) * LANES] for gi in range(CONV_DIM // LANES)],
                   z[:, Z_SM:Z_SM + LANES], k)
    g = _gdn_back(f, _gdn_inverse([f["ac"]], k, nlev)[0])
    rowi2 = lax.broadcasted_iota(jnp.int32, (2 * C, 1), 0)
    seq_of_row2 = (rowi2 % C) // SEQ_S
    lhs = [jnp.concatenate([g["solk"][hh * C:(hh + 1) * C], g["qst"][hh * C:(hh + 1) * C]], axis=0)
           for hh in range(NH)]
    accs = [jnp.zeros((2 * C, GDN_DV), F32) for _ in range(NH)]
    for n in range(NSEQ):
        rm2 = seq_of_row2 == n
        for hh in range(NH):
            accs[hh] = jnp.where(rm2, _mm(lhs[hh], sgdn_in[n, hh]), accs[hh])
    us = [g["solv"][hh * C:(hh + 1) * C] - accs[hh][:C] for hh in range(NH)]
    ost = (g["ebcol"] * jnp.concatenate([accs[hh][C:] for hh in range(NH)], axis=0)
           + _mm(g["qk"], jnp.concatenate(us, axis=0)))
    kdts = [g["kdec"][hh * C:(hh + 1) * C].T for hh in range(NH)]
    for n in range(NSEQ):
        lm = seq_of_lane == n
        for hh in range(NH):
            scale = g["eblast"][hh * C + n * SEQ_S:hh * C + n * SEQ_S + 1]
            sgdn_out[n, hh] = scale * sgdn_in[n, hh] + _mm(jnp.where(lm, kdts[hh], 0.0), us[hh])
    y_d = _gdn_out(jnp.concatenate([ost[hh * C:(hh + 1) * C] for hh in range(NH)], axis=1),
                   z[:, Z_DG:Z_DG + 512], k)

    o, qe, ke, v, b, blast = _gla_intra(z, k, nlev)
    ket = ke.T
    hi, mid = _split2(blast * (1.0 / SEQ_S))
    blt2 = lax.dot_general(jnp.concatenate([hi, mid], axis=1), k["seqsel"][...], TN, preferred_element_type=F32)
    eblt = jnp.exp(blt2[:NH * GLA_DK] + blt2[NH * GLA_DK:])
    accs = [jnp.zeros((C, GLA_DV), F32) for _ in range(NH)]
    for n in range(NSEQ):
        rm = seq_of_row == n
        lm = seq_of_lane == n
        for hh in range(NH):
            sk = slice(hh * GLA_DK, (hh + 1) * GLA_DK)
            sv = slice(hh * GLA_DV, (hh + 1) * GLA_DV)
            s = sgla_in[n, hh]
            accs[hh] = jnp.where(rm, _mm(qe[:, sk], s), accs[hh])
            sgla_out[n, hh] = s * eblt[sk, n:n + 1] + _mm(jnp.where(lm, ket[sk], 0.0), v[:, sv])
    o = o + jnp.concatenate(accs, axis=1)
    y_l = _gla_out(o, z[:, Z_LG:Z_LG + 256], k)

    y = jnp.concatenate([y_r, y_d, y_l], axis=1).astype(BF16)
    o_ref[...] = x + jnp.dot(y, wout_ref[...].astype(BF16), preferred_element_type=F32)


def _mixer_sample(x, l, nw, win, wout, cos, sin, sret, sgdn, sgla, cbuf, acc_ret, acc_gdn, acc_gla,
                  params, consts, *, row0):
    nb = sret.shape[1]
    n = nb * SEQ_S
    nlev = len(_levels(SEQ_S))
    blk0 = row0 // C

    def st_spec(dk, dv):
        return pl.BlockSpec((None, NSEQ, NH, dk, dv), lambda i: (l, i, 0, 0, 0))

    st_specs = [st_spec(RET_DK, RET_DV), st_spec(GDN_DK, GDN_DV), st_spec(GLA_DK, GLA_DV)]
    any_spec = pl.BlockSpec(memory_space=pl.ANY)
    z = _sample_inproj(x, l, nw, win, row0=row0, n=n)
    in_specs = [
        pl.BlockSpec((C, D_MODEL), lambda i: (blk0 + i, 0)),
        pl.BlockSpec((C, NZ), lambda i: (i, 0)),
        _layer_resident(wout, l), _resident(cos), _resident(sin),
    ] + st_specs + [pl.BlockSpec((C, CONV_DIM), lambda i: (i, 0))] + [
        _layer_resident(params[nm], l) for nm in _PARAM_NAMES] + [
        _resident(consts[nm]) for nm in _CONST_NAMES] + [any_spec] * 3
    extras = [params[nm] for nm in _PARAM_NAMES] + [consts[nm] for nm in _CONST_NAMES]
    n_in = len(in_specs)
    out_shape = (
        jax.ShapeDtypeStruct(x.shape, F32),
        jax.ShapeDtypeStruct(acc_ret.shape, F32),
        jax.ShapeDtypeStruct(acc_gdn.shape, F32),
        jax.ShapeDtypeStruct(acc_gla.shape, F32),
        jax.ShapeDtypeStruct((n, CONV_DIM), F32),
    )
    out_specs = (pl.BlockSpec((C, D_MODEL), lambda i: (blk0 + i, 0)),) + tuple(st_specs) + (
        pl.BlockSpec((C, CONV_DIM), lambda i: (i, 0)),)
    return pl.pallas_call(
        functools.partial(_mixer_sample_kernel, nlev=nlev),
        grid=(n // C,),
        in_specs=in_specs,
        out_specs=out_specs,
        out_shape=out_shape,
        input_output_aliases={0: 0, n_in - 3: 1, n_in - 2: 2, n_in - 1: 3},
        compiler_params=pltpu.CompilerParams(
            dimension_semantics=("parallel",), vmem_limit_bytes=VMEM_LIMIT),
        name="mixer_sample",
    )(x, z, wout, cos, sin, sret, sgdn, sgla, cbuf, *extras, acc_ret, acc_gdn, acc_gla)


def _permute_w_in(w):
    wt = jnp.swapaxes(w, 1, 2)
    pad = jnp.zeros((w.shape[0], NZ - Z_SM - 24, w.shape[1]), w.dtype)
    return jnp.concatenate(
        [wt[:, 0:2560], wt[:, 2568:3080], wt[:, 3080:3592], wt[:, 3608:3864], wt[:, 2560:2568],
         wt[:, 3592:3608], pad], axis=1).astype(BF16)


def _rows(v, width=None):
    v = v.astype(F32)[:, None, :]
    if width is not None and v.shape[-1] < width:
        v = jnp.pad(v, ((0, 0), (0, 0), (0, width - v.shape[-1])))
    return v


def kernel(x_prompt, x_sample, state_ret, state_gdn, state_gdn_conv, state_gla, norm_ffn1, ffn1_w1, ffn1_w3, ffn1_w2, norm_mix, w_in, ret_gn_w, ret_gn_b, gdn_conv_w, gdn_A_log, gdn_dt_bias, gdn_norm_w, gla_gate_w, gla_gate_b, gla_norm_w, w_out, norm_ffn2, ffn2_w1, ffn2_w3, ffn2_w2, norm_final):
    bp, tp, _ = x_prompt.shape
    bs, ts, _ = x_sample.shape
    n_prompt, n_sample = bp * tp, bs * ts
    assert ts == SEQ_S and tp % MIX_TC == 0 and n_sample % C == 0 and n_prompt % C == 0

    consts_p = dict(_chunk_consts(C), **_ret_consts(C))
    consts_s = dict(_chunk_consts(SEQ_S), **_ret_consts(SEQ_S))
    cos_p, sin_p = _rope_tables(jnp.arange(tp, dtype=jnp.int32))
    cos_s, sin_s = _rope_tables(PAST_LEN + (jnp.arange(C, dtype=jnp.int32) % SEQ_S))

    params = dict(
        alog=_rows(gdn_A_log, LANES), dtb=_rows(gdn_dt_bias, LANES),
        gwp=jnp.pad(gla_gate_w, ((0, 0), (SM_LLR, LANES - SM_LLR - GLA_RANK), (0, 0))).astype(BF16),
        gla_gate_b=_rows(gla_gate_b), ret_gn_w=_rows(ret_gn_w), ret_gn_b=_rows(ret_gn_b),
        gdn_norm_w=_rows(gdn_norm_w), gla_norm_w=_rows(jnp.tile(gla_norm_w, (1, NH))),
        conv_w=gdn_conv_w.astype(F32),
    )
    win = _permute_w_in(w_in)
    wout = w_out.astype(F32)
    f1 = (_rows(norm_ffn1), ffn1_w1.astype(F32), ffn1_w3.astype(F32), ffn1_w2.astype(F32))
    f2 = (_rows(norm_ffn2), ffn2_w1.astype(F32), ffn2_w3.astype(F32), ffn2_w2.astype(F32))
    nw = _rows(norm_mix)
    fw = norm_final.astype(F32)[None, :]
    sret_in, sgdn_in, sgla_in = state_ret.astype(F32), state_gdn.astype(F32), state_gla.astype(F32)
    cbuf = jnp.pad(state_gdn_conv.astype(F32), ((0, 0), (0, 0), (0, 1), (0, 0))).reshape(DEPTH, n_sample, CONV_DIM)
    acc_ret, acc_gdn, acc_gla = (jnp.zeros(s.shape, F32) for s in (sret_in, sgdn_in, sgla_in))

    ffn = functools.partial(_ffn, n_prompt=n_prompt, n_sample=n_sample)
    xs = (x_prompt.reshape(n_prompt, D_MODEL), x_sample.reshape(n_sample, D_MODEL))
    outs_p = [[] for _ in range(4)]
    conv_s = []
    for l in range(DEPTH):
        x = ffn(xs, l, *f1, fw, first=(l == 0), last=False)
        x, sret, sgdn, sgla, conv = _mixer_prompt(x, l, nw, win, wout, cos_p, sin_p, params, consts_p,
                                                  bsz=bp, tlen=tp)
        outs_p[0].append(jnp.stack([sret[:, h * 64:(h + 1) * 64, h * 64:(h + 1) * 64] for h in range(NH)], axis=1))
        outs_p[1].append(sgdn)
        outs_p[2].append(conv)
        outs_p[3].append(jnp.stack([sgla[:, h * 32:(h + 1) * 32, h * 64:(h + 1) * 64] for h in range(NH)], axis=1))
        x, acc_ret, acc_gdn, acc_gla, cout = _mixer_sample(
            x, l, nw, win, wout, cos_s, sin_s, sret_in, sgdn_in, sgla_in, cbuf[l], acc_ret, acc_gdn, acc_gla,
            params, consts_s, row0=n_prompt)
        conv_s.append(cout.reshape(bs, ts, CONV_DIM)[:, :CONV_W - 1])
        xs = (ffn((x,), l, *f2, fw, first=False, last=(l == DEPTH - 1)),)

    y_prompt, y_sample = xs[0]
    dts = (state_ret.dtype, state_gdn.dtype, state_gdn_conv.dtype, state_gla.dtype)
    sp = [jnp.stack(o).astype(d) for o, d in zip(outs_p, dts)]
    return (y_prompt.reshape(bp, tp, D_MODEL), y_sample.reshape(bs, ts, D_MODEL), sp[0], sp[1], sp[2], sp[3],
            acc_ret.astype(dts[0]), acc_gdn.astype(dts[1]), jnp.stack(conv_s).astype(dts[2]), acc_gla.astype(dts[3]))
```

```python
import functools

import numpy as np
import jax
import jax.numpy as jnp
from jax import lax
from jax.experimental import pallas as pl
from jax.experimental.pallas import tpu as pltpu

F32 = jnp.float32
BF16 = jnp.bfloat16

D_MODEL = 1024
DEPTH = 4
PAST_LEN = 16384
NH = 4
RET_DK = 64
RET_DV = 64
GDN_DK = 128
GDN_DV = 128
GLA_DK = 32
GLA_DV = 64
GLA_RANK = 16
GLA_GATE_NORM = 16.0
CONV_W = 4
CONV_DIM = NH * (2 * GDN_DK + GDN_DV)
D_FF = 2816
ROPE_BASE = 10000.0
EPS = 1e-6
GN_EPS = 1e-5

C = 64
R = NH * C
LANES = 128

Z_RQ, Z_RK, Z_RV, Z_RG = 0, 256, 512, 768
Z_CONV = 1024
Z_DG = 2560
Z_LQ, Z_LK, Z_LV, Z_LG = 3072, 3200, 3328, 3584
Z_SM = 3840
NZ = 3968
SM_DA, SM_DB, SM_LLR = 0, 4, 8

NN = (((1,), (0,)), ((), ()))
NT = (((1,), (1,)), ((), ()))
TN = (((0,), (0,)), ((), ()))

VMEM_LIMIT = 56 * 1024 * 1024
FFN_TM = 512
FFN_TF = 256
MIX_TC = 512


def _mm(a, b, dims=NN):
    return lax.dot_general(a.astype(BF16), b.astype(BF16), dims, preferred_element_type=F32)


def _proj(h, w_rows):
    return lax.dot_general(h, w_rows, NT, preferred_element_type=F32)


def _split2(x):
    hi = x.astype(BF16)
    return hi, (x - hi.astype(F32)).astype(BF16)


def _mm_data_const(x, cb):
    hi, mid = _split2(x)
    m = x.shape[0]
    r = lax.dot_general(jnp.concatenate([hi, mid], axis=0), cb, NN, preferred_element_type=F32)
    return r[:m] + r[m:]


def _mm_const_data(cb, x):
    hi, mid = _split2(x)
    n = x.shape[1]
    r = lax.dot_general(cb, jnp.concatenate([hi, mid], axis=1), NN, preferred_element_type=F32)
    return r[:, :n] + r[:, n:]


def _silu(x):
    return x * jax.nn.sigmoid(x)


def _softplus(x):
    return jnp.maximum(x, 0.0) + jnp.log1p(jnp.exp(-jnp.abs(x)))


def _log_sigmoid(x):
    return jnp.minimum(x, 0.0) - jnp.log1p(jnp.exp(-jnp.abs(x)))


def _rms_rows(x, w):
    ms = jnp.mean(x * x, axis=-1, keepdims=True)
    return x * lax.rsqrt(ms + EPS) * w


def _stack_bd(x, width):
    lane = lax.broadcasted_iota(jnp.int32, x.shape, 1) // width
    zero = jnp.zeros_like(x)
    return jnp.concatenate([jnp.where(lane == h, x, zero) for h in range(NH)], axis=0)


def _col_to_compact(col):
    lane = lax.broadcasted_iota(jnp.int32, (C, R), 1) // C
    out = jnp.zeros((C, R), F32)
    for h in range(NH):
        out = jnp.where(lane == h, col[h * C:(h + 1) * C], out)
    return out


def _ffn_kernel(*refs, first, last, npb, tf):
    n_x = 2 if first else 1
    xrefs, (nw_ref, w1_ref, w3_ref, w2_ref, fw_ref) = refs[:n_x], refs[n_x:n_x + 5]
    n_o = 2 if last else 1
    orefs = refs[n_x + 5:n_x + 5 + n_o]
    h_ref, acc_ref = refs[n_x + 5 + n_o:]
    i = pl.program_id(0)
    x = jnp.where(i < npb, xrefs[0][...], xrefs[1][...]) if first else xrefs[0][...]
    h_ref[...] = _rms_rows(x, nw_ref[...]).astype(BF16)
    for j in range(D_FF // tf):
        h = h_ref[...]
        a = jnp.dot(h, w1_ref[:, j * tf:(j + 1) * tf].astype(BF16), preferred_element_type=F32)
        g = jnp.dot(h, w3_ref[:, j * tf:(j + 1) * tf].astype(BF16), preferred_element_type=F32)
        p = jnp.dot((_silu(a) * g).astype(BF16), w2_ref[j * tf:(j + 1) * tf, :].astype(BF16),
                    preferred_element_type=F32)
        if j == 0:
            acc_ref[...] = p
        else:
            acc_ref[...] += p
    y = x + 0.5 * acc_ref[...]
    if last:
        y = _rms_rows(y, fw_ref[...])

        @pl.when(i < npb)
        def _():
            orefs[0][...] = y

        @pl.when(i >= npb)
        def _():
            orefs[1][...] = y
    else:
        orefs[0][...] = y


def _resident(a):
    nd = a.ndim
    return pl.BlockSpec(a.shape, lambda *_, _n=nd: (0,) * _n, pipeline_mode=pl.Buffered(1))


def _layer_resident(a, l):
    nd = a.ndim - 1
    return pl.BlockSpec((None,) + a.shape[1:], lambda *_, _n=nd: (l,) + (0,) * _n, pipeline_mode=pl.Buffered(1))


def _ffn(xs, l, nw, w1, w3, w2, fw, *, first, last, n_prompt, n_sample, tm=FFN_TM, tf=FFN_TF):
    assert n_sample == tm and n_prompt % tm == 0
    npb = n_prompt // tm
    prompt_blk = pl.BlockSpec((tm, D_MODEL), lambda i: (jnp.minimum(i, npb - 1), 0))
    sample_blk = pl.BlockSpec((tm, D_MODEL), lambda i: (0, 0))
    unified_blk = pl.BlockSpec((tm, D_MODEL), lambda i: (i, 0))
    if last:
        out_specs = (prompt_blk, sample_blk)
        out_shape = (jax.ShapeDtypeStruct((n_prompt, D_MODEL), F32), jax.ShapeDtypeStruct((n_sample, D_MODEL), F32))
    else:
        out_specs = unified_blk
        out_shape = jax.ShapeDtypeStruct((n_prompt + n_sample, D_MODEL), F32)
    return pl.pallas_call(
        functools.partial(_ffn_kernel, first=first, last=last, npb=npb, tf=tf),
        grid=(npb + 1,),
        in_specs=([prompt_blk, sample_blk] if first else [unified_blk]) + [
            _layer_resident(nw, l), _layer_resident(w1, l), _layer_resident(w3, l), _layer_resident(w2, l),
            _resident(fw)],
        out_specs=out_specs,
        out_shape=out_shape,
        scratch_shapes=[pltpu.VMEM((tm, D_MODEL), BF16), pltpu.VMEM((tm, D_MODEL), F32)],
        compiler_params=pltpu.CompilerParams(
            dimension_semantics=("arbitrary",), vmem_limit_bytes=VMEM_LIMIT),
        name="ffn",
    )(*xs, nw, w1, w3, w2, fw)


def _levels(L):
    return [s for s in (1, 2, 4, 8, 16, 32) if s < L]


def _chunk_consts(L):
    i = np.arange(C)
    sid, p = i // L, i % L
    same = sid[:, None] == sid[None, :]
    causal = same & (i[:, None] >= i[None, :])
    strict = same & (i[:, None] > i[None, :])
    lv = _levels(L)
    cum, pm, lm = [causal, same], [np.eye(C, dtype=bool)], []
    for s in lv:
        blk = p // s
        inblk = same & (blk[:, None] == blk[None, :])
        odd = (blk % 2 == 1)[:, None]
        cum.append(inblk & np.where(odd, i[None, :] <= i[:, None], i[None, :] > i[:, None]))
        pm.append(same & odd & (blk[None, :] == blk[:, None] - 1))
        lm.append(strict & ((p[:, None] // (2 * s)) == (p[None, :] // (2 * s))) & (blk[:, None] != blk[None, :]))
    r = np.arange(R)
    bdfull = (r[:, None] // C) == (r[None, :] // C)
    wide = lambda m: np.tile(m, (1, NH))
    f = lambda m: jnp.asarray(np.asarray(m, dtype=np.float32))
    gla_bd = (np.arange(NH * GLA_DK)[:, None] // GLA_DK) == (np.arange(NH * GLA_DV)[None, :] // GLA_DV)
    seqsel = np.zeros((C, LANES), np.float32)
    seqsel[i, sid] = 1.0
    return dict(
        cumg=f(np.concatenate(cum, axis=0)).astype(BF16),
        cumd=f(np.concatenate(cum[:2], axis=0)).astype(BF16),
        pmc=f(np.stack([wide(m) for m in pm])),
        lmc=f(np.stack([wide(m) for m in lm])),
        causal4=f(wide(causal)), eye4=f(wide(np.eye(C, dtype=bool))),
        g64=(f(bdfull) * (1.0 / 64.0)).astype(BF16), bdfull=f(bdfull), glabd=f(gla_bd),
        seqsel=jnp.asarray(seqsel).astype(BF16),
    )


def _ret_consts(L):
    log_gamma = jnp.log(1.0 - 2.0 ** (-5.0 - jnp.arange(NH, dtype=F32)))
    i = np.arange(C)
    sid, p = i // L, (i % L).astype(np.float32)
    causal = (sid[:, None] == sid[None, :]) & (i[:, None] >= i[None, :])
    diff = jnp.asarray(np.where(causal, p[:, None] - p[None, :], 0.0).astype(np.float32))
    dec = jnp.where(causal[None], jnp.exp(diff[None] * log_gamma[:, None, None]), 0.0)
    lg_l = jnp.repeat(log_gamma, RET_DV)[None, :]
    pj = jnp.asarray(p)[:, None]
    return dict(
        decc=jnp.concatenate([dec[h] for h in range(NH)], axis=1),
        dq=jnp.exp((pj + 1.0) * lg_l),
        dk=jnp.exp((L - 1.0 - pj) * lg_l),
        dch=jnp.exp(L * lg_l),
    )


def _rope_tables(pos):
    half = RET_DK // 2
    inv = ROPE_BASE ** (-jnp.arange(half, dtype=F32) / half)
    ang = pos.astype(F32)[:, None] * inv[None, :]
    cos, sin = jnp.cos(ang), jnp.sin(ang)
    return (jnp.tile(jnp.concatenate([cos, cos], axis=1), (1, NH)),
            jnp.tile(jnp.concatenate([-sin, sin], axis=1), (1, NH)))


def _rotary(x, cos, sin):
    lane = lax.broadcasted_iota(jnp.int32, x.shape, 1)
    swapped = jnp.where((lane % RET_DK) < RET_DK // 2,
                        pltpu.roll(x, x.shape[1] - RET_DK // 2, 1), pltpu.roll(x, RET_DK // 2, 1))
    return x * cos + swapped * sin


def _ret_intra(z, cos, sin, k):
    q = _rotary(z[:, Z_RQ:Z_RQ + 256], cos, sin)
    kk = _rotary(z[:, Z_RK:Z_RK + 256], cos, sin) * (RET_DK ** -0.5)
    v = z[:, Z_RV:Z_RV + 256]
    sc = _mm(q, _stack_bd(kk.astype(BF16), RET_DK), NT) * k["decc"][...]
    o = _mm(sc, _stack_bd(v.astype(BF16), RET_DV))
    return o, q * k["dq"][...], kk * k["dk"][...], v


def _ret_out(o, rg, k):
    g = k["g64"][...]
    d = o - _mm_data_const(o, g)
    var = _mm_data_const(d * d, g)
    return _silu(rg) * (d * lax.rsqrt(var + GN_EPS) * k["ret_gn_w"][...] + k["ret_gn_b"][...])


def _gla_intra(z, k, nlev):
    sm = z[:, Z_SM:Z_SM + LANES]
    gk = _log_sigmoid(_mm(sm, k["gwp"][...]) + k["gla_gate_b"][...]) * (1.0 / GLA_GATE_NORM)
    q = z[:, Z_LQ:Z_LQ + 128] * (GLA_DK ** -0.5)
    kk = z[:, Z_LK:Z_LK + 128]
    v = z[:, Z_LV:Z_LV + 256]
    cs = _mm_const_data(k["cumg"][...], gk)
    b, blast = cs[:C], cs[C:2 * C]
    sc = _mm(q, _stack_bd(kk.astype(BF16), GLA_DK), NT) * k["pmc"][0]
    for li in range(nlev):
        e = jnp.exp(cs[(2 + li) * C:(3 + li) * C])
        sc = sc + _mm(q * e, _stack_bd((kk * e).astype(BF16), GLA_DK), NT) * k["pmc"][li + 1]
    o = _mm(sc, _stack_bd(v.astype(BF16), GLA_DV))
    return o, q * jnp.exp(b), kk * jnp.exp(blast - b), v, b, blast


def _gla_out(o, lg, k):
    ms = _mm_data_const(o * o, k["g64"][...])
    return o * lax.rsqrt(ms + EPS) * k["gla_norm_w"][...] * _silu(lg)


def _lane_col(blk, lane0):
    lane = lax.broadcasted_iota(jnp.int32, blk.shape, 1)
    st = jnp.concatenate([jnp.where(lane == lane0 + h, blk, 0.0) for h in range(NH)], axis=0)
    return jnp.sum(st, axis=-1, keepdims=True)


def _gdn_front(conv, sm, k):
    def l2n(x):
        return x * lax.rsqrt(jnp.sum(x * x, axis=-1, keepdims=True) + EPS)

    qs = [l2n(conv[h]) * (GDN_DK ** -0.5) for h in range(NH)]
    ks = [l2n(conv[NH + h]) for h in range(NH)]
    vst = jnp.concatenate(conv[2 * NH:], axis=0)
    qst = jnp.concatenate(qs, axis=0)
    kst = jnp.concatenate(ks, axis=0)
    kq = []
    for p in range(NH // 2):
        kp = jnp.concatenate(ks[2 * p:2 * p + 2], axis=1).astype(BF16)
        lane = lax.broadcasted_iota(jnp.int32, kp.shape, 1) // GDN_DK
        kbd = jnp.concatenate([jnp.where(lane == hl, kp, jnp.zeros_like(kp)) for hl in range(2)], axis=0)
        lhs = jnp.concatenate([kp, jnp.concatenate(qs[2 * p:2 * p + 2], axis=1).astype(BF16)], axis=0)
        kq.append(lax.dot_general(lhs, kbd, NT, preferred_element_type=F32))
    kq = jnp.concatenate(kq, axis=1)
    gd = -jnp.exp(k["alog"][...]) * _softplus(sm + k["dtb"][...])
    beta = jax.nn.sigmoid(sm)
    cs = _mm_const_data(k["cumd"][...], gd)
    bcol = _lane_col(cs[:C], SM_DA)
    blcol = _lane_col(cs[C:], SM_DA)
    betacol = _lane_col(beta, SM_DB)
    bcolc = _col_to_compact(bcol)
    browc = jnp.sum(bcolc * k["eye4"][...], axis=0, keepdims=True)
    causal = k["causal4"][...] > 0.5
    decc = jnp.where(causal, jnp.exp(jnp.where(causal, bcolc - browc, 0.0)), 0.0)
    ac = kq[:C] * decc * _col_to_compact(betacol)
    qkc = kq[C:] * decc
    return dict(ac=ac, qkc=qkc, qst=qst, kst=kst, vst=vst, bcol=bcol, blcol=blcol, betacol=betacol)


def _gdn_inverse(acs, k, nlev):
    xs = [k["eye4"][...] - a * k["lmc"][0] for a in acs]
    for li in range(1, nlev):
        ms = [a * k["lmc"][li] for a in acs]
        ys = [_mm(m, _stack_bd(x.astype(BF16), C)) for m, x in zip(ms, xs)]
        xs = [x - _mm(x, _stack_bd(y.astype(BF16), C)) for x, y in zip(xs, ys)]
    return xs


def _gdn_back(f, xc):
    kb = f["kst"] * f["betacol"]
    rhs = jnp.concatenate([f["vst"] * f["betacol"], kb * jnp.exp(f["bcol"])], axis=1)
    sol = _mm(_stack_bd(xc.astype(BF16), C), rhs)
    return dict(qst=f["qst"], solv=sol[:, :GDN_DV], solk=sol[:, GDN_DV:],
                qk=_stack_bd(f["qkc"].astype(BF16), C), ebcol=jnp.exp(f["bcol"]),
                kdec=f["kst"] * jnp.exp(f["blcol"] - f["bcol"]), eblast=jnp.exp(f["blcol"]))


def _gdn_out(o, dg, k):
    w = k["gdn_norm_w"][...]
    y = jnp.concatenate([_rms_rows(o[:, h * GDN_DV:(h + 1) * GDN_DV], w) for h in range(NH)], axis=1)
    return y * _silu(dg)


_PARAM_NAMES = ("alog", "dtb", "gwp", "gla_gate_b", "ret_gn_w", "ret_gn_b", "gdn_norm_w", "gla_norm_w", "conv_w")
_CONST_NAMES = ("cumg", "cumd", "pmc", "lmc", "causal4", "eye4", "g64", "bdfull", "glabd", "seqsel",
                "decc", "dq", "dk", "dch")


def _mixer_prompt_kernel(*refs, tc, nlev):
    n_in = 6 + len(_PARAM_NAMES) + len(_CONST_NAMES)
    x_ref, nw_ref, win_ref, wout_ref, cos_ref, sin_ref = refs[:6]
    k = dict(zip(_PARAM_NAMES + _CONST_NAMES, refs[6:n_in]))
    o_ref, sret_ref, sgdn_ref, sgla_ref, conv_ref = refs[n_in:n_in + 5]
    z_ref, xp_ref, cv_ref, y_ref = refs[n_in + 5:]
    t = pl.program_id(1)
    nch = tc // C
    ng = CONV_DIM // LANES

    @pl.when(t == 0)
    def _():
        sret_ref[...] = jnp.zeros_like(sret_ref)
        sgdn_ref[...] = jnp.zeros_like(sgdn_ref)
        sgla_ref[...] = jnp.zeros_like(sgla_ref)
        xp_ref[:, pl.ds(0, 8), :] = jnp.zeros((ng, 8, LANES), F32)

    h = _rms_rows(x_ref[...], nw_ref[...]).astype(BF16)
    zc = _proj(h, win_ref[Z_CONV:Z_CONV + CONV_DIM, :])
    for g in range(ng):
        xp_ref[g, pl.ds(8, tc), :] = zc[:, g * LANES:(g + 1) * LANES]
    z_ref[:, :Z_CONV] = _proj(h, win_ref[:Z_CONV, :])
    z_ref[:, Z_DG:] = _proj(h, win_ref[Z_DG:, :])

    cw = k["conv_w"]
    nb = tc // 8
    for g in range(ng):
        wg = [cw[i:i + 1, g * LANES:(g + 1) * LANES] for i in range(CONV_W)]
        taps = {s: xp_ref[g, pl.ds(s, nb, stride=8), :] for s in range(5, 5 + 8 + CONV_W - 1)}
        for j in range(8):
            acc = taps[j + 8] * wg[3]
            for i in range(CONV_W - 1):
                acc = acc + taps[j + 5 + i] * wg[i]
            cv_ref[g, pl.ds(j, nb, stride=8), :] = _silu(acc)
        xp_ref[g, pl.ds(0, 8), :] = xp_ref[g, pl.ds(tc, 8), :]

    zs = [z_ref[pl.ds(c * C, C), :] for c in range(nch)]
    fronts = [_gdn_front([cv_ref[g, pl.ds(c * C, C), :] for g in range(ng)], z[:, Z_SM:Z_SM + LANES], k)
              for c, z in enumerate(zs)]
    xcs = _gdn_inverse([f["ac"] for f in fronts], k, nlev)
    gs = [_gdn_back(f, xc) for f, xc in zip(fronts, xcs)]
    rets = [_ret_intra(z, cos_ref[pl.ds(c * C, C), :], sin_ref[pl.ds(c * C, C), :], k) for c, z in enumerate(zs)]
    glas = [_gla_intra(z, k, nlev) for z in zs]

    for c in range(nch):
        rows = pl.ds(c * C, C)

        o, qd, kd, v = rets[c]
        s = sret_ref[0]
        z_ref[rows, Z_RQ:Z_RQ + 256] = o + _mm(qd, s)
        sret_ref[0] = s * k["dch"][...] + _mm(kd, v, TN) * k["bdfull"][...]

        g = gs[c]
        us, qss = [], []
        for hh in range(NH):
            lhs = jnp.concatenate([g["solk"][hh * C:(hh + 1) * C], g["qst"][hh * C:(hh + 1) * C]], axis=0)
            r = _mm(lhs, sgdn_ref[0, hh])
            us.append(g["solv"][hh * C:(hh + 1) * C] - r[:C])
            qss.append(r[C:])
        ost = g["ebcol"] * jnp.concatenate(qss, axis=0) + _mm(g["qk"], jnp.concatenate(us, axis=0))
        for hh in range(NH):
            sl = slice(hh * C, (hh + 1) * C)
            sgdn_ref[0, hh] = (g["eblast"][hh * C:hh * C + 1] * sgdn_ref[0, hh]
                               + _mm(g["kdec"][sl], us[hh], TN))
        z_ref[rows, Z_CONV:Z_CONV + NH * GDN_DV] = jnp.concatenate(
            [ost[hh * C:(hh + 1) * C] for hh in range(NH)], axis=1)

        o, qe, ke, v, b, _ = glas[c]
        s = sgla_ref[0]
        z_ref[rows, Z_LQ:Z_LQ + 256] = o + _mm(qe, s)
        escale = jnp.exp(b[C - 8:, :].T[:, 7:8])
        sgla_ref[0] = s * escale + _mm(ke, v, TN) * k["glabd"][...]

    y_r = _ret_out(z_ref[:, Z_RQ:Z_RQ + 256], z_ref[:, Z_RG:Z_RG + 256], k)
    y_d = _gdn_out(z_ref[:, Z_CONV:Z_CONV + NH * GDN_DV], z_ref[:, Z_DG:Z_DG + 512], k)
    y_l = _gla_out(z_ref[:, Z_LQ:Z_LQ + 256], z_ref[:, Z_LG:Z_LG + 256], k)
    y_ref[...] = jnp.concatenate([y_r, y_d, y_l], axis=1).astype(BF16)

    o_ref[...] = x_ref[...] + jnp.dot(y_ref[...], wout_ref[...].astype(BF16), preferred_element_type=F32)

    @pl.when(t == pl.num_programs(1) - 1)
    def _():
        for g in range(ng):
            conv_ref[0, :, g * LANES:(g + 1) * LANES] = xp_ref[g, pl.ds(5, 3), :]


def _mixer_prompt(x, l, nw, win, wout, cos, sin, params, consts, *, bsz, tlen, tc=MIX_TC):
    nlev = len(_levels(C))
    nt = tlen // tc
    in_specs = [
        pl.BlockSpec((tc, D_MODEL), lambda b, t: (b * nt + t, 0)),
        _layer_resident(nw, l), _layer_resident(win, l), _layer_resident(wout, l),
        pl.BlockSpec((tc, 256), lambda b, t: (t, 0)),
        pl.BlockSpec((tc, 256), lambda b, t: (t, 0)),
    ] + [_layer_resident(params[n], l) for n in _PARAM_NAMES] + [_resident(consts[n]) for n in _CONST_NAMES]
    extras = [params[n] for n in _PARAM_NAMES] + [consts[n] for n in _CONST_NAMES]
    out_shape = (
        jax.ShapeDtypeStruct(x.shape, F32),
        jax.ShapeDtypeStruct((bsz, 256, 256), F32),
        jax.ShapeDtypeStruct((bsz, NH, GDN_DK, GDN_DV), F32),
        jax.ShapeDtypeStruct((bsz, NH * GLA_DK, NH * GLA_DV), F32),
        jax.ShapeDtypeStruct((bsz, CONV_W - 1, CONV_DIM), F32),
    )
    out_specs = (
        pl.BlockSpec((tc, D_MODEL), lambda b, t: (b * nt + t, 0)),
        pl.BlockSpec((1, 256, 256), lambda b, t: (b, 0, 0)),
        pl.BlockSpec((1, NH, GDN_DK, GDN_DV), lambda b, t: (b, 0, 0, 0)),
        pl.BlockSpec((1, NH * GLA_DK, NH * GLA_DV), lambda b, t: (b, 0, 0)),
        pl.BlockSpec((1, CONV_W - 1, CONV_DIM), lambda b, t: (b, 0, 0)),
    )
    return pl.pallas_call(
        functools.partial(_mixer_prompt_kernel, tc=tc, nlev=nlev),
        grid=(bsz, nt),
        in_specs=in_specs,
        out_specs=out_specs,
        out_shape=out_shape,
        input_output_aliases={0: 0},
        scratch_shapes=[
            pltpu.VMEM((tc, NZ), F32),
            pltpu.VMEM((CONV_DIM // LANES, tc + 8, LANES), F32),
            pltpu.VMEM((CONV_DIM // LANES, tc, LANES), F32),
            pltpu.VMEM((tc, D_MODEL), BF16),
        ],
        compiler_params=pltpu.CompilerParams(
            dimension_semantics=("parallel", "arbitrary"), vmem_limit_bytes=VMEM_LIMIT),
        name="mixer_prompt",
    )(x, nw, win, wout, cos, sin, *extras)


SEQ_S = 4
NSEQ = C // SEQ_S


def _inproj_kernel(x_ref, nw_ref, win_ref, z_ref):
    z_ref[...] = _proj(_rms_rows(x_ref[...], nw_ref[...]).astype(BF16), win_ref[...])


def _sample_inproj(x, l, nw, win, *, row0, n):
    return pl.pallas_call(
        _inproj_kernel,
        grid=(1,),
        in_specs=[pl.BlockSpec((n, D_MODEL), lambda i: (row0 // n, 0)), _layer_resident(nw, l),
                  _layer_resident(win, l)],
        out_specs=pl.BlockSpec((n, NZ), lambda i: (0, 0)),
        out_shape=jax.ShapeDtypeStruct((n, NZ), F32),
        compiler_params=pltpu.CompilerParams(
            dimension_semantics=("arbitrary",), vmem_limit_bytes=VMEM_LIMIT),
        name="sample_inproj",
    )(x, nw, win)


def _mixer_sample_kernel(*refs, nlev):
    n_in = 9 + len(_PARAM_NAMES) + len(_CONST_NAMES)
    (x_ref, zin_ref, wout_ref, cos_ref, sin_ref,
     sret_in, sgdn_in, sgla_in, cbuf_ref) = refs[:9]
    k = dict(zip(_PARAM_NAMES + _CONST_NAMES, refs[9:n_in]))
    o_ref, sret_out, sgdn_out, sgla_out, cout_ref = refs[n_in + 3:n_in + 8]

    x = x_ref[...]
    z = zin_ref[...]

    rowi = lax.broadcasted_iota(jnp.int32, (C, 1), 0)
    tpos = rowi % SEQ_S

    xc = z[:, Z_CONV:Z_CONV + CONV_DIM]
    cb = cbuf_ref[...]
    cw = k["conv_w"]
    acc = xc * cw[3:4, :]
    for i in range(CONV_W - 1):
        cur = pltpu.roll(xc, 3 - i, 0)
        old = cb if i == 0 else pltpu.roll(cb, C - i, 0)
        acc = acc + jnp.where(tpos + i >= 3, cur, old) * cw[i:i + 1, :]
    cout_ref[...] = pltpu.roll(xc, C - 1, 0)
    conv = _silu(acc)

    seq_of_row = rowi // SEQ_S
    seq_of_lane = lax.broadcasted_iota(jnp.int32, (1, C), 1) // SEQ_S

    o, qd, kd, v = _ret_intra(z, cos_ref[...], sin_ref[...], k)
    dch = k["dch"][...]
    kdt = kd.T
    accs = [jnp.zeros((C, RET_DV), F32) for _ in range(NH)]
    for n in range(NSEQ):
        rm = seq_of_row == n
        lm = seq_of_lane == n
        for hh in range(NH):
            sl = slice(hh * RET_DK, (hh + 1) * RET_DK)
            s = sret_in[n, hh]
            accs[hh] = jnp.where(rm, _mm(qd[:, sl], s), accs[hh])
            sret_out[n, hh] = s * dch[:, sl] + _mm(jnp.where(lm, kdt[sl], 0.0), v[:, sl])
    o = o + jnp.concatenate(accs, axis=1)
    y_r = _ret_out(o, z[:, Z_RG:Z_RG + 256], k)

    f = _gdn_front([conv[:, gi * LANES:(gi + 1) * LANES] for gi in range(CONV_DIM // LANES)],
                   z[:, Z_SM:Z_SM + LANES], k)
    g = _gdn_back(f, _gdn_inverse([f["ac"]], k, nlev)[0])
    rowi2 = lax.broadcasted_iota(jnp.int32, (2 * C, 1), 0)
    seq_of_row2 = (rowi2 % C) // SEQ_S
    lhs = [jnp.concatenate([g["solk"][hh * C:(hh + 1) * C], g["qst"][hh * C:(hh + 1) * C]], axis=0)
           for hh in range(NH)]
    accs = [jnp.zeros((2 * C, GDN_DV), F32) for _ in range(NH)]
    for n in range(NSEQ):
        rm2 = seq_of_row2 == n
        for hh in range(NH):
            accs[hh] = jnp.where(rm2, _mm(lhs[hh], sgdn_in[n, hh]), accs[hh])
    us = [g["solv"][hh * C:(hh + 1) * C] - accs[hh][:C] for hh in range(NH)]
    ost = (g["ebcol"] * jnp.concatenate([accs[hh][C:] for hh in range(NH)], axis=0)
           + _mm(g["qk"], jnp.concatenate(us, axis=0)))
    kdts = [g["kdec"][hh * C:(hh + 1) * C].T for hh in range(NH)]
    for n in range(NSEQ):
        lm = seq_of_lane == n
        for hh in range(NH):
            scale = g["eblast"][hh * C + n * SEQ_S:hh * C + n * SEQ_S + 1]
            sgdn_out[n, hh] = scale * sgdn_in[n, hh] + _mm(jnp.where(lm, kdts[hh], 0.0), us[hh])
    y_d = _gdn_out(jnp.concatenate([ost[hh * C:(hh + 1) * C] for hh in range(NH)], axis=1),
                   z[:, Z_DG:Z_DG + 512], k)

    o, qe, ke, v, b, blast = _gla_intra(z, k, nlev)
    ket = ke.T
    hi, mid = _split2(blast * (1.0 / SEQ_S))
    blt2 = lax.dot_general(jnp.concatenate([hi, mid], axis=1), k["seqsel"][...], TN, preferred_element_type=F32)
    eblt = jnp.exp(blt2[:NH * GLA_DK] + blt2[NH * GLA_DK:])
    accs = [jnp.zeros((C, GLA_DV), F32) for _ in range(NH)]
    for n in range(NSEQ):
        rm = seq_of_row == n
        lm = seq_of_lane == n
        for hh in range(NH):
            sk = slice(hh * GLA_DK, (hh + 1) * GLA_DK)
            sv = slice(hh * GLA_DV, (hh + 1) * GLA_DV)
            s = sgla_in[n, hh]
            accs[hh] = jnp.where(rm, _mm(qe[:, sk], s), accs[hh])
            sgla_out[n, hh] = s * eblt[sk, n:n + 1] + _mm(jnp.where(lm, ket[sk], 0.0), v[:, sv])
    o = o + jnp.concatenate(accs, axis=1)
    y_l = _gla_out(o, z[:, Z_LG:Z_LG + 256], k)

    y = jnp.concatenate([y_r, y_d, y_l], axis=1).astype(BF16)
    o_ref[...] = x + jnp.dot(y, wout_ref[...].astype(BF16), preferred_element_type=F32)


def _mixer_sample(x, l, nw, win, wout, cos, sin, sret, sgdn, sgla, cbuf, acc_ret, acc_gdn, acc_gla,
                  params, consts, *, row0):
    nb = sret.shape[1]
    n = nb * SEQ_S
    nlev = len(_levels(SEQ_S))
    blk0 = row0 // C

    def st_spec(dk, dv):
        return pl.BlockSpec((None, NSEQ, NH, dk, dv), lambda i: (l, i, 0, 0, 0))

    st_specs = [st_spec(RET_DK, RET_DV), st_spec(GDN_DK, GDN_DV), st_spec(GLA_DK, GLA_DV)]
    any_spec = pl.BlockSpec(memory_space=pl.ANY)
    z = _sample_inproj(x, l, nw, win, row0=row0, n=n)
    in_specs = [
        pl.BlockSpec((C, D_MODEL), lambda i: (blk0 + i, 0)),
        pl.BlockSpec((C, NZ), lambda i: (i, 0)),
        _layer_resident(wout, l), _resident(cos), _resident(sin),
    ] + st_specs + [pl.BlockSpec((C, CONV_DIM), lambda i: (i, 0))] + [
        _layer_resident(params[nm], l) for nm in _PARAM_NAMES] + [
        _resident(consts[nm]) for nm in _CONST_NAMES] + [any_spec] * 3
    extras = [params[nm] for nm in _PARAM_NAMES] + [consts[nm] for nm in _CONST_NAMES]
    n_in = len(in_specs)
    out_shape = (
        jax.ShapeDtypeStruct(x.shape, F32),
        jax.ShapeDtypeStruct(acc_ret.shape, F32),
        jax.ShapeDtypeStruct(acc_gdn.shape, F32),
        jax.ShapeDtypeStruct(acc_gla.shape, F32),
        jax.ShapeDtypeStruct((n, CONV_DIM), F32),
    )
    out_specs = (pl.BlockSpec((C, D_MODEL), lambda i: (blk0 + i, 0)),) + tuple(st_specs) + (
        pl.BlockSpec((C, CONV_DIM), lambda i: (i, 0)),)
    return pl.pallas_call(
        functools.partial(_mixer_sample_kernel, nlev=nlev),
        grid=(n // C,),
        in_specs=in_specs,
        out_specs=out_specs,
        out_shape=out_shape,
        input_output_aliases={0: 0, n_in - 3: 1, n_in - 2: 2, n_in - 1: 3},
        compiler_params=pltpu.CompilerParams(
            dimension_semantics=("parallel",), vmem_limit_bytes=VMEM_LIMIT),
        name="mixer_sample",
    )(x, z, wout, cos, sin, sret, sgdn, sgla, cbuf, *extras, acc_ret, acc_gdn, acc_gla)


def _permute_w_in(w):
    wt = jnp.swapaxes(w, 1, 2)
    pad = jnp.zeros((w.shape[0], NZ - Z_SM - 24, w.shape[1]), w.dtype)
    return jnp.concatenate(
        [wt[:, 0:2560], wt[:, 2568:3080], wt[:, 3080:3592], wt[:, 3608:3864], wt[:, 2560:2568],
         wt[:, 3592:3608], pad], axis=1).astype(BF16)


def _rows(v, width=None):
    v = v.astype(F32)[:, None, :]
    if width is not None and v.shape[-1] < width:
        v = jnp.pad(v, ((0, 0), (0, 0), (0, width - v.shape[-1])))
    return v


def kernel(x_prompt, x_sample, state_ret, state_gdn, state_gdn_conv, state_gla, norm_ffn1, ffn1_w1, ffn1_w3, ffn1_w2, norm_mix, w_in, ret_gn_w, ret_gn_b, gdn_conv_w, gdn_A_log, gdn_dt_bias, gdn_norm_w, gla_gate_w, gla_gate_b, gla_norm_w, w_out, norm_ffn2, ffn2_w1, ffn2_w3, ffn2_w2, norm_final):
    bp, tp, _ = x_prompt.shape
    bs, ts, _ = x_sample.shape
    n_prompt, n_sample = bp * tp, bs * ts
    assert ts == SEQ_S and tp % MIX_TC == 0 and n_sample % C == 0 and n_prompt % C == 0

    consts_p = dict(_chunk_consts(C), **_ret_consts(C))
    consts_s = dict(_chunk_consts(SEQ_S), **_ret_consts(SEQ_S))
    cos_p, sin_p = _rope_tables(jnp.arange(tp, dtype=jnp.int32))
    cos_s, sin_s = _rope_tables(PAST_LEN + (jnp.arange(C, dtype=jnp.int32) % SEQ_S))

    params = dict(
        alog=_rows(gdn_A_log, LANES), dtb=_rows(gdn_dt_bias, LANES),
        gwp=jnp.pad(gla_gate_w, ((0, 0), (SM_LLR, LANES - SM_LLR - GLA_RANK), (0, 0))).astype(BF16),
        gla_gate_b=_rows(gla_gate_b), ret_gn_w=_rows(ret_gn_w), ret_gn_b=_rows(ret_gn_b),
        gdn_norm_w=_rows(gdn_norm_w), gla_norm_w=_rows(jnp.tile(gla_norm_w, (1, NH))),
        conv_w=gdn_conv_w.astype(F32),
    )
    win = _permute_w_in(w_in)
    wout = w_out.astype(F32)
    f1 = (_rows(norm_ffn1), ffn1_w1.astype(F32), ffn1_w3.astype(F32), ffn1_w2.astype(F32))
    f2 = (_rows(norm_ffn2), ffn2_w1.astype(F32), ffn2_w3.astype(F32), ffn2_w2.astype(F32))
    nw = _rows(norm_mix)
    fw = norm_final.astype(F32)[None, :]
    sret_in, sgdn_in, sgla_in = state_ret.astype(F32), state_gdn.astype(F32), state_gla.astype(F32)
    cbuf = jnp.pad(state_gdn_conv.astype(F32), ((0, 0), (0, 0), (0, 1), (0, 0))).reshape(DEPTH, n_sample, CONV_DIM)
    acc_ret, acc_gdn, acc_gla = (lax.empty(s.shape, F32) for s in (sret_in, sgdn_in, sgla_in))

    ffn = functools.partial(_ffn, n_prompt=n_prompt, n_sample=n_sample)
    xs = (x_prompt.reshape(n_prompt, D_MODEL), x_sample.reshape(n_sample, D_MODEL))
    outs_p = [[] for _ in range(4)]
    conv_s = []
    for l in range(DEPTH):
        x = ffn(xs, l, *f1, fw, first=(l == 0), last=False)
        x, sret, sgdn, sgla, conv = _mixer_prompt(x, l, nw, win, wout, cos_p, sin_p, params, consts_p,
                                                  bsz=bp, tlen=tp)
        outs_p[0].append(jnp.stack([sret[:, h * 64:(h + 1) * 64, h * 64:(h + 1) * 64] for h in range(NH)], axis=1))
        outs_p[1].append(sgdn)
        outs_p[2].append(conv)
        outs_p[3].append(jnp.stack([sgla[:, h * 32:(h + 1) * 32, h * 64:(h + 1) * 64] for h in range(NH)], axis=1))
        x, acc_ret, acc_gdn, acc_gla, cout = _mixer_sample(
            x, l, nw, win, wout, cos_s, sin_s, sret_in, sgdn_in, sgla_in, cbuf[l], acc_ret, acc_gdn, acc_gla,
            params, consts_s, row0=n_prompt)
        conv_s.append(cout.reshape(bs, ts, CONV_DIM)[:, :CONV_W - 1])
        xs = (ffn((x,), l, *f2, fw, first=False, last=(l == DEPTH - 1)),)

    y_prompt, y_sample = xs[0]
    dts = (state_ret.dtype, state_gdn.dtype, state_gdn_conv.dtype, state_gla.dtype)
    sp = [jnp.stack(o).astype(d) for o, d in zip(outs_p, dts)]
    return (y_prompt.reshape(bp, tp, D_MODEL), y_sample.reshape(bs, ts, D_MODEL), sp[0], sp[1], sp[2], sp[3],
            acc_ret.astype(dts[0]), acc_gdn.astype(dts[1]), jnp.stack(conv_s).astype(dts[2]), acc_gla.astype(dts[3]))
```

```python
import functools

import numpy as np
import jax
import jax.numpy as jnp
from jax import lax
from jax.experimental import pallas as pl
from jax.experimental.pallas import tpu as pltpu

F32 = jnp.float32
BF16 = jnp.bfloat16

D_MODEL = 1024
DEPTH = 4
PAST_LEN = 16384
NH = 4
RET_DK = 64
RET_DV = 64
GDN_DK = 128
GDN_DV = 128
GLA_DK = 32
GLA_DV = 64
GLA_RANK = 16
GLA_GATE_NORM = 16.0
CONV_W = 4
CONV_DIM = NH * (2 * GDN_DK + GDN_DV)
D_FF = 2816
ROPE_BASE = 10000.0
EPS = 1e-6
GN_EPS = 1e-5

C = 64
R = NH * C
LANES = 128

Z_RQ, Z_RK, Z_RV, Z_RG = 0, 256, 512, 768
Z_CONV = 1024
Z_DG = 2560
Z_LQ, Z_LK, Z_LV, Z_LG = 3072, 3200, 3328, 3584
Z_SM = 3840
NZ = 3968
SM_DA, SM_DB, SM_LLR = 0, 4, 8

NN = (((1,), (0,)), ((), ()))
NT = (((1,), (1,)), ((), ()))
TN = (((0,), (0,)), ((), ()))

VMEM_LIMIT = 56 * 1024 * 1024
FFN_TM = 512
FFN_TF = 256
MIX_TC = 512


def _mm(a, b, dims=NN):
    return lax.dot_general(a.astype(BF16), b.astype(BF16), dims, preferred_element_type=F32)


def _proj(h, w_rows):
    return lax.dot_general(h, w_rows, NT, preferred_element_type=F32)


def _split2(x):
    hi = x.astype(BF16)
    return hi, (x - hi.astype(F32)).astype(BF16)


def _mm_data_const(x, cb):
    hi, mid = _split2(x)
    m = x.shape[0]
    r = lax.dot_general(jnp.concatenate([hi, mid], axis=0), cb, NN, preferred_element_type=F32)
    return r[:m] + r[m:]


def _mm_const_data(cb, x):
    hi, mid = _split2(x)
    n = x.shape[1]
    r = lax.dot_general(cb, jnp.concatenate([hi, mid], axis=1), NN, preferred_element_type=F32)
    return r[:, :n] + r[:, n:]


def _silu(x):
    hx = 0.5 * x
    return hx + hx * jnp.tanh(hx)


def _softplus(x):
    return jnp.maximum(x, 0.0) + jnp.log1p(jnp.exp(-jnp.abs(x)))


def _log_sigmoid(x):
    return jnp.minimum(x, 0.0) - jnp.log1p(jnp.exp(-jnp.abs(x)))


def _rms_rows(x, w):
    ms = jnp.mean(x * x, axis=-1, keepdims=True)
    return x * lax.rsqrt(ms + EPS) * w


def _stack_bd(x, width):
    lane = lax.broadcasted_iota(jnp.int32, x.shape, 1) // width
    zero = jnp.zeros_like(x)
    return jnp.concatenate([jnp.where(lane == h, x, zero) for h in range(NH)], axis=0)


def _col_to_compact(col):
    lane = lax.broadcasted_iota(jnp.int32, (C, R), 1) // C
    out = jnp.zeros((C, R), F32)
    for h in range(NH):
        out = jnp.where(lane == h, col[h * C:(h + 1) * C], out)
    return out


def _ffn_kernel(*refs, first, last, npb, tf):
    n_x = 2 if first else 1
    xrefs, (nw_ref, w1_ref, w3_ref, w2_ref, fw_ref) = refs[:n_x], refs[n_x:n_x + 5]
    n_o = 2 if last else 1
    orefs = refs[n_x + 5:n_x + 5 + n_o]
    h_ref, acc_ref = refs[n_x + 5 + n_o:]
    i = pl.program_id(0)
    x = jnp.where(i < npb, xrefs[0][...], xrefs[1][...]) if first else xrefs[0][...]
    h_ref[...] = _rms_rows(x, nw_ref[...]).astype(BF16)
    for j in range(D_FF // tf):
        h = h_ref[...]
        a = jnp.dot(h, w1_ref[:, j * tf:(j + 1) * tf].astype(BF16), preferred_element_type=F32)
        g = jnp.dot(h, w3_ref[:, j * tf:(j + 1) * tf].astype(BF16), preferred_element_type=F32)
        p = jnp.dot((_silu(a) * g).astype(BF16), w2_ref[j * tf:(j + 1) * tf, :].astype(BF16),
                    preferred_element_type=F32)
        if j == 0:
            acc_ref[...] = p
        else:
            acc_ref[...] += p
    y = x + 0.5 * acc_ref[...]
    if last:
        y = _rms_rows(y, fw_ref[...])

        @pl.when(i < npb)
        def _():
            orefs[0][...] = y

        @pl.when(i >= npb)
        def _():
            orefs[1][...] = y
    else:
        orefs[0][...] = y


def _resident(a):
    nd = a.ndim
    return pl.BlockSpec(a.shape, lambda *_, _n=nd: (0,) * _n, pipeline_mode=pl.Buffered(1))


def _layer_resident(a, l):
    nd = a.ndim - 1
    return pl.BlockSpec((None,) + a.shape[1:], lambda *_, _n=nd: (l,) + (0,) * _n, pipeline_mode=pl.Buffered(1))


def _ffn(xs, l, nw, w1, w3, w2, fw, *, first, last, n_prompt, n_sample, tm=FFN_TM, tf=FFN_TF):
    assert n_sample == tm and n_prompt % tm == 0
    npb = n_prompt // tm
    prompt_blk = pl.BlockSpec((tm, D_MODEL), lambda i: (jnp.minimum(i, npb - 1), 0))
    sample_blk = pl.BlockSpec((tm, D_MODEL), lambda i: (0, 0))
    unified_blk = pl.BlockSpec((tm, D_MODEL), lambda i: (i, 0))
    if last:
        out_specs = (prompt_blk, sample_blk)
        out_shape = (jax.ShapeDtypeStruct((n_prompt, D_MODEL), F32), jax.ShapeDtypeStruct((n_sample, D_MODEL), F32))
    else:
        out_specs = unified_blk
        out_shape = jax.ShapeDtypeStruct((n_prompt + n_sample, D_MODEL), F32)
    return pl.pallas_call(
        functools.partial(_ffn_kernel, first=first, last=last, npb=npb, tf=tf),
        grid=(npb + 1,),
        in_specs=([prompt_blk, sample_blk] if first else [unified_blk]) + [
            _layer_resident(nw, l), _layer_resident(w1, l), _layer_resident(w3, l), _layer_resident(w2, l),
            _resident(fw)],
        out_specs=out_specs,
        out_shape=out_shape,
        scratch_shapes=[pltpu.VMEM((tm, D_MODEL), BF16), pltpu.VMEM((tm, D_MODEL), F32)],
        compiler_params=pltpu.CompilerParams(
            dimension_semantics=("arbitrary",), vmem_limit_bytes=VMEM_LIMIT),
        name="ffn",
    )(*xs, nw, w1, w3, w2, fw)


def _levels(L):
    return [s for s in (1, 2, 4, 8, 16, 32) if s < L]


def _chunk_consts(L):
    i = np.arange(C)
    sid, p = i // L, i % L
    same = sid[:, None] == sid[None, :]
    causal = same & (i[:, None] >= i[None, :])
    strict = same & (i[:, None] > i[None, :])
    lv = _levels(L)
    cum, pm, lm = [causal, same], [np.eye(C, dtype=bool)], []
    for s in lv:
        blk = p // s
        inblk = same & (blk[:, None] == blk[None, :])
        odd = (blk % 2 == 1)[:, None]
        cum.append(inblk & np.where(odd, i[None, :] <= i[:, None], i[None, :] > i[:, None]))
        pm.append(same & odd & (blk[None, :] == blk[:, None] - 1))
        lm.append(strict & ((p[:, None] // (2 * s)) == (p[None, :] // (2 * s))) & (blk[:, None] != blk[None, :]))
    r = np.arange(R)
    bdfull = (r[:, None] // C) == (r[None, :] // C)
    wide = lambda m: np.tile(m, (1, NH))
    f = lambda m: jnp.asarray(np.asarray(m, dtype=np.float32))
    gla_bd = (np.arange(NH * GLA_DK)[:, None] // GLA_DK) == (np.arange(NH * GLA_DV)[None, :] // GLA_DV)
    seqsel = np.zeros((C, LANES), np.float32)
    seqsel[i, sid] = 1.0
    return dict(
        cumg=f(np.concatenate(cum, axis=0)).astype(BF16),
        cumd=f(np.concatenate(cum[:2], axis=0)).astype(BF16),
        pmc=f(np.stack([wide(m) for m in pm])),
        lmc=f(np.stack([wide(m) for m in lm])),
        causal4=f(wide(causal)), eye4=f(wide(np.eye(C, dtype=bool))),
        g64=(f(bdfull) * (1.0 / 64.0)).astype(BF16), bdfull=f(bdfull), glabd=f(gla_bd),
        seqsel=jnp.asarray(seqsel).astype(BF16),
    )


def _ret_consts(L):
    log_gamma = jnp.log(1.0 - 2.0 ** (-5.0 - jnp.arange(NH, dtype=F32)))
    i = np.arange(C)
    sid, p = i // L, (i % L).astype(np.float32)
    causal = (sid[:, None] == sid[None, :]) & (i[:, None] >= i[None, :])
    diff = jnp.asarray(np.where(causal, p[:, None] - p[None, :], 0.0).astype(np.float32))
    dec = jnp.where(causal[None], jnp.exp(diff[None] * log_gamma[:, None, None]), 0.0)
    lg_l = jnp.repeat(log_gamma, RET_DV)[None, :]
    pj = jnp.asarray(p)[:, None]
    return dict(
        decc=jnp.concatenate([dec[h] for h in range(NH)], axis=1),
        dq=jnp.exp((pj + 1.0) * lg_l),
        dk=jnp.exp((L - 1.0 - pj) * lg_l),
        dch=jnp.exp(L * lg_l),
    )


def _rope_tables(pos):
    half = RET_DK // 2
    inv = ROPE_BASE ** (-jnp.arange(half, dtype=F32) / half)
    ang = pos.astype(F32)[:, None] * inv[None, :]
    cos, sin = jnp.cos(ang), jnp.sin(ang)
    return (jnp.tile(jnp.concatenate([cos, cos], axis=1), (1, NH)),
            jnp.tile(jnp.concatenate([-sin, sin], axis=1), (1, NH)))


def _rotary(x, cos, sin):
    lane = lax.broadcasted_iota(jnp.int32, x.shape, 1)
    swapped = jnp.where((lane % RET_DK) < RET_DK // 2,
                        pltpu.roll(x, x.shape[1] - RET_DK // 2, 1), pltpu.roll(x, RET_DK // 2, 1))
    return x * cos + swapped * sin


def _ret_intra(z, cos, sin, k):
    q = _rotary(z[:, Z_RQ:Z_RQ + 256], cos, sin)
    kk = _rotary(z[:, Z_RK:Z_RK + 256], cos, sin) * (RET_DK ** -0.5)
    v = z[:, Z_RV:Z_RV + 256]
    sc = _mm(q, _stack_bd(kk.astype(BF16), RET_DK), NT) * k["decc"][...]
    o = _mm(sc, _stack_bd(v.astype(BF16), RET_DV))
    return o, q * k["dq"][...], kk * k["dk"][...], v


def _ret_out(o, rg, k):
    g = k["g64"][...]
    d = o - _mm_data_const(o, g)
    var = _mm_data_const(d * d, g)
    return _silu(rg) * (d * lax.rsqrt(var + GN_EPS) * k["ret_gn_w"][...] + k["ret_gn_b"][...])


def _gla_intra(z, k, nlev):
    sm = z[:, Z_SM:Z_SM + LANES]
    gk = _log_sigmoid(_mm(sm, k["gwp"][...]) + k["gla_gate_b"][...]) * (1.0 / GLA_GATE_NORM)
    q = z[:, Z_LQ:Z_LQ + 128] * (GLA_DK ** -0.5)
    kk = z[:, Z_LK:Z_LK + 128]
    v = z[:, Z_LV:Z_LV + 256]
    cs = _mm_const_data(k["cumg"][...], gk)
    b, blast = cs[:C], cs[C:2 * C]
    sc = _mm(q, _stack_bd(kk.astype(BF16), GLA_DK), NT) * k["pmc"][0]
    for li in range(nlev):
        e = jnp.exp(cs[(2 + li) * C:(3 + li) * C])
        sc = sc + _mm(q * e, _stack_bd((kk * e).astype(BF16), GLA_DK), NT) * k["pmc"][li + 1]
    o = _mm(sc, _stack_bd(v.astype(BF16), GLA_DV))
    return o, q * jnp.exp(b), kk * jnp.exp(blast - b), v, b, blast


def _gla_out(o, lg, k):
    ms = _mm_data_const(o * o, k["g64"][...])
    return o * lax.rsqrt(ms + EPS) * k["gla_norm_w"][...] * _silu(lg)


def _lane_col(blk, lane0):
    lane = lax.broadcasted_iota(jnp.int32, blk.shape, 1)
    st = jnp.concatenate([jnp.where(lane == lane0 + h, blk, 0.0) for h in range(NH)], axis=0)
    return jnp.sum(st, axis=-1, keepdims=True)


def _gdn_front(conv, sm, k):
    def l2n(x):
        return x * lax.rsqrt(jnp.sum(x * x, axis=-1, keepdims=True) + EPS)

    qs = [l2n(conv[h]) * (GDN_DK ** -0.5) for h in range(NH)]
    ks = [l2n(conv[NH + h]) for h in range(NH)]
    vst = jnp.concatenate(conv[2 * NH:], axis=0)
    qst = jnp.concatenate(qs, axis=0)
    kst = jnp.concatenate(ks, axis=0)
    kq = []
    for p in range(NH // 2):
        kp = jnp.concatenate(ks[2 * p:2 * p + 2], axis=1).astype(BF16)
        lane = lax.broadcasted_iota(jnp.int32, kp.shape, 1) // GDN_DK
        kbd = jnp.concatenate([jnp.where(lane == hl, kp, jnp.zeros_like(kp)) for hl in range(2)], axis=0)
        lhs = jnp.concatenate([kp, jnp.concatenate(qs[2 * p:2 * p + 2], axis=1).astype(BF16)], axis=0)
        kq.append(lax.dot_general(lhs, kbd, NT, preferred_element_type=F32))
    kq = jnp.concatenate(kq, axis=1)
    gd = -jnp.exp(k["alog"][...]) * _softplus(sm + k["dtb"][...])
    beta = jax.nn.sigmoid(sm)
    cs = _mm_const_data(k["cumd"][...], gd)
    bcol = _lane_col(cs[:C], SM_DA)
    blcol = _lane_col(cs[C:], SM_DA)
    betacol = _lane_col(beta, SM_DB)
    bcolc = _col_to_compact(bcol)
    browc = jnp.sum(bcolc * k["eye4"][...], axis=0, keepdims=True)
    causal = k["causal4"][...] > 0.5
    decc = jnp.where(causal, jnp.exp(jnp.where(causal, bcolc - browc, 0.0)), 0.0)
    ac = kq[:C] * decc * _col_to_compact(betacol)
    qkc = kq[C:] * decc
    return dict(ac=ac, qkc=qkc, qst=qst, kst=kst, vst=vst, bcol=bcol, blcol=blcol, betacol=betacol)


def _gdn_inverse(acs, k, nlev):
    xs = [k["eye4"][...] - a * k["lmc"][0] for a in acs]
    for li in range(1, nlev):
        ms = [a * k["lmc"][li] for a in acs]
        ys = [_mm(m, _stack_bd(x.astype(BF16), C)) for m, x in zip(ms, xs)]
        xs = [x - _mm(x, _stack_bd(y.astype(BF16), C)) for x, y in zip(xs, ys)]
    return xs


def _gdn_back(f, xc):
    kb = f["kst"] * f["betacol"]
    rhs = jnp.concatenate([f["vst"] * f["betacol"], kb * jnp.exp(f["bcol"])], axis=1)
    sol = _mm(_stack_bd(xc.astype(BF16), C), rhs)
    return dict(qst=f["qst"], solv=sol[:, :GDN_DV], solk=sol[:, GDN_DV:],
                qk=_stack_bd(f["qkc"].astype(BF16), C), ebcol=jnp.exp(f["bcol"]),
                kdec=f["kst"] * jnp.exp(f["blcol"] - f["bcol"]), eblast=jnp.exp(f["blcol"]))


def _gdn_out(o, dg, k):
    w = k["gdn_norm_w"][...]
    y = jnp.concatenate([_rms_rows(o[:, h * GDN_DV:(h + 1) * GDN_DV], w) for h in range(NH)], axis=1)
    return y * _silu(dg)


_PARAM_NAMES = ("alog", "dtb", "gwp", "gla_gate_b", "ret_gn_w", "ret_gn_b", "gdn_norm_w", "gla_norm_w", "conv_w")
_CONST_NAMES = ("cumg", "cumd", "pmc", "lmc", "causal4", "eye4", "g64", "bdfull", "glabd", "seqsel",
                "decc", "dq", "dk", "dch")


def _mixer_prompt_kernel(*refs, tc, nlev):
    n_in = 6 + len(_PARAM_NAMES) + len(_CONST_NAMES)
    x_ref, nw_ref, win_ref, wout_ref, cos_ref, sin_ref = refs[:6]
    k = dict(zip(_PARAM_NAMES + _CONST_NAMES, refs[6:n_in]))
    o_ref, sret_ref, sgdn_ref, sgla_ref, conv_ref = refs[n_in:n_in + 5]
    z_ref, xp_ref, cv_ref, y_ref = refs[n_in + 5:]
    t = pl.program_id(1)
    nch = tc // C
    ng = CONV_DIM // LANES

    @pl.when(t == 0)
    def _():
        sret_ref[...] = jnp.zeros_like(sret_ref)
        sgdn_ref[...] = jnp.zeros_like(sgdn_ref)
        sgla_ref[...] = jnp.zeros_like(sgla_ref)
        xp_ref[:, pl.ds(0, 8), :] = jnp.zeros((ng, 8, LANES), F32)

    h = _rms_rows(x_ref[...], nw_ref[...]).astype(BF16)
    zc = _proj(h, win_ref[Z_CONV:Z_CONV + CONV_DIM, :])
    for g in range(ng):
        xp_ref[g, pl.ds(8, tc), :] = zc[:, g * LANES:(g + 1) * LANES]
    z_ref[:, :Z_CONV] = _proj(h, win_ref[:Z_CONV, :])
    z_ref[:, Z_DG:] = _proj(h, win_ref[Z_DG:, :])

    cw = k["conv_w"]
    nb = tc // 8
    for g in range(ng):
        wg = [cw[i:i + 1, g * LANES:(g + 1) * LANES] for i in range(CONV_W)]
        taps = {s: xp_ref[g, pl.ds(s, nb, stride=8), :] for s in range(5, 5 + 8 + CONV_W - 1)}
        for j in range(8):
            acc = taps[j + 8] * wg[3]
            for i in range(CONV_W - 1):
                acc = acc + taps[j + 5 + i] * wg[i]
            cv_ref[g, pl.ds(j, nb, stride=8), :] = _silu(acc)
        xp_ref[g, pl.ds(0, 8), :] = xp_ref[g, pl.ds(tc, 8), :]

    zs = [z_ref[pl.ds(c * C, C), :] for c in range(nch)]
    fronts = [_gdn_front([cv_ref[g, pl.ds(c * C, C), :] for g in range(ng)], z[:, Z_SM:Z_SM + LANES], k)
              for c, z in enumerate(zs)]
    xcs = _gdn_inverse([f["ac"] for f in fronts], k, nlev)
    gs = [_gdn_back(f, xc) for f, xc in zip(fronts, xcs)]
    rets = [_ret_intra(z, cos_ref[pl.ds(c * C, C), :], sin_ref[pl.ds(c * C, C), :], k) for c, z in enumerate(zs)]
    glas = [_gla_intra(z, k, nlev) for z in zs]

    for c in range(nch):
        rows = pl.ds(c * C, C)

        o, qd, kd, v = rets[c]
        s = sret_ref[0]
        z_ref[rows, Z_RQ:Z_RQ + 256] = o + _mm(qd, s)
        sret_ref[0] = s * k["dch"][...] + _mm(kd, v, TN) * k["bdfull"][...]

        g = gs[c]
        us, qss = [], []
        for hh in range(NH):
            lhs = jnp.concatenate([g["solk"][hh * C:(hh + 1) * C], g["qst"][hh * C:(hh + 1) * C]], axis=0)
            r = _mm(lhs, sgdn_ref[0, hh])
            us.append(g["solv"][hh * C:(hh + 1) * C] - r[:C])
            qss.append(r[C:])
        ost = g["ebcol"] * jnp.concatenate(qss, axis=0) + _mm(g["qk"], jnp.concatenate(us, axis=0))
        for hh in range(NH):
            sl = slice(hh * C, (hh + 1) * C)
            sgdn_ref[0, hh] = (g["eblast"][hh * C:hh * C + 1] * sgdn_ref[0, hh]
                               + _mm(g["kdec"][sl], us[hh], TN))
        z_ref[rows, Z_CONV:Z_CONV + NH * GDN_DV] = jnp.concatenate(
            [ost[hh * C:(hh + 1) * C] for hh in range(NH)], axis=1)

        o, qe, ke, v, b, _ = glas[c]
        s = sgla_ref[0]
        z_ref[rows, Z_LQ:Z_LQ + 256] = o + _mm(qe, s)
        escale = jnp.exp(b[C - 8:, :].T[:, 7:8])
        sgla_ref[0] = s * escale + _mm(ke, v, TN) * k["glabd"][...]

    y_r = _ret_out(z_ref[:, Z_RQ:Z_RQ + 256], z_ref[:, Z_RG:Z_RG + 256], k)
    y_d = _gdn_out(z_ref[:, Z_CONV:Z_CONV + NH * GDN_DV], z_ref[:, Z_DG:Z_DG + 512], k)
    y_l = _gla_out(z_ref[:, Z_LQ:Z_LQ + 256], z_ref[:, Z_LG:Z_LG + 256], k)
    y_ref[...] = jnp.concatenate([y_r, y_d, y_l], axis=1).astype(BF16)

    o_ref[...] = x_ref[...] + jnp.dot(y_ref[...], wout_ref[...].astype(BF16), preferred_element_type=F32)

    @pl.when(t == pl.num_programs(1) - 1)
    def _():
        for g in range(ng):
            conv_ref[0, :, g * LANES:(g + 1) * LANES] = xp_ref[g, pl.ds(5, 3), :]


def _mixer_prompt(x, l, nw, win, wout, cos, sin, params, consts, *, bsz, tlen, tc=MIX_TC):
    nlev = len(_levels(C))
    nt = tlen // tc
    in_specs = [
        pl.BlockSpec((tc, D_MODEL), lambda b, t: (b * nt + t, 0)),
        _layer_resident(nw, l), _layer_resident(win, l), _layer_resident(wout, l),
        pl.BlockSpec((tc, 256), lambda b, t: (t, 0)),
        pl.BlockSpec((tc, 256), lambda b, t: (t, 0)),
    ] + [_layer_resident(params[n], l) for n in _PARAM_NAMES] + [_resident(consts[n]) for n in _CONST_NAMES]
    extras = [params[n] for n in _PARAM_NAMES] + [consts[n] for n in _CONST_NAMES]
    out_shape = (
        jax.ShapeDtypeStruct(x.shape, F32),
        jax.ShapeDtypeStruct((bsz, 256, 256), F32),
        jax.ShapeDtypeStruct((bsz, NH, GDN_DK, GDN_DV), F32),
        jax.ShapeDtypeStruct((bsz, NH * GLA_DK, NH * GLA_DV), F32),
        jax.ShapeDtypeStruct((bsz, CONV_W - 1, CONV_DIM), F32),
    )
    out_specs = (
        pl.BlockSpec((tc, D_MODEL), lambda b, t: (b * nt + t, 0)),
        pl.BlockSpec((1, 256, 256), lambda b, t: (b, 0, 0)),
        pl.BlockSpec((1, NH, GDN_DK, GDN_DV), lambda b, t: (b, 0, 0, 0)),
        pl.BlockSpec((1, NH * GLA_DK, NH * GLA_DV), lambda b, t: (b, 0, 0)),
        pl.BlockSpec((1, CONV_W - 1, CONV_DIM), lambda b, t: (b, 0, 0)),
    )
    return pl.pallas_call(
        functools.partial(_mixer_prompt_kernel, tc=tc, nlev=nlev),
        grid=(bsz, nt),
        in_specs=in_specs,
        out_specs=out_specs,
        out_shape=out_shape,
        input_output_aliases={0: 0},
        scratch_shapes=[
            pltpu.VMEM((tc, NZ), F32),
            pltpu.VMEM((CONV_DIM // LANES, tc + 8, LANES), F32),
            pltpu.VMEM((CONV_DIM // LANES, tc, LANES), F32),
            pltpu.VMEM((tc, D_MODEL), BF16),
        ],
        compiler_params=pltpu.CompilerParams(
            dimension_semantics=("parallel", "arbitrary"), vmem_limit_bytes=VMEM_LIMIT),
        name="mixer_prompt",
    )(x, nw, win, wout, cos, sin, *extras)


SEQ_S = 4
NSEQ = C // SEQ_S


def _inproj_kernel(x_ref, nw_ref, win_ref, z_ref):
    z_ref[...] = _proj(_rms_rows(x_ref[...], nw_ref[...]).astype(BF16), win_ref[...])


def _sample_inproj(x, l, nw, win, *, row0, n):
    return pl.pallas_call(
        _inproj_kernel,
        grid=(1,),
        in_specs=[pl.BlockSpec((n, D_MODEL), lambda i: (row0 // n, 0)), _layer_resident(nw, l),
                  _layer_resident(win, l)],
        out_specs=pl.BlockSpec((n, NZ), lambda i: (0, 0)),
        out_shape=jax.ShapeDtypeStruct((n, NZ), F32),
        compiler_params=pltpu.CompilerParams(
            dimension_semantics=("arbitrary",), vmem_limit_bytes=VMEM_LIMIT),
        name="sample_inproj",
    )(x, nw, win)


def _mixer_sample_kernel(*refs, nlev):
    n_in = 9 + len(_PARAM_NAMES) + len(_CONST_NAMES)
    (x_ref, zin_ref, wout_ref, cos_ref, sin_ref,
     sret_in, sgdn_in, sgla_in, cbuf_ref) = refs[:9]
    k = dict(zip(_PARAM_NAMES + _CONST_NAMES, refs[9:n_in]))
    o_ref, sret_out, sgdn_out, sgla_out, cout_ref = refs[n_in + 3:n_in + 8]

    x = x_ref[...]
    z = zin_ref[...]

    rowi = lax.broadcasted_iota(jnp.int32, (C, 1), 0)
    tpos = rowi % SEQ_S

    xc = z[:, Z_CONV:Z_CONV + CONV_DIM]
    cb = cbuf_ref[...]
    cw = k["conv_w"]
    acc = xc * cw[3:4, :]
    for i in range(CONV_W - 1):
        cur = pltpu.roll(xc, 3 - i, 0)
        old = cb if i == 0 else pltpu.roll(cb, C - i, 0)
        acc = acc + jnp.where(tpos + i >= 3, cur, old) * cw[i:i + 1, :]
    cout_ref[...] = pltpu.roll(xc, C - 1, 0)
    conv = _silu(acc)

    seq_of_row = rowi // SEQ_S
    seq_of_lane = lax.broadcasted_iota(jnp.int32, (1, C), 1) // SEQ_S

    o, qd, kd, v = _ret_intra(z, cos_ref[...], sin_ref[...], k)
    dch = k["dch"][...]
    kdt = kd.T
    accs = [jnp.zeros((C, RET_DV), F32) for _ in range(NH)]
    for n in range(NSEQ):
        rm = seq_of_row == n
        lm = seq_of_lane == n
        for hh in range(NH):
            sl = slice(hh * RET_DK, (hh + 1) * RET_DK)
            s = sret_in[n, hh]
            accs[hh] = jnp.where(rm, _mm(qd[:, sl], s), accs[hh])
            sret_out[n, hh] = s * dch[:, sl] + _mm(jnp.where(lm, kdt[sl], 0.0), v[:, sl])
    o = o + jnp.concatenate(accs, axis=1)
    y_r = _ret_out(o, z[:, Z_RG:Z_RG + 256], k)

    f = _gdn_front([conv[:, gi * LANES:(gi + 1) * LANES] for gi in range(CONV_DIM // LANES)],
                   z[:, Z_SM:Z_SM + LANES], k)
    g = _gdn_back(f, _gdn_inverse([f["ac"]], k, nlev)[0])
    rowi2 = lax.broadcasted_iota(jnp.int32, (2 * C, 1), 0)
    seq_of_row2 = (rowi2 % C) // SEQ_S
    lhs = [jnp.concatenate([g["solk"][hh * C:(hh + 1) * C], g["qst"][hh * C:(hh + 1) * C]], axis=0)
           for hh in range(NH)]
    accs = [jnp.zeros((2 * C, GDN_DV), F32) for _ in range(NH)]
    for n in range(NSEQ):
        rm2 = seq_of_row2 == n
        for hh in range(NH):
            accs[hh] = jnp.where(rm2, _mm(lhs[hh], sgdn_in[n, hh]), accs[hh])
    us = [g["solv"][hh * C:(hh + 1) * C] - accs[hh][:C] for hh in range(NH)]
    ost = (g["ebcol"] * jnp.concatenate([accs[hh][C:] for hh in range(NH)], axis=0)
           + _mm(g["qk"], jnp.concatenate(us, axis=0)))
    kdts = [g["kdec"][hh * C:(hh + 1) * C].T for hh in range(NH)]
    for n in range(NSEQ):
        lm = seq_of_lane == n
        for hh in range(NH):
            scale = g["eblast"][hh * C + n * SEQ_S:hh * C + n * SEQ_S + 1]
            sgdn_out[n, hh] = scale * sgdn_in[n, hh] + _mm(jnp.where(lm, kdts[hh], 0.0), us[hh])
    y_d = _gdn_out(jnp.concatenate([ost[hh * C:(hh + 1) * C] for hh in range(NH)], axis=1),
                   z[:, Z_DG:Z_DG + 512], k)

    o, qe, ke, v, b, blast = _gla_intra(z, k, nlev)
    ket = ke.T
    hi, mid = _split2(blast * (1.0 / SEQ_S))
    blt2 = lax.dot_general(jnp.concatenate([hi, mid], axis=1), k["seqsel"][...], TN, preferred_element_type=F32)
    eblt = jnp.exp(blt2[:NH * GLA_DK] + blt2[NH * GLA_DK:])
    accs = [jnp.zeros((C, GLA_DV), F32) for _ in range(NH)]
    for n in range(NSEQ):
        rm = seq_of_row == n
        lm = seq_of_lane == n
        for hh in range(NH):
            sk = slice(hh * GLA_DK, (hh + 1) * GLA_DK)
            sv = slice(hh * GLA_DV, (hh + 1) * GLA_DV)
            s = sgla_in[n, hh]
            accs[hh] = jnp.where(rm, _mm(qe[:, sk], s), accs[hh])
            sgla_out[n, hh] = s * eblt[sk, n:n + 1] + _mm(jnp.where(lm, ket[sk], 0.0), v[:, sv])
    o = o + jnp.concatenate(accs, axis=1)
    y_l = _gla_out(o, z[:, Z_LG:Z_LG + 256], k)

    y = jnp.concatenate([y_r, y_d, y_l], axis=1).astype(BF16)
    o_ref[...] = x + jnp.dot(y, wout_ref[...].astype(BF16), preferred_element_type=F32)


def _mixer_sample(x, l, nw, win, wout, cos, sin, sret, sgdn, sgla, cbuf, acc_ret, acc_gdn, acc_gla,
                  params, consts, *, row0):
    nb = sret.shape[1]
    n = nb * SEQ_S
    nlev = len(_levels(SEQ_S))
    blk0 = row0 // C

    def st_spec(dk, dv):
        return pl.BlockSpec((None, NSEQ, NH, dk, dv), lambda i: (l, i, 0, 0, 0))

    st_specs = [st_spec(RET_DK, RET_DV), st_spec(GDN_DK, GDN_DV), st_spec(GLA_DK, GLA_DV)]
    any_spec = pl.BlockSpec(memory_space=pl.ANY)
    z = _sample_inproj(x, l, nw, win, row0=row0, n=n)
    in_specs = [
        pl.BlockSpec((C, D_MODEL), lambda i: (blk0 + i, 0)),
        pl.BlockSpec((C, NZ), lambda i: (i, 0)),
        _layer_resident(wout, l), _resident(cos), _resident(sin),
    ] + st_specs + [pl.BlockSpec((C, CONV_DIM), lambda i: (i, 0))] + [
        _layer_resident(params[nm], l) for nm in _PARAM_NAMES] + [
        _resident(consts[nm]) for nm in _CONST_NAMES] + [any_spec] * 3
    extras = [params[nm] for nm in _PARAM_NAMES] + [consts[nm] for nm in _CONST_NAMES]
    n_in = len(in_specs)
    out_shape = (
        jax.ShapeDtypeStruct(x.shape, F32),
        jax.ShapeDtypeStruct(acc_ret.shape, F32),
        jax.ShapeDtypeStruct(acc_gdn.shape, F32),
        jax.ShapeDtypeStruct(acc_gla.shape, F32),
        jax.ShapeDtypeStruct((n, CONV_DIM), F32),
    )
    out_specs = (pl.BlockSpec((C, D_MODEL), lambda i: (blk0 + i, 0)),) + tuple(st_specs) + (
        pl.BlockSpec((C, CONV_DIM), lambda i: (i, 0)),)
    return pl.pallas_call(
        functools.partial(_mixer_sample_kernel, nlev=nlev),
        grid=(n // C,),
        in_specs=in_specs,
        out_specs=out_specs,
        out_shape=out_shape,
        input_output_aliases={0: 0, n_in - 3: 1, n_in - 2: 2, n_in - 1: 3},
        compiler_params=pltpu.CompilerParams(
            dimension_semantics=("parallel",), vmem_limit_bytes=VMEM_LIMIT),
        name="mixer_sample",
    )(x, z, wout, cos, sin, sret, sgdn, sgla, cbuf, *extras, acc_ret, acc_gdn, acc_gla)


def _permute_w_in(w):
    wt = jnp.swapaxes(w, 1, 2)
    pad = jnp.zeros((w.shape[0], NZ - Z_SM - 24, w.shape[1]), w.dtype)
    return jnp.concatenate(
        [wt[:, 0:2560], wt[:, 2568:3080], wt[:, 3080:3592], wt[:, 3608:3864], wt[:, 2560:2568],
         wt[:, 3592:3608], pad], axis=1).astype(BF16)


def _rows(v, width=None):
    v = v.astype(F32)[:, None, :]
    if width is not None and v.shape[-1] < width:
        v = jnp.pad(v, ((0, 0), (0, 0), (0, width - v.shape[-1])))
    return v


def kernel(x_prompt, x_sample, state_ret, state_gdn, state_gdn_conv, state_gla, norm_ffn1, ffn1_w1, ffn1_w3, ffn1_w2, norm_mix, w_in, ret_gn_w, ret_gn_b, gdn_conv_w, gdn_A_log, gdn_dt_bias, gdn_norm_w, gla_gate_w, gla_gate_b, gla_norm_w, w_out, norm_ffn2, ffn2_w1, ffn2_w3, ffn2_w2, norm_final):
    bp, tp, _ = x_prompt.shape
    bs, ts, _ = x_sample.shape
    n_prompt, n_sample = bp * tp, bs * ts
    assert ts == SEQ_S and tp % MIX_TC == 0 and n_sample % C == 0 and n_prompt % C == 0

    consts_p = dict(_chunk_consts(C), **_ret_consts(C))
    consts_s = dict(_chunk_consts(SEQ_S), **_ret_consts(SEQ_S))
    cos_p, sin_p = _rope_tables(jnp.arange(tp, dtype=jnp.int32))
    cos_s, sin_s = _rope_tables(PAST_LEN + (jnp.arange(C, dtype=jnp.int32) % SEQ_S))

    params = dict(
        alog=_rows(gdn_A_log, LANES), dtb=_rows(gdn_dt_bias, LANES),
        gwp=jnp.pad(gla_gate_w, ((0, 0), (SM_LLR, LANES - SM_LLR - GLA_RANK), (0, 0))).astype(BF16),
        gla_gate_b=_rows(gla_gate_b), ret_gn_w=_rows(ret_gn_w), ret_gn_b=_rows(ret_gn_b),
        gdn_norm_w=_rows(gdn_norm_w), gla_norm_w=_rows(jnp.tile(gla_norm_w, (1, NH))),
        conv_w=gdn_conv_w.astype(F32),
    )
    win = _permute_w_in(w_in)
    wout = w_out.astype(F32)
    f1 = (_rows(norm_ffn1), ffn1_w1.astype(F32), ffn1_w3.astype(F32), ffn1_w2.astype(F32))
    f2 = (_rows(norm_ffn2), ffn2_w1.astype(F32), ffn2_w3.astype(F32), ffn2_w2.astype(F32))
    nw = _rows(norm_mix)
    fw = norm_final.astype(F32)[None, :]
    sret_in, sgdn_in, sgla_in = state_ret.astype(F32), state_gdn.astype(F32), state_gla.astype(F32)
    cbuf = jnp.pad(state_gdn_conv.astype(F32), ((0, 0), (0, 0), (0, 1), (0, 0))).reshape(DEPTH, n_sample, CONV_DIM)
    acc_ret, acc_gdn, acc_gla = (lax.empty(s.shape, F32) for s in (sret_in, sgdn_in, sgla_in))

    ffn = functools.partial(_ffn, n_prompt=n_prompt, n_sample=n_sample)
    xs = (x_prompt.reshape(n_prompt, D_MODEL), x_sample.reshape(n_sample, D_MODEL))
    outs_p = [[] for _ in range(4)]
    conv_s = []
    for l in range(DEPTH):
        x = ffn(xs, l, *f1, fw, first=(l == 0), last=False)
        x, sret, sgdn, sgla, conv = _mixer_prompt(x, l, nw, win, wout, cos_p, sin_p, params, consts_p,
                                                  bsz=bp, tlen=tp)
        outs_p[0].append(jnp.stack([sret[:, h * 64:(h + 1) * 64, h * 64:(h + 1) * 64] for h in range(NH)], axis=1))
        outs_p[1].append(sgdn)
        outs_p[2].append(conv)
        outs_p[3].append(jnp.stack([sgla[:, h * 32:(h + 1) * 32, h * 64:(h + 1) * 64] for h in range(NH)], axis=1))
        x, acc_ret, acc_gdn, acc_gla, cout = _mixer_sample(
            x, l, nw, win, wout, cos_s, sin_s, sret_in, sgdn_in, sgla_in, cbuf[l], acc_ret, acc_gdn, acc_gla,
            params, consts_s, row0=n_prompt)
        conv_s.append(cout.reshape(bs, ts, CONV_DIM)[:, :CONV_W - 1])
        xs = (ffn((x,), l, *f2, fw, first=False, last=(l == DEPTH - 1)),)

    y_prompt, y_sample = xs[0]
    dts = (state_ret.dtype, state_gdn.dtype, state_gdn_conv.dtype, state_gla.dtype)
    sp = [jnp.stack(o).astype(d) for o, d in zip(outs_p, dts)]
    return (y_prompt.reshape(bp, tp, D_MODEL), y_sample.reshape(bs, ts, D_MODEL), sp[0], sp[1], sp[2], sp[3],
            acc_ret.astype(dts[0]), acc_gdn.astype(dts[1]), jnp.stack(conv_s).astype(dts[2]), acc_gla.astype(dts[3]))
```

```python
import functools

import numpy as np
import jax
import jax.numpy as jnp
from jax import lax
from jax.experimental import pallas as pl
from jax.experimental.pallas import tpu as pltpu

F32 = jnp.float32
BF16 = jnp.bfloat16

D_MODEL = 1024
DEPTH = 4
PAST_LEN = 16384
NH = 4
RET_DK = 64
RET_DV = 64
GDN_DK = 128
GDN_DV = 128
GLA_DK = 32
GLA_DV = 64
GLA_RANK = 16
GLA_GATE_NORM = 16.0
CONV_W = 4
CONV_DIM = NH * (2 * GDN_DK + GDN_DV)
D_FF = 2816
ROPE_BASE = 10000.0
EPS = 1e-6
GN_EPS = 1e-5

RET_KSCALE = RET_DK ** -0.5

C = 64
R = NH * C
LANES = 128

Z_RQ, Z_RK, Z_RV, Z_RG = 0, 256, 512, 768
Z_CONV = 1024
Z_DG = 2560
Z_LQ, Z_LK, Z_LV, Z_LG = 3072, 3200, 3328, 3584
Z_SM = 3840
NZ = 3968
SM_DA, SM_DB, SM_LLR = 0, 4, 8

NN = (((1,), (0,)), ((), ()))
NT = (((1,), (1,)), ((), ()))
TN = (((0,), (0,)), ((), ()))

VMEM_LIMIT = 56 * 1024 * 1024
FFN_TM = 512
FFN_TF = 256
MIX_TC = 512


def _mm(a, b, dims=NN):
    return lax.dot_general(a.astype(BF16), b.astype(BF16), dims, preferred_element_type=F32)


def _proj(h, w_rows):
    return lax.dot_general(h, w_rows, NT, preferred_element_type=F32)


def _split2(x):
    hi = x.astype(BF16)
    return hi, (x - hi.astype(F32)).astype(BF16)


def _mm_data_const(x, cb):
    hi, mid = _split2(x)
    m = x.shape[0]
    r = lax.dot_general(jnp.concatenate([hi, mid], axis=0), cb, NN, preferred_element_type=F32)
    return r[:m] + r[m:]


def _mm_const_data(cb, x):
    hi, mid = _split2(x)
    n = x.shape[1]
    r = lax.dot_general(cb, jnp.concatenate([hi, mid], axis=1), NN, preferred_element_type=F32)
    return r[:, :n] + r[:, n:]


def _silu(x):
    hx = 0.5 * x
    return hx + hx * jnp.tanh(hx)


def _softplus(x):
    return jnp.maximum(x, 0.0) + jnp.log1p(jnp.exp(-jnp.abs(x)))


def _log_sigmoid(x):
    return jnp.minimum(x, 0.0) - jnp.log1p(jnp.exp(-jnp.abs(x)))


def _rms_rows(x, w):
    ms = jnp.mean(x * x, axis=-1, keepdims=True)
    return x * lax.rsqrt(ms + EPS) * w


def _stack_bd(x, width):
    lane = lax.broadcasted_iota(jnp.int32, x.shape, 1) // width
    zero = jnp.zeros_like(x)
    return jnp.concatenate([jnp.where(lane == h, x, zero) for h in range(NH)], axis=0)


def _col_to_compact(col):
    lane = lax.broadcasted_iota(jnp.int32, (C, R), 1) // C
    out = jnp.zeros((C, R), F32)
    for h in range(NH):
        out = jnp.where(lane == h, col[h * C:(h + 1) * C], out)
    return out


def _ffn_kernel(*refs, first, last, npb, tf):
    n_x = 2 if first else 1
    xrefs, (nw_ref, w1_ref, w3_ref, w2_ref, fw_ref) = refs[:n_x], refs[n_x:n_x + 5]
    n_o = 2 if last else 1
    orefs = refs[n_x + 5:n_x + 5 + n_o]
    h_ref, acc_ref = refs[n_x + 5 + n_o:]
    i = pl.program_id(0)
    x = jnp.where(i < npb, xrefs[0][...], xrefs[1][...]) if first else xrefs[0][...]
    h_ref[...] = _rms_rows(x, nw_ref[...]).astype(BF16)
    for j in range(D_FF // tf):
        h = h_ref[...]
        a = jnp.dot(h, w1_ref[:, j * tf:(j + 1) * tf].astype(BF16), preferred_element_type=F32)
        g = jnp.dot(h, w3_ref[:, j * tf:(j + 1) * tf].astype(BF16), preferred_element_type=F32)
        p = jnp.dot((_silu(a) * g).astype(BF16), w2_ref[j * tf:(j + 1) * tf, :].astype(BF16),
                    preferred_element_type=F32)
        if j == 0:
            acc_ref[...] = p
        else:
            acc_ref[...] += p
    y = x + 0.5 * acc_ref[...]
    if last:
        y = _rms_rows(y, fw_ref[...])

        @pl.when(i < npb)
        def _():
            orefs[0][...] = y

        @pl.when(i >= npb)
        def _():
            orefs[1][...] = y
    else:
        orefs[0][...] = y


def _resident(a):
    nd = a.ndim
    return pl.BlockSpec(a.shape, lambda *_, _n=nd: (0,) * _n, pipeline_mode=pl.Buffered(1))


def _layer_resident(a, l):
    nd = a.ndim - 1
    return pl.BlockSpec((None,) + a.shape[1:], lambda *_, _n=nd: (l,) + (0,) * _n, pipeline_mode=pl.Buffered(1))


def _ffn(xs, l, nw, w1, w3, w2, fw, *, first, last, n_prompt, n_sample, tm=FFN_TM, tf=FFN_TF):
    assert n_sample == tm and n_prompt % tm == 0
    npb = n_prompt // tm
    prompt_blk = pl.BlockSpec((tm, D_MODEL), lambda i: (jnp.minimum(i, npb - 1), 0))
    sample_blk = pl.BlockSpec((tm, D_MODEL), lambda i: (0, 0))
    unified_blk = pl.BlockSpec((tm, D_MODEL), lambda i: (i, 0))
    if last:
        out_specs = (prompt_blk, sample_blk)
        out_shape = (jax.ShapeDtypeStruct((n_prompt, D_MODEL), F32), jax.ShapeDtypeStruct((n_sample, D_MODEL), F32))
    else:
        out_specs = unified_blk
        out_shape = jax.ShapeDtypeStruct((n_prompt + n_sample, D_MODEL), F32)
    return pl.pallas_call(
        functools.partial(_ffn_kernel, first=first, last=last, npb=npb, tf=tf),
        grid=(npb + 1,),
        in_specs=([prompt_blk, sample_blk] if first else [unified_blk]) + [
            _layer_resident(nw, l), _layer_resident(w1, l), _layer_resident(w3, l), _layer_resident(w2, l),
            _resident(fw)],
        out_specs=out_specs,
        out_shape=out_shape,
        scratch_shapes=[pltpu.VMEM((tm, D_MODEL), BF16), pltpu.VMEM((tm, D_MODEL), F32)],
        compiler_params=pltpu.CompilerParams(
            dimension_semantics=("arbitrary",), vmem_limit_bytes=VMEM_LIMIT),
        name="ffn",
    )(*xs, nw, w1, w3, w2, fw)


def _levels(L):
    return [s for s in (1, 2, 4, 8, 16, 32) if s < L]


def _chunk_consts(L):
    i = np.arange(C)
    sid, p = i // L, i % L
    same = sid[:, None] == sid[None, :]
    causal = same & (i[:, None] >= i[None, :])
    strict = same & (i[:, None] > i[None, :])
    lv = _levels(L)
    cum, pm, lm = [causal, same], [np.eye(C, dtype=bool)], []
    for s in lv:
        blk = p // s
        inblk = same & (blk[:, None] == blk[None, :])
        odd = (blk % 2 == 1)[:, None]
        cum.append(inblk & np.where(odd, i[None, :] <= i[:, None], i[None, :] > i[:, None]))
        pm.append(same & odd & (blk[None, :] == blk[:, None] - 1))
        lm.append(strict & ((p[:, None] // (2 * s)) == (p[None, :] // (2 * s))) & (blk[:, None] != blk[None, :]))
    r = np.arange(R)
    bdfull = (r[:, None] // C) == (r[None, :] // C)
    wide = lambda m: np.tile(m, (1, NH))
    f = lambda m: jnp.asarray(np.asarray(m, dtype=np.float32))
    gla_bd = (np.arange(NH * GLA_DK)[:, None] // GLA_DK) == (np.arange(NH * GLA_DV)[None, :] // GLA_DV)
    seqsel = np.zeros((C, LANES), np.float32)
    seqsel[i, sid] = 1.0
    return dict(
        cumg=f(np.concatenate(cum, axis=0)).astype(BF16),
        cumd=f(np.concatenate(cum[:2], axis=0)).astype(BF16),
        pmc=f(np.stack([wide(m) for m in pm])),
        lmc=f(np.stack([wide(m) for m in lm])),
        causal4=f(wide(causal)), eye4=f(wide(np.eye(C, dtype=bool))),
        g64=(f(bdfull) * (1.0 / 64.0)).astype(BF16), bdfull=f(bdfull), glabd=f(gla_bd),
        seqsel=jnp.asarray(seqsel).astype(BF16),
    )


def _ret_consts(L):
    log_gamma = jnp.log(1.0 - 2.0 ** (-5.0 - jnp.arange(NH, dtype=F32)))
    i = np.arange(C)
    sid, p = i // L, (i % L).astype(np.float32)
    causal = (sid[:, None] == sid[None, :]) & (i[:, None] >= i[None, :])
    diff = jnp.asarray(np.where(causal, p[:, None] - p[None, :], 0.0).astype(np.float32))
    dec = jnp.where(causal[None], jnp.exp(diff[None] * log_gamma[:, None, None]), 0.0)
    lg_l = jnp.repeat(log_gamma, RET_DV)[None, :]
    pj = jnp.asarray(p)[:, None]
    return dict(
        decc=jnp.concatenate([dec[h] for h in range(NH)], axis=1) * RET_KSCALE,
        dq=jnp.exp((pj + 1.0) * lg_l),
        dk=jnp.exp((L - 1.0 - pj) * lg_l) * RET_KSCALE,
        dch=jnp.exp(L * lg_l),
    )


def _rope_tables(pos):
    half = RET_DK // 2
    inv = ROPE_BASE ** (-jnp.arange(half, dtype=F32) / half)
    ang = pos.astype(F32)[:, None] * inv[None, :]
    cos, sin = jnp.cos(ang), jnp.sin(ang)
    return (jnp.tile(jnp.concatenate([cos, cos], axis=1), (1, NH)),
            jnp.tile(jnp.concatenate([-sin, sin], axis=1), (1, NH)))


def _rotary(x, cos, sin):
    lane = lax.broadcasted_iota(jnp.int32, x.shape, 1)
    swapped = jnp.where((lane % RET_DK) < RET_DK // 2,
                        pltpu.roll(x, x.shape[1] - RET_DK // 2, 1), pltpu.roll(x, RET_DK // 2, 1))
    return x * cos + swapped * sin


def _ret_intra(z, cos, sin, k):
    q = _rotary(z[:, Z_RQ:Z_RQ + 256], cos, sin)
    kk = _rotary(z[:, Z_RK:Z_RK + 256], cos, sin)
    v = z[:, Z_RV:Z_RV + 256]
    sc = _mm(q, _stack_bd(kk.astype(BF16), RET_DK), NT) * k["decc"][...]
    o = _mm(sc, _stack_bd(v.astype(BF16), RET_DV))
    return o, q * k["dq"][...], kk * k["dk"][...], v


def _ret_out(o, rg, k):
    g = k["g64"][...]
    d = o - _mm_data_const(o, g)
    var = _mm(d * d, g)
    return _silu(rg) * (d * lax.rsqrt(var + GN_EPS) * k["ret_gn_w"][...] + k["ret_gn_b"][...])


def _gla_intra(z, k, nlev):
    sm = z[:, Z_SM:Z_SM + LANES]
    gk = _log_sigmoid(_mm(sm, k["gwp"][...]) + k["gla_gate_b"][...]) * (1.0 / GLA_GATE_NORM)
    q = z[:, Z_LQ:Z_LQ + 128] * (GLA_DK ** -0.5)
    kk = z[:, Z_LK:Z_LK + 128]
    v = z[:, Z_LV:Z_LV + 256]
    cs = _mm_const_data(k["cumg"][...], gk)
    b, blast = cs[:C], cs[C:2 * C]
    sc = _mm(q, _stack_bd(kk.astype(BF16), GLA_DK), NT) * k["pmc"][0]
    for li in range(nlev):
        e = jnp.exp(cs[(2 + li) * C:(3 + li) * C])
        sc = sc + _mm(q * e, _stack_bd((kk * e).astype(BF16), GLA_DK), NT) * k["pmc"][li + 1]
    o = _mm(sc, _stack_bd(v.astype(BF16), GLA_DV))
    return o, q * jnp.exp(b), kk * jnp.exp(blast - b), v, b, blast


def _gla_out(o, lg, k):
    ms = _mm(o * o, k["g64"][...])
    return o * lax.rsqrt(ms + EPS) * k["gla_norm_w"][...] * _silu(lg)


def _lane_col(blk, lane0):
    lane = lax.broadcasted_iota(jnp.int32, blk.shape, 1)
    st = jnp.concatenate([jnp.where(lane == lane0 + h, blk, 0.0) for h in range(NH)], axis=0)
    return jnp.sum(st, axis=-1, keepdims=True)


def _gdn_front(conv, sm, k):
    def l2n(x, scale=1.0):
        return x * (lax.rsqrt(jnp.sum(x * x, axis=-1, keepdims=True) + EPS) * scale)

    qs = [l2n(conv[h], GDN_DK ** -0.5) for h in range(NH)]
    ks = [l2n(conv[NH + h]) for h in range(NH)]
    vst = jnp.concatenate(conv[2 * NH:], axis=0)
    qst = jnp.concatenate(qs, axis=0)
    kst = jnp.concatenate(ks, axis=0)
    kq = []
    for p in range(NH // 2):
        kp = jnp.concatenate(ks[2 * p:2 * p + 2], axis=1).astype(BF16)
        lane = lax.broadcasted_iota(jnp.int32, kp.shape, 1) // GDN_DK
        kbd = jnp.concatenate([jnp.where(lane == hl, kp, jnp.zeros_like(kp)) for hl in range(2)], axis=0)
        lhs = jnp.concatenate([kp, jnp.concatenate(qs[2 * p:2 * p + 2], axis=1).astype(BF16)], axis=0)
        kq.append(lax.dot_general(lhs, kbd, NT, preferred_element_type=F32))
    kq = jnp.concatenate(kq, axis=1)
    gd = -jnp.exp(k["alog"][...]) * _softplus(sm + k["dtb"][...])
    beta = jax.nn.sigmoid(sm)
    cs = _mm_const_data(k["cumd"][...], gd)
    bcol = _lane_col(cs[:C], SM_DA)
    blcol = _lane_col(cs[C:], SM_DA)
    betacol = _lane_col(beta, SM_DB)
    bcolc = _col_to_compact(bcol)
    browc = jnp.sum(bcolc * k["eye4"][...], axis=0, keepdims=True)
    causal = k["causal4"][...] > 0.5
    decc = jnp.where(causal, jnp.exp(jnp.where(causal, bcolc - browc, 0.0)), 0.0)
    ac = kq[:C] * decc * _col_to_compact(betacol)
    qkc = kq[C:] * decc
    return dict(ac=ac, qkc=qkc, qst=qst, kst=kst, vst=vst, bcol=bcol, blcol=blcol, betacol=betacol)


def _gdn_inverse(acs, k, nlev):
    xs = [k["eye4"][...] - a * k["lmc"][0] for a in acs]
    for li in range(1, nlev):
        ms = [a * k["lmc"][li] for a in acs]
        ys = [_mm(m, _stack_bd(x.astype(BF16), C)) for m, x in zip(ms, xs)]
        xs = [x - _mm(x, _stack_bd(y.astype(BF16), C)) for x, y in zip(xs, ys)]
    return xs


def _gdn_back(f, xc):
    kb = f["kst"] * f["betacol"]
    rhs = jnp.concatenate([f["vst"] * f["betacol"], kb * jnp.exp(f["bcol"])], axis=1)
    sol = _mm(_stack_bd(xc.astype(BF16), C), rhs)
    return dict(qst=f["qst"], solv=sol[:, :GDN_DV], solk=sol[:, GDN_DV:],
                qk=_stack_bd(f["qkc"].astype(BF16), C), ebcol=jnp.exp(f["bcol"]),
                kdec=f["kst"] * jnp.exp(f["blcol"] - f["bcol"]), eblast=jnp.exp(f["blcol"]))


def _gdn_out(o, dg, k):
    w = k["gdn_norm_w"][...]
    y = jnp.concatenate([_rms_rows(o[:, h * GDN_DV:(h + 1) * GDN_DV], w) for h in range(NH)], axis=1)
    return y * _silu(dg)


_PARAM_NAMES = ("alog", "dtb", "gwp", "gla_gate_b", "ret_gn_w", "ret_gn_b", "gdn_norm_w", "gla_norm_w", "conv_w")
_CONST_NAMES = ("cumg", "cumd", "pmc", "lmc", "causal4", "eye4", "g64", "bdfull", "glabd", "seqsel",
                "decc", "dq", "dk", "dch")


def _mixer_prompt_kernel(*refs, tc, nlev):
    n_in = 6 + len(_PARAM_NAMES) + len(_CONST_NAMES)
    x_ref, nw_ref, win_ref, wout_ref, cos_ref, sin_ref = refs[:6]
    k = dict(zip(_PARAM_NAMES + _CONST_NAMES, refs[6:n_in]))
    o_ref, sret_ref, sgdn_ref, sgla_ref, conv_ref = refs[n_in:n_in + 5]
    z_ref, xp_ref, cv_ref, y_ref = refs[n_in + 5:]
    t = pl.program_id(1)
    nch = tc // C
    ng = CONV_DIM // LANES

    @pl.when(t == 0)
    def _():
        sret_ref[...] = jnp.zeros_like(sret_ref)
        sgdn_ref[...] = jnp.zeros_like(sgdn_ref)
        sgla_ref[...] = jnp.zeros_like(sgla_ref)
        xp_ref[:, pl.ds(0, 8), :] = jnp.zeros((ng, 8, LANES), F32)

    h = _rms_rows(x_ref[...], nw_ref[...]).astype(BF16)
    zc = _proj(h, win_ref[Z_CONV:Z_CONV + CONV_DIM, :])
    for g in range(ng):
        xp_ref[g, pl.ds(8, tc), :] = zc[:, g * LANES:(g + 1) * LANES]
    z_ref[:, :Z_CONV] = _proj(h, win_ref[:Z_CONV, :])
    z_ref[:, Z_DG:] = _proj(h, win_ref[Z_DG:, :])

    cw = k["conv_w"]
    nb = tc // 8
    for g in range(ng):
        wg = [cw[i:i + 1, g * LANES:(g + 1) * LANES] for i in range(CONV_W)]
        taps = {s: xp_ref[g, pl.ds(s, nb, stride=8), :] for s in range(5, 5 + 8 + CONV_W - 1)}
        for j in range(8):
            acc = taps[j + 8] * wg[3]
            for i in range(CONV_W - 1):
                acc = acc + taps[j + 5 + i] * wg[i]
            cv_ref[g, pl.ds(j, nb, stride=8), :] = _silu(acc)
        xp_ref[g, pl.ds(0, 8), :] = xp_ref[g, pl.ds(tc, 8), :]

    zs = [z_ref[pl.ds(c * C, C), :] for c in range(nch)]
    fronts = [_gdn_front([cv_ref[g, pl.ds(c * C, C), :] for g in range(ng)], z[:, Z_SM:Z_SM + LANES], k)
              for c, z in enumerate(zs)]
    xcs = _gdn_inverse([f["ac"] for f in fronts], k, nlev)
    gs = [_gdn_back(f, xc) for f, xc in zip(fronts, xcs)]
    rets = [_ret_intra(z, cos_ref[pl.ds(c * C, C), :], sin_ref[pl.ds(c * C, C), :], k) for c, z in enumerate(zs)]
    glas = [_gla_intra(z, k, nlev) for z in zs]

    for c in range(nch):
        rows = pl.ds(c * C, C)

        o, qd, kd, v = rets[c]
        s = sret_ref[0]
        z_ref[rows, Z_RQ:Z_RQ + 256] = o + _mm(qd, s)
        sret_ref[0] = s * k["dch"][...] + _mm(kd, v, TN) * k["bdfull"][...]

        g = gs[c]
        us, qss = [], []
        for hh in range(NH):
            lhs = jnp.concatenate([g["solk"][hh * C:(hh + 1) * C], g["qst"][hh * C:(hh + 1) * C]], axis=0)
            r = _mm(lhs, sgdn_ref[0, hh])
            us.append(g["solv"][hh * C:(hh + 1) * C] - r[:C])
            qss.append(r[C:])
        ost = g["ebcol"] * jnp.concatenate(qss, axis=0) + _mm(g["qk"], jnp.concatenate(us, axis=0))
        for hh in range(NH):
            sl = slice(hh * C, (hh + 1) * C)
            sgdn_ref[0, hh] = (g["eblast"][hh * C:hh * C + 1] * sgdn_ref[0, hh]
                               + _mm(g["kdec"][sl], us[hh], TN))
        z_ref[rows, Z_CONV:Z_CONV + NH * GDN_DV] = jnp.concatenate(
            [ost[hh * C:(hh + 1) * C] for hh in range(NH)], axis=1)

        o, qe, ke, v, b, _ = glas[c]
        s = sgla_ref[0]
        z_ref[rows, Z_LQ:Z_LQ + 256] = o + _mm(qe, s)
        escale = jnp.exp(b[C - 8:, :].T[:, 7:8])
        sgla_ref[0] = s * escale + _mm(ke, v, TN) * k["glabd"][...]

    y_r = _ret_out(z_ref[:, Z_RQ:Z_RQ + 256], z_ref[:, Z_RG:Z_RG + 256], k)
    y_d = _gdn_out(z_ref[:, Z_CONV:Z_CONV + NH * GDN_DV], z_ref[:, Z_DG:Z_DG + 512], k)
    y_l = _gla_out(z_ref[:, Z_LQ:Z_LQ + 256], z_ref[:, Z_LG:Z_LG + 256], k)
    y_ref[...] = jnp.concatenate([y_r, y_d, y_l], axis=1).astype(BF16)

    o_ref[...] = x_ref[...] + jnp.dot(y_ref[...], wout_ref[...].astype(BF16), preferred_element_type=F32)

    @pl.when(t == pl.num_programs(1) - 1)
    def _():
        for g in range(ng):
            conv_ref[0, :, g * LANES:(g + 1) * LANES] = xp_ref[g, pl.ds(5, 3), :]


def _mixer_prompt(x, l, nw, win, wout, cos, sin, params, consts, *, bsz, tlen, tc=MIX_TC):
    nlev = len(_levels(C))
    nt = tlen // tc
    in_specs = [
        pl.BlockSpec((tc, D_MODEL), lambda b, t: (b * nt + t, 0)),
        _layer_resident(nw, l), _layer_resident(win, l), _layer_resident(wout, l),
        pl.BlockSpec((tc, 256), lambda b, t: (t, 0)),
        pl.BlockSpec((tc, 256), lambda b, t: (t, 0)),
    ] + [_layer_resident(params[n], l) for n in _PARAM_NAMES] + [_resident(consts[n]) for n in _CONST_NAMES]
    extras = [params[n] for n in _PARAM_NAMES] + [consts[n] for n in _CONST_NAMES]
    out_shape = (
        jax.ShapeDtypeStruct(x.shape, F32),
        jax.ShapeDtypeStruct((bsz, 256, 256), F32),
        jax.ShapeDtypeStruct((bsz, NH, GDN_DK, GDN_DV), F32),
        jax.ShapeDtypeStruct((bsz, NH * GLA_DK, NH * GLA_DV), F32),
        jax.ShapeDtypeStruct((bsz, CONV_W - 1, CONV_DIM), F32),
    )
    out_specs = (
        pl.BlockSpec((tc, D_MODEL), lambda b, t: (b * nt + t, 0)),
        pl.BlockSpec((1, 256, 256), lambda b, t: (b, 0, 0)),
        pl.BlockSpec((1, NH, GDN_DK, GDN_DV), lambda b, t: (b, 0, 0, 0)),
        pl.BlockSpec((1, NH * GLA_DK, NH * GLA_DV), lambda b, t: (b, 0, 0)),
        pl.BlockSpec((1, CONV_W - 1, CONV_DIM), lambda b, t: (b, 0, 0)),
    )
    return pl.pallas_call(
        functools.partial(_mixer_prompt_kernel, tc=tc, nlev=nlev),
        grid=(bsz, nt),
        in_specs=in_specs,
        out_specs=out_specs,
        out_shape=out_shape,
        input_output_aliases={0: 0},
        scratch_shapes=[
            pltpu.VMEM((tc, NZ), F32),
            pltpu.VMEM((CONV_DIM // LANES, tc + 8, LANES), F32),
            pltpu.VMEM((CONV_DIM // LANES, tc, LANES), F32),
            pltpu.VMEM((tc, D_MODEL), BF16),
        ],
        compiler_params=pltpu.CompilerParams(
            dimension_semantics=("parallel", "arbitrary"), vmem_limit_bytes=VMEM_LIMIT),
        name="mixer_prompt",
    )(x, nw, win, wout, cos, sin, *extras)


SEQ_S = 4
NSEQ = C // SEQ_S


def _inproj_kernel(x_ref, nw_ref, win_ref, z_ref):
    z_ref[...] = _proj(_rms_rows(x_ref[...], nw_ref[...]).astype(BF16), win_ref[...])


def _sample_inproj(x, l, nw, win, *, row0, n):
    return pl.pallas_call(
        _inproj_kernel,
        grid=(1,),
        in_specs=[pl.BlockSpec((n, D_MODEL), lambda i: (row0 // n, 0)), _layer_resident(nw, l),
                  _layer_resident(win, l)],
        out_specs=pl.BlockSpec((n, NZ), lambda i: (0, 0)),
        out_shape=jax.ShapeDtypeStruct((n, NZ), F32),
        compiler_params=pltpu.CompilerParams(
            dimension_semantics=("arbitrary",), vmem_limit_bytes=VMEM_LIMIT),
        name="sample_inproj",
    )(x, nw, win)


def _mixer_sample_kernel(*refs, nlev):
    n_in = 9 + len(_PARAM_NAMES) + len(_CONST_NAMES)
    (x_ref, zin_ref, wout_ref, cos_ref, sin_ref,
     sret_in, sgdn_in, sgla_in, cbuf_ref) = refs[:9]
    k = dict(zip(_PARAM_NAMES + _CONST_NAMES, refs[9:n_in]))
    o_ref, sret_out, sgdn_out, sgla_out, cout_ref = refs[n_in + 3:n_in + 8]

    x = x_ref[...]
    z = zin_ref[...]

    rowi = lax.broadcasted_iota(jnp.int32, (C, 1), 0)
    tpos = rowi % SEQ_S

    xc = z[:, Z_CONV:Z_CONV + CONV_DIM]
    cb = cbuf_ref[...]
    cw = k["conv_w"]
    acc = xc * cw[3:4, :]
    for i in range(CONV_W - 1):
        cur = pltpu.roll(xc, 3 - i, 0)
        old = cb if i == 0 else pltpu.roll(cb, C - i, 0)
        acc = acc + jnp.where(tpos + i >= 3, cur, old) * cw[i:i + 1, :]
    cout_ref[...] = pltpu.roll(xc, C - 1, 0)
    conv = _silu(acc)

    seq_of_row = rowi // SEQ_S
    seq_of_lane = lax.broadcasted_iota(jnp.int32, (1, C), 1) // SEQ_S

    o, qd, kd, v = _ret_intra(z, cos_ref[...], sin_ref[...], k)
    dch = k["dch"][...]
    kdt = kd.T
    accs = [jnp.zeros((C, RET_DV), F32) for _ in range(NH)]
    for n in range(NSEQ):
        rm = seq_of_row == n
        lm = seq_of_lane == n
        for hh in range(NH):
            sl = slice(hh * RET_DK, (hh + 1) * RET_DK)
            s = sret_in[n, hh]
            accs[hh] = jnp.where(rm, _mm(qd[:, sl], s), accs[hh])
            sret_out[n, hh] = s * dch[:, sl] + _mm(jnp.where(lm, kdt[sl], 0.0), v[:, sl])
    o = o + jnp.concatenate(accs, axis=1)
    y_r = _ret_out(o, z[:, Z_RG:Z_RG + 256], k)

    f = _gdn_front([conv[:, gi * LANES:(gi + 1) * LANES] for gi in range(CONV_DIM // LANES)],
                   z[:, Z_SM:Z_SM + LANES], k)
    g = _gdn_back(f, _gdn_inverse([f["ac"]], k, nlev)[0])
    rowi2 = lax.broadcasted_iota(jnp.int32, (2 * C, 1), 0)
    seq_of_row2 = (rowi2 % C) // SEQ_S
    lhs = [jnp.concatenate([g["solk"][hh * C:(hh + 1) * C], g["qst"][hh * C:(hh + 1) * C]], axis=0)
           for hh in range(NH)]
    accs = [jnp.zeros((2 * C, GDN_DV), F32) for _ in range(NH)]
    for n in range(NSEQ):
        rm2 = seq_of_row2 == n
        for hh in range(NH):
            accs[hh] = jnp.where(rm2, _mm(lhs[hh], sgdn_in[n, hh]), accs[hh])
    us = [g["solv"][hh * C:(hh + 1) * C] - accs[hh][:C] for hh in range(NH)]
    ost = (g["ebcol"] * jnp.concatenate([accs[hh][C:] for hh in range(NH)], axis=0)
           + _mm(g["qk"], jnp.concatenate(us, axis=0)))
    kdts = [g["kdec"][hh * C:(hh + 1) * C].T for hh in range(NH)]
    for n in range(NSEQ):
        lm = seq_of_lane == n
        for hh in range(NH):
            scale = g["eblast"][hh * C + n * SEQ_S:hh * C + n * SEQ_S + 1]
            sgdn_out[n, hh] = scale * sgdn_in[n, hh] + _mm(jnp.where(lm, kdts[hh], 0.0), us[hh])
    y_d = _gdn_out(jnp.concatenate([ost[hh * C:(hh + 1) * C] for hh in range(NH)], axis=1),
                   z[:, Z_DG:Z_DG + 512], k)

    o, qe, ke, v, b, blast = _gla_intra(z, k, nlev)
    ket = ke.T
    hi, mid = _split2(blast * (1.0 / SEQ_S))
    blt2 = lax.dot_general(jnp.concatenate([hi, mid], axis=1), k["seqsel"][...], TN, preferred_element_type=F32)
    eblt = jnp.exp(blt2[:NH * GLA_DK] + blt2[NH * GLA_DK:])
    accs = [jnp.zeros((C, GLA_DV), F32) for _ in range(NH)]
    for n in range(NSEQ):
        rm = seq_of_row == n
        lm = seq_of_lane == n
        for hh in range(NH):
            sk = slice(hh * GLA_DK, (hh + 1) * GLA_DK)
            sv = slice(hh * GLA_DV, (hh + 1) * GLA_DV)
            s = sgla_in[n, hh]
            accs[hh] = jnp.where(rm, _mm(qe[:, sk], s), accs[hh])
            sgla_out[n, hh] = s * eblt[sk, n:n + 1] + _mm(jnp.where(lm, ket[sk], 0.0), v[:, sv])
    o = o + jnp.concatenate(accs, axis=1)
    y_l = _gla_out(o, z[:, Z_LG:Z_LG + 256], k)

    y = jnp.concatenate([y_r, y_d, y_l], axis=1).astype(BF16)
    o_ref[...] = x + jnp.dot(y, wout_ref[...].astype(BF16), preferred_element_type=F32)


def _mixer_sample(x, l, nw, win, wout, cos, sin, sret, sgdn, sgla, cbuf, acc_ret, acc_gdn, acc_gla,
                  params, consts, *, row0):
    nb = sret.shape[1]
    n = nb * SEQ_S
    nlev = len(_levels(SEQ_S))
    blk0 = row0 // C

    def st_spec(dk, dv):
        return pl.BlockSpec((None, NSEQ, NH, dk, dv), lambda i: (l, i, 0, 0, 0))

    st_specs = [st_spec(RET_DK, RET_DV), st_spec(GDN_DK, GDN_DV), st_spec(GLA_DK, GLA_DV)]
    any_spec = pl.BlockSpec(memory_space=pl.ANY)
    z = _sample_inproj(x, l, nw, win, row0=row0, n=n)
    in_specs = [
        pl.BlockSpec((C, D_MODEL), lambda i: (blk0 + i, 0)),
        pl.BlockSpec((C, NZ), lambda i: (i, 0)),
        _layer_resident(wout, l), _resident(cos), _resident(sin),
    ] + st_specs + [pl.BlockSpec((C, CONV_DIM), lambda i: (i, 0))] + [
        _layer_resident(params[nm], l) for nm in _PARAM_NAMES] + [
        _resident(consts[nm]) for nm in _CONST_NAMES] + [any_spec] * 3
    extras = [params[nm] for nm in _PARAM_NAMES] + [consts[nm] for nm in _CONST_NAMES]
    n_in = len(in_specs)
    out_shape = (
        jax.ShapeDtypeStruct(x.shape, F32),
        jax.ShapeDtypeStruct(acc_ret.shape, F32),
        jax.ShapeDtypeStruct(acc_gdn.shape, F32),
        jax.ShapeDtypeStruct(acc_gla.shape, F32),
        jax.ShapeDtypeStruct((n, CONV_DIM), F32),
    )
    out_specs = (pl.BlockSpec((C, D_MODEL), lambda i: (blk0 + i, 0)),) + tuple(st_specs) + (
        pl.BlockSpec((C, CONV_DIM), lambda i: (i, 0)),)
    return pl.pallas_call(
        functools.partial(_mixer_sample_kernel, nlev=nlev),
        grid=(n // C,),
        in_specs=in_specs,
        out_specs=out_specs,
        out_shape=out_shape,
        input_output_aliases={0: 0, n_in - 3: 1, n_in - 2: 2, n_in - 1: 3},
        compiler_params=pltpu.CompilerParams(
            dimension_semantics=("parallel",), vmem_limit_bytes=VMEM_LIMIT),
        name="mixer_sample",
    )(x, z, wout, cos, sin, sret, sgdn, sgla, cbuf, *extras, acc_ret, acc_gdn, acc_gla)


def _permute_w_in(w):
    wt = jnp.swapaxes(w, 1, 2)
    pad = jnp.zeros((w.shape[0], NZ - Z_SM - 24, w.shape[1]), w.dtype)
    return jnp.concatenate(
        [wt[:, 0:2560], wt[:, 2568:3080], wt[:, 3080:3592], wt[:, 3608:3864], wt[:, 2560:2568],
         wt[:, 3592:3608], pad], axis=1).astype(BF16)


def _rows(v, width=None):
    v = v.astype(F32)[:, None, :]
    if width is not None and v.shape[-1] < width:
        v = jnp.pad(v, ((0, 0), (0, 0), (0, width - v.shape[-1])))
    return v


def kernel(x_prompt, x_sample, state_ret, state_gdn, state_gdn_conv, state_gla, norm_ffn1, ffn1_w1, ffn1_w3, ffn1_w2, norm_mix, w_in, ret_gn_w, ret_gn_b, gdn_conv_w, gdn_A_log, gdn_dt_bias, gdn_norm_w, gla_gate_w, gla_gate_b, gla_norm_w, w_out, norm_ffn2, ffn2_w1, ffn2_w3, ffn2_w2, norm_final):
    bp, tp, _ = x_prompt.shape
    bs, ts, _ = x_sample.shape
    n_prompt, n_sample = bp * tp, bs * ts
    assert ts == SEQ_S and tp % MIX_TC == 0 and n_sample % C == 0 and n_prompt % C == 0

    consts_p = dict(_chunk_consts(C), **_ret_consts(C))
    consts_s = dict(_chunk_consts(SEQ_S), **_ret_consts(SEQ_S))
    cos_p, sin_p = _rope_tables(jnp.arange(tp, dtype=jnp.int32))
    cos_s, sin_s = _rope_tables(PAST_LEN + (jnp.arange(C, dtype=jnp.int32) % SEQ_S))

    params = dict(
        alog=_rows(gdn_A_log, LANES), dtb=_rows(gdn_dt_bias, LANES),
        gwp=jnp.pad(gla_gate_w, ((0, 0), (SM_LLR, LANES - SM_LLR - GLA_RANK), (0, 0))).astype(BF16),
        gla_gate_b=_rows(gla_gate_b), ret_gn_w=_rows(ret_gn_w), ret_gn_b=_rows(ret_gn_b),
        gdn_norm_w=_rows(gdn_norm_w), gla_norm_w=_rows(jnp.tile(gla_norm_w, (1, NH))),
        conv_w=gdn_conv_w.astype(F32),
    )
    win = _permute_w_in(w_in)
    wout = w_out.astype(F32)
    f1 = (_rows(norm_ffn1), ffn1_w1.astype(F32), ffn1_w3.astype(F32), ffn1_w2.astype(F32))
    f2 = (_rows(norm_ffn2), ffn2_w1.astype(F32), ffn2_w3.astype(F32), ffn2_w2.astype(F32))
    nw = _rows(norm_mix)
    fw = norm_final.astype(F32)[None, :]
    sret_in, sgdn_in, sgla_in = state_ret.astype(F32), state_gdn.astype(F32), state_gla.astype(F32)
    cbuf = jnp.pad(state_gdn_conv.astype(F32), ((0, 0), (0, 0), (0, 1), (0, 0))).reshape(DEPTH, n_sample, CONV_DIM)
    acc_ret, acc_gdn, acc_gla = (lax.empty(s.shape, F32) for s in (sret_in, sgdn_in, sgla_in))

    ffn = functools.partial(_ffn, n_prompt=n_prompt, n_sample=n_sample)
    xs = (x_prompt.reshape(n_prompt, D_MODEL), x_sample.reshape(n_sample, D_MODEL))
    outs_p = [[] for _ in range(4)]
    conv_s = []
    for l in range(DEPTH):
        x = ffn(xs, l, *f1, fw, first=(l == 0), last=False)
        x, sret, sgdn, sgla, conv = _mixer_prompt(x, l, nw, win, wout, cos_p, sin_p, params, consts_p,
                                                  bsz=bp, tlen=tp)
        outs_p[0].append(jnp.stack([sret[:, h * 64:(h + 1) * 64, h * 64:(h + 1) * 64] for h in range(NH)], axis=1))
        outs_p[1].append(sgdn)
        outs_p[2].append(conv)
        outs_p[3].append(jnp.stack([sgla[:, h * 32:(h + 1) * 32, h * 64:(h + 1) * 64] for h in range(NH)], axis=1))
        x, acc_ret, acc_gdn, acc_gla, cout = _mixer_sample(
            x, l, nw, win, wout, cos_s, sin_s, sret_in, sgdn_in, sgla_in, cbuf[l], acc_ret, acc_gdn, acc_gla,
            params, consts_s, row0=n_prompt)
        conv_s.append(cout.reshape(bs, ts, CONV_DIM)[:, :CONV_W - 1])
        xs = (ffn((x,), l, *f2, fw, first=False, last=(l == DEPTH - 1)),)

    y_prompt, y_sample = xs[0]
    dts = (state_ret.dtype, state_gdn.dtype, state_gdn_conv.dtype, state_gla.dtype)
    sp = [jnp.stack(o).astype(d) for o, d in zip(outs_p, dts)]
    return (y_prompt.reshape(bp, tp, D_MODEL), y_sample.reshape(bs, ts, D_MODEL), sp[0], sp[1], sp[2], sp[3],
            acc_ret.astype(dts[0]), acc_gdn.astype(dts[1]), jnp.stack(conv_s).astype(dts[2]), acc_gla.astype(dts[3]))
```

```python
import functools

import numpy as np
import jax
import jax.numpy as jnp
from jax import lax
from jax.experimental import pallas as pl
from jax.experimental.pallas import tpu as pltpu

F32 = jnp.float32
BF16 = jnp.bfloat16

D_MODEL = 1024
DEPTH = 4
PAST_LEN = 16384
NH = 4
RET_DK = 64
RET_DV = 64
GDN_DK = 128
GDN_DV = 128
GLA_DK = 32
GLA_DV = 64
GLA_RANK = 16
GLA_GATE_NORM = 16.0
CONV_W = 4
CONV_DIM = NH * (2 * GDN_DK + GDN_DV)
D_FF = 2816
ROPE_BASE = 10000.0
EPS = 1e-6
GN_EPS = 1e-5

C = 64
R = NH * C
LANES = 128

Z_RQ, Z_RK, Z_RV, Z_RG = 0, 256, 512, 768
Z_CONV = 1024
Z_DG = 2560
Z_LQ, Z_LK, Z_LV, Z_LG = 3072, 3200, 3328, 3584
Z_SM = 3840
NZ = 3968
SM_DA, SM_DB, SM_LLR = 0, 4, 8

NN = (((1,), (0,)), ((), ()))
NT = (((1,), (1,)), ((), ()))
TN = (((0,), (0,)), ((), ()))

VMEM_LIMIT = 56 * 1024 * 1024
FFN_TM = 512
FFN_TF = 256
MIX_TC = 512


def _mm(a, b, dims=NN):
    return lax.dot_general(a.astype(BF16), b.astype(BF16), dims, preferred_element_type=F32)


def _proj(h, w_rows):
    return lax.dot_general(h, w_rows, NT, preferred_element_type=F32)


def _split2(x):
    hi = x.astype(BF16)
    return hi, (x - hi.astype(F32)).astype(BF16)


def _mm_data_const(x, cb):
    hi, mid = _split2(x)
    m = x.shape[0]
    r = lax.dot_general(jnp.concatenate([hi, mid], axis=0), cb, NN, preferred_element_type=F32)
    return r[:m] + r[m:]


def _mm_const_data(cb, x):
    hi, mid = _split2(x)
    n = x.shape[1]
    r = lax.dot_general(cb, jnp.concatenate([hi, mid], axis=1), NN, preferred_element_type=F32)
    return r[:, :n] + r[:, n:]


def _silu(x):
    hx = 0.5 * x
    return hx + hx * jnp.tanh(hx)


def _softplus(x):
    return jnp.maximum(x, 0.0) + jnp.log1p(jnp.exp(-jnp.abs(x)))


def _log_sigmoid(x):
    return jnp.minimum(x, 0.0) - jnp.log1p(jnp.exp(-jnp.abs(x)))


def _rms_rows(x, w):
    ms = jnp.mean(x * x, axis=-1, keepdims=True)
    return x * lax.rsqrt(ms + EPS) * w


def _stack_bd(x, width):
    lane = lax.broadcasted_iota(jnp.int32, x.shape, 1) // width
    zero = jnp.zeros_like(x)
    return jnp.concatenate([jnp.where(lane == h, x, zero) for h in range(NH)], axis=0)


def _col_to_compact(col):
    lane = lax.broadcasted_iota(jnp.int32, (C, R), 1) // C
    out = jnp.zeros((C, R), F32)
    for h in range(NH):
        out = jnp.where(lane == h, col[h * C:(h + 1) * C], out)
    return out


def _ffn_kernel(*refs, first, last, npb, tf, l):
    n_x = 2 if first else 1
    xrefs, (nw_ref, w1_hbm, w3_hbm, w2_hbm, fw_ref) = refs[:n_x], refs[n_x:n_x + 5]
    n_o = 2 if last else 1
    orefs = refs[n_x + 5:n_x + 5 + n_o]
    h_ref, acc_ref, w1s, w3s, w2s, sem = refs[n_x + 5 + n_o:]
    nf = D_FF // tf
    i = pl.program_id(0)

    def tile_copies(j):
        cols = pl.ds(j * tf, tf)
        return (pltpu.make_async_copy(w1_hbm.at[l, :, cols], w1s.at[j], sem.at[0, j]),
                pltpu.make_async_copy(w3_hbm.at[l, :, cols], w3s.at[j], sem.at[1, j]),
                pltpu.make_async_copy(w2_hbm.at[l, cols, :], w2s.at[j], sem.at[2, j]))

    @pl.when(i == 0)
    def _():
        for j in range(nf):
            for cp in tile_copies(j):
                cp.start()

    x = jnp.where(i < npb, xrefs[0][...], xrefs[1][...]) if first else xrefs[0][...]
    h_ref[...] = _rms_rows(x, nw_ref[...]).astype(BF16)

    def tiles(wait):
        for j in range(nf):
            if wait:
                for cp in tile_copies(j):
                    cp.wait()
            h = h_ref[...]
            a = jnp.dot(h, w1s[j].astype(BF16), preferred_element_type=F32)
            g = jnp.dot(h, w3s[j].astype(BF16), preferred_element_type=F32)
            p = jnp.dot((_silu(a) * g).astype(BF16), w2s[j].astype(BF16), preferred_element_type=F32)
            if j == 0:
                acc_ref[...] = p
            else:
                acc_ref[...] += p

    lax.cond(i == 0, lambda: tiles(True), lambda: tiles(False))
    y = x + 0.5 * acc_ref[...]
    if last:
        y = _rms_rows(y, fw_ref[...])

        @pl.when(i < npb)
        def _():
            orefs[0][...] = y

        @pl.when(i >= npb)
        def _():
            orefs[1][...] = y
    else:
        orefs[0][...] = y


def _resident(a):
    nd = a.ndim
    return pl.BlockSpec(a.shape, lambda *_, _n=nd: (0,) * _n, pipeline_mode=pl.Buffered(1))


def _layer_resident(a, l):
    nd = a.ndim - 1
    return pl.BlockSpec((None,) + a.shape[1:], lambda *_, _n=nd: (l,) + (0,) * _n, pipeline_mode=pl.Buffered(1))


def _ffn(xs, l, nw, w1, w3, w2, fw, *, first, last, n_prompt, n_sample, tm=FFN_TM, tf=FFN_TF):
    assert n_sample == tm and n_prompt % tm == 0
    npb = n_prompt // tm
    prompt_blk = pl.BlockSpec((tm, D_MODEL), lambda i: (jnp.minimum(i, npb - 1), 0))
    sample_blk = pl.BlockSpec((tm, D_MODEL), lambda i: (0, 0))
    unified_blk = pl.BlockSpec((tm, D_MODEL), lambda i: (i, 0))
    if last:
        out_specs = (prompt_blk, sample_blk)
        out_shape = (jax.ShapeDtypeStruct((n_prompt, D_MODEL), F32), jax.ShapeDtypeStruct((n_sample, D_MODEL), F32))
    else:
        out_specs = unified_blk
        out_shape = jax.ShapeDtypeStruct((n_prompt + n_sample, D_MODEL), F32)
    nf = D_FF // tf
    hbm = pl.BlockSpec(memory_space=pl.ANY)
    return pl.pallas_call(
        functools.partial(_ffn_kernel, first=first, last=last, npb=npb, tf=tf, l=l),
        grid=(npb + 1,),
        in_specs=([prompt_blk, sample_blk] if first else [unified_blk]) + [
            _layer_resident(nw, l), hbm, hbm, hbm, _resident(fw)],
        out_specs=out_specs,
        out_shape=out_shape,
        scratch_shapes=[pltpu.VMEM((tm, D_MODEL), BF16), pltpu.VMEM((tm, D_MODEL), F32),
                        pltpu.VMEM((nf, D_MODEL, tf), F32), pltpu.VMEM((nf, D_MODEL, tf), F32),
                        pltpu.VMEM((nf, tf, D_MODEL), F32), pltpu.SemaphoreType.DMA((3, nf))],
        compiler_params=pltpu.CompilerParams(
            dimension_semantics=("arbitrary",), vmem_limit_bytes=VMEM_LIMIT),
        name="ffn",
    )(*xs, nw, w1, w3, w2, fw)


def _levels(L):
    return [s for s in (1, 2, 4, 8, 16, 32) if s < L]


def _chunk_consts(L):
    i = np.arange(C)
    sid, p = i // L, i % L
    same = sid[:, None] == sid[None, :]
    causal = same & (i[:, None] >= i[None, :])
    strict = same & (i[:, None] > i[None, :])
    lv = _levels(L)
    cum, pm, lm = [causal, same], [np.eye(C, dtype=bool)], []
    for s in lv:
        blk = p // s
        inblk = same & (blk[:, None] == blk[None, :])
        odd = (blk % 2 == 1)[:, None]
        cum.append(inblk & np.where(odd, i[None, :] <= i[:, None], i[None, :] > i[:, None]))
        pm.append(same & odd & (blk[None, :] == blk[:, None] - 1))
        lm.append(strict & ((p[:, None] // (2 * s)) == (p[None, :] // (2 * s))) & (blk[:, None] != blk[None, :]))
    r = np.arange(R)
    bdfull = (r[:, None] // C) == (r[None, :] // C)
    wide = lambda m: np.tile(m, (1, NH))
    f = lambda m: jnp.asarray(np.asarray(m, dtype=np.float32))
    gla_bd = (np.arange(NH * GLA_DK)[:, None] // GLA_DK) == (np.arange(NH * GLA_DV)[None, :] // GLA_DV)
    seqsel = np.zeros((C, LANES), np.float32)
    seqsel[i, sid] = 1.0
    return dict(
        cumg=f(np.concatenate(cum, axis=0)).astype(BF16),
        cumd=f(np.concatenate(cum[:2], axis=0)).astype(BF16),
        pmc=f(np.stack([wide(m) for m in pm])),
        lmc=f(np.stack([wide(m) for m in lm])),
        causal4=f(wide(causal)), eye4=f(wide(np.eye(C, dtype=bool))),
        g64=(f(bdfull) * (1.0 / 64.0)).astype(BF16), bdfull=f(bdfull), glabd=f(gla_bd),
        seqsel=jnp.asarray(seqsel).astype(BF16),
    )


def _ret_consts(L):
    log_gamma = jnp.log(1.0 - 2.0 ** (-5.0 - jnp.arange(NH, dtype=F32)))
    i = np.arange(C)
    sid, p = i // L, (i % L).astype(np.float32)
    causal = (sid[:, None] == sid[None, :]) & (i[:, None] >= i[None, :])
    diff = jnp.asarray(np.where(causal, p[:, None] - p[None, :], 0.0).astype(np.float32))
    dec = jnp.where(causal[None], jnp.exp(diff[None] * log_gamma[:, None, None]), 0.0)
    lg_l = jnp.repeat(log_gamma, RET_DV)[None, :]
    pj = jnp.asarray(p)[:, None]
    return dict(
        decc=jnp.concatenate([dec[h] for h in range(NH)], axis=1),
        dq=jnp.exp((pj + 1.0) * lg_l),
        dk=jnp.exp((L - 1.0 - pj) * lg_l),
        dch=jnp.exp(L * lg_l),
    )


def _rope_tables(pos):
    half = RET_DK // 2
    inv = ROPE_BASE ** (-jnp.arange(half, dtype=F32) / half)
    ang = pos.astype(F32)[:, None] * inv[None, :]
    cos, sin = jnp.cos(ang), jnp.sin(ang)
    return (jnp.tile(jnp.concatenate([cos, cos], axis=1), (1, NH)),
            jnp.tile(jnp.concatenate([-sin, sin], axis=1), (1, NH)))


def _rotary(x, cos, sin):
    lane = lax.broadcasted_iota(jnp.int32, x.shape, 1)
    swapped = jnp.where((lane % RET_DK) < RET_DK // 2,
                        pltpu.roll(x, x.shape[1] - RET_DK // 2, 1), pltpu.roll(x, RET_DK // 2, 1))
    return x * cos + swapped * sin


def _ret_intra(z, cos, sin, k):
    q = _rotary(z[:, Z_RQ:Z_RQ + 256], cos, sin)
    kk = _rotary(z[:, Z_RK:Z_RK + 256], cos, sin) * (RET_DK ** -0.5)
    v = z[:, Z_RV:Z_RV + 256]
    sc = _mm(q, _stack_bd(kk.astype(BF16), RET_DK), NT) * k["decc"][...]
    o = _mm(sc, _stack_bd(v.astype(BF16), RET_DV))
    return o, q * k["dq"][...], kk * k["dk"][...], v


def _ret_out(o, rg, k):
    g = k["g64"][...]
    d = o - _mm_data_const(o, g)
    var = _mm_data_const(d * d, g)
    return _silu(rg) * (d * lax.rsqrt(var + GN_EPS) * k["ret_gn_w"][...] + k["ret_gn_b"][...])


def _gla_intra(z, k, nlev):
    sm = z[:, Z_SM:Z_SM + LANES]
    gk = _log_sigmoid(_mm(sm, k["gwp"][...]) + k["gla_gate_b"][...]) * (1.0 / GLA_GATE_NORM)
    q = z[:, Z_LQ:Z_LQ + 128] * (GLA_DK ** -0.5)
    kk = z[:, Z_LK:Z_LK + 128]
    v = z[:, Z_LV:Z_LV + 256]
    cs = _mm_const_data(k["cumg"][...], gk)
    b, blast = cs[:C], cs[C:2 * C]
    sc = _mm(q, _stack_bd(kk.astype(BF16), GLA_DK), NT) * k["pmc"][0]
    for li in range(nlev):
        e = jnp.exp(cs[(2 + li) * C:(3 + li) * C])
        sc = sc + _mm(q * e, _stack_bd((kk * e).astype(BF16), GLA_DK), NT) * k["pmc"][li + 1]
    o = _mm(sc, _stack_bd(v.astype(BF16), GLA_DV))
    return o, q * jnp.exp(b), kk * jnp.exp(blast - b), v, b, blast


def _gla_out(o, lg, k):
    ms = _mm_data_const(o * o, k["g64"][...])
    return o * lax.rsqrt(ms + EPS) * k["gla_norm_w"][...] * _silu(lg)


def _lane_col(blk, lane0):
    lane = lax.broadcasted_iota(jnp.int32, blk.shape, 1)
    st = jnp.concatenate([jnp.where(lane == lane0 + h, blk, 0.0) for h in range(NH)], axis=0)
    return jnp.sum(st, axis=-1, keepdims=True)


def _gdn_front(conv, sm, k):
    def l2n(x):
        return x * lax.rsqrt(jnp.sum(x * x, axis=-1, keepdims=True) + EPS)

    qs = [l2n(conv[h]) * (GDN_DK ** -0.5) for h in range(NH)]
    ks = [l2n(conv[NH + h]) for h in range(NH)]
    vst = jnp.concatenate(conv[2 * NH:], axis=0)
    qst = jnp.concatenate(qs, axis=0)
    kst = jnp.concatenate(ks, axis=0)
    kq = []
    for p in range(NH // 2):
        kp = jnp.concatenate(ks[2 * p:2 * p + 2], axis=1).astype(BF16)
        lane = lax.broadcasted_iota(jnp.int32, kp.shape, 1) // GDN_DK
        kbd = jnp.concatenate([jnp.where(lane == hl, kp, jnp.zeros_like(kp)) for hl in range(2)], axis=0)
        lhs = jnp.concatenate([kp, jnp.concatenate(qs[2 * p:2 * p + 2], axis=1).astype(BF16)], axis=0)
        kq.append(lax.dot_general(lhs, kbd, NT, preferred_element_type=F32))
    kq = jnp.concatenate(kq, axis=1)
    gd = -jnp.exp(k["alog"][...]) * _softplus(sm + k["dtb"][...])
    beta = jax.nn.sigmoid(sm)
    cs = _mm_const_data(k["cumd"][...], gd)
    bcol = _lane_col(cs[:C], SM_DA)
    blcol = _lane_col(cs[C:], SM_DA)
    betacol = _lane_col(beta, SM_DB)
    bcolc = _col_to_compact(bcol)
    browc = jnp.sum(bcolc * k["eye4"][...], axis=0, keepdims=True)
    causal = k["causal4"][...] > 0.5
    decc = jnp.where(causal, jnp.exp(jnp.where(causal, bcolc - browc, 0.0)), 0.0)
    ac = kq[:C] * decc * _col_to_compact(betacol)
    qkc = kq[C:] * decc
    return dict(ac=ac, qkc=qkc, qst=qst, kst=kst, vst=vst, bcol=bcol, blcol=blcol, betacol=betacol)


def _gdn_inverse(acs, k, nlev):
    xs = [k["eye4"][...] - a * k["lmc"][0] for a in acs]
    for li in range(1, nlev):
        ms = [a * k["lmc"][li] for a in acs]
        ys = [_mm(m, _stack_bd(x.astype(BF16), C)) for m, x in zip(ms, xs)]
        xs = [x - _mm(x, _stack_bd(y.astype(BF16), C)) for x, y in zip(xs, ys)]
    return xs


def _gdn_back(f, xc):
    kb = f["kst"] * f["betacol"]
    rhs = jnp.concatenate([f["vst"] * f["betacol"], kb * jnp.exp(f["bcol"])], axis=1)
    sol = _mm(_stack_bd(xc.astype(BF16), C), rhs)
    return dict(qst=f["qst"], solv=sol[:, :GDN_DV], solk=sol[:, GDN_DV:],
                qk=_stack_bd(f["qkc"].astype(BF16), C), ebcol=jnp.exp(f["bcol"]),
                kdec=f["kst"] * jnp.exp(f["blcol"] - f["bcol"]), eblast=jnp.exp(f["blcol"]))


def _gdn_out(o, dg, k):
    w = k["gdn_norm_w"][...]
    y = jnp.concatenate([_rms_rows(o[:, h * GDN_DV:(h + 1) * GDN_DV], w) for h in range(NH)], axis=1)
    return y * _silu(dg)


_PARAM_NAMES = ("alog", "dtb", "gwp", "gla_gate_b", "ret_gn_w", "ret_gn_b", "gdn_norm_w", "gla_norm_w", "conv_w")
_CONST_NAMES = ("cumg", "cumd", "pmc", "lmc", "causal4", "eye4", "g64", "bdfull", "glabd", "seqsel",
                "decc", "dq", "dk", "dch")


def _mixer_prompt_kernel(*refs, tc, nlev):
    n_in = 6 + len(_PARAM_NAMES) + len(_CONST_NAMES)
    x_ref, nw_ref, win_ref, wout_ref, cos_ref, sin_ref = refs[:6]
    k = dict(zip(_PARAM_NAMES + _CONST_NAMES, refs[6:n_in]))
    o_ref, sret_ref, sgdn_ref, sgla_ref, conv_ref = refs[n_in:n_in + 5]
    z_ref, xp_ref, cv_ref, y_ref = refs[n_in + 5:]
    t = pl.program_id(1)
    nch = tc // C
    ng = CONV_DIM // LANES

    @pl.when(t == 0)
    def _():
        sret_ref[...] = jnp.zeros_like(sret_ref)
        sgdn_ref[...] = jnp.zeros_like(sgdn_ref)
        sgla_ref[...] = jnp.zeros_like(sgla_ref)
        xp_ref[:, pl.ds(0, 8), :] = jnp.zeros((ng, 8, LANES), F32)

    h = _rms_rows(x_ref[...], nw_ref[...]).astype(BF16)
    zc = _proj(h, win_ref[Z_CONV:Z_CONV + CONV_DIM, :])
    for g in range(ng):
        xp_ref[g, pl.ds(8, tc), :] = zc[:, g * LANES:(g + 1) * LANES]
    z_ref[:, :Z_CONV] = _proj(h, win_ref[:Z_CONV, :])
    z_ref[:, Z_DG:] = _proj(h, win_ref[Z_DG:, :])

    cw = k["conv_w"]
    nb = tc // 8
    for g in range(ng):
        wg = [cw[i:i + 1, g * LANES:(g + 1) * LANES] for i in range(CONV_W)]
        taps = {s: xp_ref[g, pl.ds(s, nb, stride=8), :] for s in range(5, 5 + 8 + CONV_W - 1)}
        for j in range(8):
            acc = taps[j + 8] * wg[3]
            for i in range(CONV_W - 1):
                acc = acc + taps[j + 5 + i] * wg[i]
            cv_ref[g, pl.ds(j, nb, stride=8), :] = _silu(acc)
        xp_ref[g, pl.ds(0, 8), :] = xp_ref[g, pl.ds(tc, 8), :]

    zs = [z_ref[pl.ds(c * C, C), :] for c in range(nch)]
    fronts = [_gdn_front([cv_ref[g, pl.ds(c * C, C), :] for g in range(ng)], z[:, Z_SM:Z_SM + LANES], k)
              for c, z in enumerate(zs)]
    xcs = _gdn_inverse([f["ac"] for f in fronts], k, nlev)
    gs = [_gdn_back(f, xc) for f, xc in zip(fronts, xcs)]
    rets = [_ret_intra(z, cos_ref[pl.ds(c * C, C), :], sin_ref[pl.ds(c * C, C), :], k) for c, z in enumerate(zs)]
    glas = [_gla_intra(z, k, nlev) for z in zs]

    for c in range(nch):
        rows = pl.ds(c * C, C)

        o, qd, kd, v = rets[c]
        s = sret_ref[0]
        z_ref[rows, Z_RQ:Z_RQ + 256] = o + _mm(qd, s)
        sret_ref[0] = s * k["dch"][...] + _mm(kd, v, TN) * k["bdfull"][...]

        g = gs[c]
        us, qss = [], []
        for hh in range(NH):
            lhs = jnp.concatenate([g["solk"][hh * C:(hh + 1) * C], g["qst"][hh * C:(hh + 1) * C]], axis=0)
            r = _mm(lhs, sgdn_ref[0, hh])
            us.append(g["solv"][hh * C:(hh + 1) * C] - r[:C])
            qss.append(r[C:])
        ost = g["ebcol"] * jnp.concatenate(qss, axis=0) + _mm(g["qk"], jnp.concatenate(us, axis=0))
        for hh in range(NH):
            sl = slice(hh * C, (hh + 1) * C)
            sgdn_ref[0, hh] = (g["eblast"][hh * C:hh * C + 1] * sgdn_ref[0, hh]
                               + _mm(g["kdec"][sl], us[hh], TN))
        z_ref[rows, Z_CONV:Z_CONV + NH * GDN_DV] = jnp.concatenate(
            [ost[hh * C:(hh + 1) * C] for hh in range(NH)], axis=1)

        o, qe, ke, v, b, _ = glas[c]
        s = sgla_ref[0]
        z_ref[rows, Z_LQ:Z_LQ + 256] = o + _mm(qe, s)
        escale = jnp.exp(b[C - 8:, :].T[:, 7:8])
        sgla_ref[0] = s * escale + _mm(ke, v, TN) * k["glabd"][...]

    y_r = _ret_out(z_ref[:, Z_RQ:Z_RQ + 256], z_ref[:, Z_RG:Z_RG + 256], k)
    y_d = _gdn_out(z_ref[:, Z_CONV:Z_CONV + NH * GDN_DV], z_ref[:, Z_DG:Z_DG + 512], k)
    y_l = _gla_out(z_ref[:, Z_LQ:Z_LQ + 256], z_ref[:, Z_LG:Z_LG + 256], k)
    y_ref[...] = jnp.concatenate([y_r, y_d, y_l], axis=1).astype(BF16)

    o_ref[...] = x_ref[...] + jnp.dot(y_ref[...], wout_ref[...].astype(BF16), preferred_element_type=F32)

    @pl.when(t == pl.num_programs(1) - 1)
    def _():
        for g in range(ng):
            conv_ref[0, :, g * LANES:(g + 1) * LANES] = xp_ref[g, pl.ds(5, 3), :]


def _mixer_prompt(x, l, nw, win, wout, cos, sin, params, consts, *, bsz, tlen, tc=MIX_TC):
    nlev = len(_levels(C))
    nt = tlen // tc
    in_specs = [
        pl.BlockSpec((tc, D_MODEL), lambda b, t: (b * nt + t, 0)),
        _layer_resident(nw, l), _layer_resident(win, l), _layer_resident(wout, l),
        pl.BlockSpec((tc, 256), lambda b, t: (t, 0)),
        pl.BlockSpec((tc, 256), lambda b, t: (t, 0)),
    ] + [_layer_resident(params[n], l) for n in _PARAM_NAMES] + [_resident(consts[n]) for n in _CONST_NAMES]
    extras = [params[n] for n in _PARAM_NAMES] + [consts[n] for n in _CONST_NAMES]
    out_shape = (
        jax.ShapeDtypeStruct(x.shape, F32),
        jax.ShapeDtypeStruct((bsz, 256, 256), F32),
        jax.ShapeDtypeStruct((bsz, NH, GDN_DK, GDN_DV), F32),
        jax.ShapeDtypeStruct((bsz, NH * GLA_DK, NH * GLA_DV), F32),
        jax.ShapeDtypeStruct((bsz, CONV_W - 1, CONV_DIM), F32),
    )
    out_specs = (
        pl.BlockSpec((tc, D_MODEL), lambda b, t: (b * nt + t, 0)),
        pl.BlockSpec((1, 256, 256), lambda b, t: (b, 0, 0)),
        pl.BlockSpec((1, NH, GDN_DK, GDN_DV), lambda b, t: (b, 0, 0, 0)),
        pl.BlockSpec((1, NH * GLA_DK, NH * GLA_DV), lambda b, t: (b, 0, 0)),
        pl.BlockSpec((1, CONV_W - 1, CONV_DIM), lambda b, t: (b, 0, 0)),
    )
    return pl.pallas_call(
        functools.partial(_mixer_prompt_kernel, tc=tc, nlev=nlev),
        grid=(bsz, nt),
        in_specs=in_specs,
        out_specs=out_specs,
        out_shape=out_shape,
        input_output_aliases={0: 0},
        scratch_shapes=[
            pltpu.VMEM((tc, NZ), F32),
            pltpu.VMEM((CONV_DIM // LANES, tc + 8, LANES), F32),
            pltpu.VMEM((CONV_DIM // LANES, tc, LANES), F32),
            pltpu.VMEM((tc, D_MODEL), BF16),
        ],
        compiler_params=pltpu.CompilerParams(
            dimension_semantics=("parallel", "arbitrary"), vmem_limit_bytes=VMEM_LIMIT),
        name="mixer_prompt",
    )(x, nw, win, wout, cos, sin, *extras)


SEQ_S = 4
NSEQ = C // SEQ_S


def _inproj_kernel(x_ref, nw_ref, win_ref, z_ref):
    z_ref[...] = _proj(_rms_rows(x_ref[...], nw_ref[...]).astype(BF16), win_ref[...])


def _sample_inproj(x, l, nw, win, *, row0, n):
    return pl.pallas_call(
        _inproj_kernel,
        grid=(1,),
        in_specs=[pl.BlockSpec((n, D_MODEL), lambda i: (row0 // n, 0)), _layer_resident(nw, l),
                  _layer_resident(win, l)],
        out_specs=pl.BlockSpec((n, NZ), lambda i: (0, 0)),
        out_shape=jax.ShapeDtypeStruct((n, NZ), F32),
        compiler_params=pltpu.CompilerParams(
            dimension_semantics=("arbitrary",), vmem_limit_bytes=VMEM_LIMIT),
        name="sample_inproj",
    )(x, nw, win)


def _mixer_sample_kernel(*refs, nlev):
    n_in = 9 + len(_PARAM_NAMES) + len(_CONST_NAMES)
    (x_ref, zin_ref, wout_ref, cos_ref, sin_ref,
     sret_in, sgdn_in, sgla_in, cbuf_ref) = refs[:9]
    k = dict(zip(_PARAM_NAMES + _CONST_NAMES, refs[9:n_in]))
    o_ref, sret_out, sgdn_out, sgla_out, cout_ref = refs[n_in + 3:n_in + 8]

    x = x_ref[...]
    z = zin_ref[...]

    rowi = lax.broadcasted_iota(jnp.int32, (C, 1), 0)
    tpos = rowi % SEQ_S

    xc = z[:, Z_CONV:Z_CONV + CONV_DIM]
    cb = cbuf_ref[...]
    cw = k["conv_w"]
    acc = xc * cw[3:4, :]
    for i in range(CONV_W - 1):
        cur = pltpu.roll(xc, 3 - i, 0)
        old = cb if i == 0 else pltpu.roll(cb, C - i, 0)
        acc = acc + jnp.where(tpos + i >= 3, cur, old) * cw[i:i + 1, :]
    cout_ref[...] = pltpu.roll(xc, C - 1, 0)
    conv = _silu(acc)

    seq_of_row = rowi // SEQ_S
    seq_of_lane = lax.broadcasted_iota(jnp.int32, (1, C), 1) // SEQ_S

    o, qd, kd, v = _ret_intra(z, cos_ref[...], sin_ref[...], k)
    dch = k["dch"][...]
    kdt = kd.T
    accs = [jnp.zeros((C, RET_DV), F32) for _ in range(NH)]
    for n in range(NSEQ):
        rm = seq_of_row == n
        lm = seq_of_lane == n
        for hh in range(NH):
            sl = slice(hh * RET_DK, (hh + 1) * RET_DK)
            s = sret_in[n, hh]
            accs[hh] = jnp.where(rm, _mm(qd[:, sl], s), accs[hh])
            sret_out[n, hh] = s * dch[:, sl] + _mm(jnp.where(lm, kdt[sl], 0.0), v[:, sl])
    o = o + jnp.concatenate(accs, axis=1)
    y_r = _ret_out(o, z[:, Z_RG:Z_RG + 256], k)

    f = _gdn_front([conv[:, gi * LANES:(gi + 1) * LANES] for gi in range(CONV_DIM // LANES)],
                   z[:, Z_SM:Z_SM + LANES], k)
    g = _gdn_back(f, _gdn_inverse([f["ac"]], k, nlev)[0])
    rowi2 = lax.broadcasted_iota(jnp.int32, (2 * C, 1), 0)
    seq_of_row2 = (rowi2 % C) // SEQ_S
    lhs = [jnp.concatenate([g["solk"][hh * C:(hh + 1) * C], g["qst"][hh * C:(hh + 1) * C]], axis=0)
           for hh in range(NH)]
    accs = [jnp.zeros((2 * C, GDN_DV), F32) for _ in range(NH)]
    for n in range(NSEQ):
        rm2 = seq_of_row2 == n
        for hh in range(NH):
            accs[hh] = jnp.where(rm2, _mm(lhs[hh], sgdn_in[n, hh]), accs[hh])
    us = [g["solv"][hh * C:(hh + 1) * C] - accs[hh][:C] for hh in range(NH)]
    ost = (g["ebcol"] * jnp.concatenate([accs[hh][C:] for hh in range(NH)], axis=0)
           + _mm(g["qk"], jnp.concatenate(us, axis=0)))
    kdts = [g["kdec"][hh * C:(hh + 1) * C].T for hh in range(NH)]
    for n in range(NSEQ):
        lm = seq_of_lane == n
        for hh in range(NH):
            scale = g["eblast"][hh * C + n * SEQ_S:hh * C + n * SEQ_S + 1]
            sgdn_out[n, hh] = scale * sgdn_in[n, hh] + _mm(jnp.where(lm, kdts[hh], 0.0), us[hh])
    y_d = _gdn_out(jnp.concatenate([ost[hh * C:(hh + 1) * C] for hh in range(NH)], axis=1),
                   z[:, Z_DG:Z_DG + 512], k)

    o, qe, ke, v, b, blast = _gla_intra(z, k, nlev)
    ket = ke.T
    hi, mid = _split2(blast * (1.0 / SEQ_S))
    blt2 = lax.dot_general(jnp.concatenate([hi, mid], axis=1), k["seqsel"][...], TN, preferred_element_type=F32)
    eblt = jnp.exp(blt2[:NH * GLA_DK] + blt2[NH * GLA_DK:])
    accs = [jnp.zeros((C, GLA_DV), F32) for _ in range(NH)]
    for n in range(NSEQ):
        rm = seq_of_row == n
        lm = seq_of_lane == n
        for hh in range(NH):
            sk = slice(hh * GLA_DK, (hh + 1) * GLA_DK)
            sv = slice(hh * GLA_DV, (hh + 1) * GLA_DV)
            s = sgla_in[n, hh]
            accs[hh] = jnp.where(rm, _mm(qe[:, sk], s), accs[hh])
            sgla_out[n, hh] = s * eblt[sk, n:n + 1] + _mm(jnp.where(lm, ket[sk], 0.0), v[:, sv])
    o = o + jnp.concatenate(accs, axis=1)
    y_l = _gla_out(o, z[:, Z_LG:Z_LG + 256], k)

    y = jnp.concatenate([y_r, y_d, y_l], axis=1).astype(BF16)
    o_ref[...] = x + jnp.dot(y, wout_ref[...].astype(BF16), preferred_element_type=F32)


def _mixer_sample(x, l, nw, win, wout, cos, sin, sret, sgdn, sgla, cbuf, acc_ret, acc_gdn, acc_gla,
                  params, consts, *, row0):
    nb = sret.shape[1]
    n = nb * SEQ_S
    nlev = len(_levels(SEQ_S))
    blk0 = row0 // C

    def st_spec(dk, dv):
        return pl.BlockSpec((None, NSEQ, NH, dk, dv), lambda i: (l, i, 0, 0, 0))

    st_specs = [st_spec(RET_DK, RET_DV), st_spec(GDN_DK, GDN_DV), st_spec(GLA_DK, GLA_DV)]
    any_spec = pl.BlockSpec(memory_space=pl.ANY)
    z = _sample_inproj(x, l, nw, win, row0=row0, n=n)
    in_specs = [
        pl.BlockSpec((C, D_MODEL), lambda i: (blk0 + i, 0)),
        pl.BlockSpec((C, NZ), lambda i: (i, 0)),
        _layer_resident(wout, l), _resident(cos), _resident(sin),
    ] + st_specs + [pl.BlockSpec((C, CONV_DIM), lambda i: (i, 0))] + [
        _layer_resident(params[nm], l) for nm in _PARAM_NAMES] + [
        _resident(consts[nm]) for nm in _CONST_NAMES] + [any_spec] * 3
    extras = [params[nm] for nm in _PARAM_NAMES] + [consts[nm] for nm in _CONST_NAMES]
    n_in = len(in_specs)
    out_shape = (
        jax.ShapeDtypeStruct(x.shape, F32),
        jax.ShapeDtypeStruct(acc_ret.shape, F32),
        jax.ShapeDtypeStruct(acc_gdn.shape, F32),
        jax.ShapeDtypeStruct(acc_gla.shape, F32),
        jax.ShapeDtypeStruct((n, CONV_DIM), F32),
    )
    out_specs = (pl.BlockSpec((C, D_MODEL), lambda i: (blk0 + i, 0)),) + tuple(st_specs) + (
        pl.BlockSpec((C, CONV_DIM), lambda i: (i, 0)),)
    return pl.pallas_call(
        functools.partial(_mixer_sample_kernel, nlev=nlev),
        grid=(n // C,),
        in_specs=in_specs,
        out_specs=out_specs,
        out_shape=out_shape,
        input_output_aliases={0: 0, n_in - 3: 1, n_in - 2: 2, n_in - 1: 3},
        compiler_params=pltpu.CompilerParams(
            dimension_semantics=("parallel",), vmem_limit_bytes=VMEM_LIMIT),
        name="mixer_sample",
    )(x, z, wout, cos, sin, sret, sgdn, sgla, cbuf, *extras, acc_ret, acc_gdn, acc_gla)


def _permute_w_in(w):
    wt = jnp.swapaxes(w, 1, 2)
    pad = jnp.zeros((w.shape[0], NZ - Z_SM - 24, w.shape[1]), w.dtype)
    return jnp.concatenate(
        [wt[:, 0:2560], wt[:, 2568:3080], wt[:, 3080:3592], wt[:, 3608:3864], wt[:, 2560:2568],
         wt[:, 3592:3608], pad], axis=1).astype(BF16)


def _rows(v, width=None):
    v = v.astype(F32)[:, None, :]
    if width is not None and v.shape[-1] < width:
        v = jnp.pad(v, ((0, 0), (0, 0), (0, width - v.shape[-1])))
    return v


def kernel(x_prompt, x_sample, state_ret, state_gdn, state_gdn_conv, state_gla, norm_ffn1, ffn1_w1, ffn1_w3, ffn1_w2, norm_mix, w_in, ret_gn_w, ret_gn_b, gdn_conv_w, gdn_A_log, gdn_dt_bias, gdn_norm_w, gla_gate_w, gla_gate_b, gla_norm_w, w_out, norm_ffn2, ffn2_w1, ffn2_w3, ffn2_w2, norm_final):
    bp, tp, _ = x_prompt.shape
    bs, ts, _ = x_sample.shape
    n_prompt, n_sample = bp * tp, bs * ts
    assert ts == SEQ_S and tp % MIX_TC == 0 and n_sample % C == 0 and n_prompt % C == 0

    consts_p = dict(_chunk_consts(C), **_ret_consts(C))
    consts_s = dict(_chunk_consts(SEQ_S), **_ret_consts(SEQ_S))
    cos_p, sin_p = _rope_tables(jnp.arange(tp, dtype=jnp.int32))
    cos_s, sin_s = _rope_tables(PAST_LEN + (jnp.arange(C, dtype=jnp.int32) % SEQ_S))

    params = dict(
        alog=_rows(gdn_A_log, LANES), dtb=_rows(gdn_dt_bias, LANES),
        gwp=jnp.pad(gla_gate_w, ((0, 0), (SM_LLR, LANES - SM_LLR - GLA_RANK), (0, 0))).astype(BF16),
        gla_gate_b=_rows(gla_gate_b), ret_gn_w=_rows(ret_gn_w), ret_gn_b=_rows(ret_gn_b),
        gdn_norm_w=_rows(gdn_norm_w), gla_norm_w=_rows(jnp.tile(gla_norm_w, (1, NH))),
        conv_w=gdn_conv_w.astype(F32),
    )
    win = _permute_w_in(w_in)
    wout = w_out.astype(F32)
    f1 = (_rows(norm_ffn1), ffn1_w1.astype(F32), ffn1_w3.astype(F32), ffn1_w2.astype(F32))
    f2 = (_rows(norm_ffn2), ffn2_w1.astype(F32), ffn2_w3.astype(F32), ffn2_w2.astype(F32))
    nw = _rows(norm_mix)
    fw = norm_final.astype(F32)[None, :]
    sret_in, sgdn_in, sgla_in = state_ret.astype(F32), state_gdn.astype(F32), state_gla.astype(F32)
    cbuf = jnp.pad(state_gdn_conv.astype(F32), ((0, 0), (0, 0), (0, 1), (0, 0))).reshape(DEPTH, n_sample, CONV_DIM)
    acc_ret, acc_gdn, acc_gla = (lax.empty(s.shape, F32) for s in (sret_in, sgdn_in, sgla_in))

    ffn = functools.partial(_ffn, n_prompt=n_prompt, n_sample=n_sample)
    xs = (x_prompt.reshape(n_prompt, D_MODEL), x_sample.reshape(n_sample, D_MODEL))
    outs_p = [[] for _ in range(4)]
    conv_s = []
    for l in range(DEPTH):
        x = ffn(xs, l, *f1, fw, first=(l == 0), last=False)
        x, sret, sgdn, sgla, conv = _mixer_prompt(x, l, nw, win, wout, cos_p, sin_p, params, consts_p,
                                                  bsz=bp, tlen=tp)
        outs_p[0].append(jnp.stack([sret[:, h * 64:(h + 1) * 64, h * 64:(h + 1) * 64] for h in range(NH)], axis=1))
        outs_p[1].append(sgdn)
        outs_p[2].append(conv)
        outs_p[3].append(jnp.stack([sgla[:, h * 32:(h + 1) * 32, h * 64:(h + 1) * 64] for h in range(NH)], axis=1))
        x, acc_ret, acc_gdn, acc_gla, cout = _mixer_sample(
            x, l, nw, win, wout, cos_s, sin_s, sret_in, sgdn_in, sgla_in, cbuf[l], acc_ret, acc_gdn, acc_gla,
            params, consts_s, row0=n_prompt)
        conv_s.append(cout.reshape(bs, ts, CONV_DIM)[:, :CONV_W - 1])
        xs = (ffn((x,), l, *f2, fw, first=False, last=(l == DEPTH - 1)),)

    y_prompt, y_sample = xs[0]
    dts = (state_ret.dtype, state_gdn.dtype, state_gdn_conv.dtype, state_gla.dtype)
    sp = [jnp.stack(o).astype(d) for o, d in zip(outs_p, dts)]
    return (y_prompt.reshape(bp, tp, D_MODEL), y_sample.reshape(bs, ts, D_MODEL), sp[0], sp[1], sp[2], sp[3],
            acc_ret.astype(dts[0]), acc_gdn.astype(dts[1]), jnp.stack(conv_s).astype(dts[2]), acc_gla.astype(dts[3]))
```

```python
import functools

import numpy as np
import jax
import jax.numpy as jnp
from jax import lax
from jax.experimental import pallas as pl
from jax.experimental.pallas import tpu as pltpu

F32 = jnp.float32
BF16 = jnp.bfloat16

D_MODEL = 1024
DEPTH = 4
PAST_LEN = 16384
NH = 4
RET_DK = 64
RET_DV = 64
GDN_DK = 128
GDN_DV = 128
GLA_DK = 32
GLA_DV = 64
GLA_RANK = 16
GLA_GATE_NORM = 16.0
CONV_W = 4
CONV_DIM = NH * (2 * GDN_DK + GDN_DV)
D_FF = 2816
ROPE_BASE = 10000.0
EPS = 1e-6
GN_EPS = 1e-5

C = 64
R = NH * C
LANES = 128

Z_RQ, Z_RK, Z_RV, Z_RG = 0, 256, 512, 768
Z_CONV = 1024
Z_DG = 2560
Z_LQ, Z_LK, Z_LV, Z_LG = 3072, 3200, 3328, 3584
Z_SM = 3840
NZ = 3968
SM_DA, SM_DB, SM_LLR = 0, 4, 8

NN = (((1,), (0,)), ((), ()))
NT = (((1,), (1,)), ((), ()))
TN = (((0,), (0,)), ((), ()))

VMEM_LIMIT = 56 * 1024 * 1024
FFN_TM = 512
FFN_TF = 256
MIX_TC = 512


def _mm(a, b, dims=NN):
    return lax.dot_general(a.astype(BF16), b.astype(BF16), dims, preferred_element_type=F32)


def _proj(h, w_rows):
    return lax.dot_general(h, w_rows, NT, preferred_element_type=F32)


def _split2(x):
    hi = x.astype(BF16)
    return hi, (x - hi.astype(F32)).astype(BF16)


def _mm_data_const(x, cb):
    hi, mid = _split2(x)
    m = x.shape[0]
    r = lax.dot_general(jnp.concatenate([hi, mid], axis=0), cb, NN, preferred_element_type=F32)
    return r[:m] + r[m:]


def _mm_const_data(cb, x):
    hi, mid = _split2(x)
    n = x.shape[1]
    r = lax.dot_general(cb, jnp.concatenate([hi, mid], axis=1), NN, preferred_element_type=F32)
    return r[:, :n] + r[:, n:]


def _silu(x):
    hx = 0.5 * x
    return hx + hx * jnp.tanh(hx)


def _softplus(x):
    return jnp.maximum(x, 0.0) + jnp.log1p(jnp.exp(-jnp.abs(x)))


def _log_sigmoid(x):
    return jnp.minimum(x, 0.0) - jnp.log1p(jnp.exp(-jnp.abs(x)))


def _rms_rows(x, w):
    ms = jnp.mean(x * x, axis=-1, keepdims=True)
    return x * lax.rsqrt(ms + EPS) * w


def _stack_bd(x, width):
    lane = lax.broadcasted_iota(jnp.int32, x.shape, 1) // width
    zero = jnp.zeros_like(x)
    return jnp.concatenate([jnp.where(lane == h, x, zero) for h in range(NH)], axis=0)


def _col_to_compact(col):
    lane = lax.broadcasted_iota(jnp.int32, (C, R), 1) // C
    out = jnp.zeros((C, R), F32)
    for h in range(NH):
        out = jnp.where(lane == h, col[h * C:(h + 1) * C], out)
    return out


def _ffn_kernel(*refs, first, last, npb, tf):
    n_x = 2 if first else 1
    xrefs, (nw_ref, w1_ref, w3_ref, w2_ref, fw_ref) = refs[:n_x], refs[n_x:n_x + 5]
    n_o = 2 if last else 1
    orefs = refs[n_x + 5:n_x + 5 + n_o]
    h_ref, acc_ref = refs[n_x + 5 + n_o:]
    i = pl.program_id(0)
    x = jnp.where(i < npb, xrefs[0][...], xrefs[1][...]) if first else xrefs[0][...]
    h_ref[...] = _rms_rows(x, nw_ref[...]).astype(BF16)
    for j in range(D_FF // tf):
        h = h_ref[...]
        a = jnp.dot(h, w1_ref[:, j * tf:(j + 1) * tf].astype(BF16), preferred_element_type=F32)
        g = jnp.dot(h, w3_ref[:, j * tf:(j + 1) * tf].astype(BF16), preferred_element_type=F32)
        p = jnp.dot((_silu(a) * g).astype(BF16), w2_ref[j * tf:(j + 1) * tf, :].astype(BF16),
                    preferred_element_type=F32)
        if j == 0:
            acc_ref[...] = p
        else:
            acc_ref[...] += p
    y = x + 0.5 * acc_ref[...]
    if last:
        y = _rms_rows(y, fw_ref[...])

        @pl.when(i < npb)
        def _():
            orefs[0][...] = y

        @pl.when(i >= npb)
        def _():
            orefs[1][...] = y
    else:
        orefs[0][...] = y


def _resident(a):
    nd = a.ndim
    return pl.BlockSpec(a.shape, lambda *_, _n=nd: (0,) * _n, pipeline_mode=pl.Buffered(1))


def _layer_resident(a, l):
    nd = a.ndim - 1
    return pl.BlockSpec((None,) + a.shape[1:], lambda *_, _n=nd: (l,) + (0,) * _n, pipeline_mode=pl.Buffered(1))


def _ffn(xs, l, nw, w1, w3, w2, fw, *, first, last, n_prompt, n_sample, tm=FFN_TM, tf=FFN_TF):
    assert n_sample == tm and n_prompt % tm == 0
    npb = n_prompt // tm
    prompt_blk = pl.BlockSpec((tm, D_MODEL), lambda i: (jnp.minimum(i, npb - 1), 0))
    sample_blk = pl.BlockSpec((tm, D_MODEL), lambda i: (0, 0))
    unified_blk = pl.BlockSpec((tm, D_MODEL), lambda i: (i, 0))
    if last:
        out_specs = (prompt_blk, sample_blk)
        out_shape = (jax.ShapeDtypeStruct((n_prompt, D_MODEL), F32), jax.ShapeDtypeStruct((n_sample, D_MODEL), F32))
    else:
        out_specs = unified_blk
        out_shape = jax.ShapeDtypeStruct((n_prompt + n_sample, D_MODEL), F32)
    return pl.pallas_call(
        functools.partial(_ffn_kernel, first=first, last=last, npb=npb, tf=tf),
        grid=(npb + 1,),
        in_specs=([prompt_blk, sample_blk] if first else [unified_blk]) + [
            _layer_resident(nw, l), _layer_resident(w1, l), _layer_resident(w3, l), _layer_resident(w2, l),
            _resident(fw)],
        out_specs=out_specs,
        out_shape=out_shape,
        scratch_shapes=[pltpu.VMEM((tm, D_MODEL), BF16), pltpu.VMEM((tm, D_MODEL), F32)],
        compiler_params=pltpu.CompilerParams(
            dimension_semantics=("arbitrary",), vmem_limit_bytes=VMEM_LIMIT),
        name="ffn",
    )(*xs, nw, w1, w3, w2, fw)


def _levels(L):
    return [s for s in (1, 2, 4, 8, 16, 32) if s < L]


def _chunk_consts(L):
    i = np.arange(C)
    sid, p = i // L, i % L
    same = sid[:, None] == sid[None, :]
    causal = same & (i[:, None] >= i[None, :])
    strict = same & (i[:, None] > i[None, :])
    lv = _levels(L)
    cum, pm, lm = [causal, same], [np.eye(C, dtype=bool)], []
    for s in lv:
        blk = p // s
        inblk = same & (blk[:, None] == blk[None, :])
        odd = (blk % 2 == 1)[:, None]
        cum.append(inblk & np.where(odd, i[None, :] <= i[:, None], i[None, :] > i[:, None]))
        pm.append(same & odd & (blk[None, :] == blk[:, None] - 1))
        lm.append(strict & ((p[:, None] // (2 * s)) == (p[None, :] // (2 * s))) & (blk[:, None] != blk[None, :]))
    r = np.arange(R)
    bdfull = (r[:, None] // C) == (r[None, :] // C)
    wide = lambda m: np.tile(m, (1, NH))
    f = lambda m: jnp.asarray(np.asarray(m, dtype=np.float32))
    gla_bd = (np.arange(NH * GLA_DK)[:, None] // GLA_DK) == (np.arange(NH * GLA_DV)[None, :] // GLA_DV)
    seqsel = np.zeros((C, LANES), np.float32)
    seqsel[i, sid] = 1.0
    return dict(
        cumg=f(np.concatenate(cum, axis=0)).astype(BF16),
        cumd=f(np.concatenate(cum[:2], axis=0)).astype(BF16),
        pmc=f(np.stack([wide(m) for m in pm])),
        lmc=f(np.stack([wide(m) for m in lm])),
        causal4=f(wide(causal)), eye4=f(wide(np.eye(C, dtype=bool))),
        g64=(f(bdfull) * (1.0 / 64.0)).astype(BF16), bdfull=f(bdfull), glabd=f(gla_bd),
        seqsel=jnp.asarray(seqsel).astype(BF16),
    )


def _ret_consts(L):
    log_gamma = jnp.log(1.0 - 2.0 ** (-5.0 - jnp.arange(NH, dtype=F32)))
    i = np.arange(C)
    sid, p = i // L, (i % L).astype(np.float32)
    causal = (sid[:, None] == sid[None, :]) & (i[:, None] >= i[None, :])
    diff = jnp.asarray(np.where(causal, p[:, None] - p[None, :], 0.0).astype(np.float32))
    dec = jnp.where(causal[None], jnp.exp(diff[None] * log_gamma[:, None, None]), 0.0)
    lg_l = jnp.repeat(log_gamma, RET_DV)[None, :]
    pj = jnp.asarray(p)[:, None]
    return dict(
        decc=jnp.concatenate([dec[h] for h in range(NH)], axis=1),
        dq=jnp.exp((pj + 1.0) * lg_l),
        dk=jnp.exp((L - 1.0 - pj) * lg_l),
        dch=jnp.exp(L * lg_l),
    )


def _rope_tables(pos):
    half = RET_DK // 2
    inv = ROPE_BASE ** (-jnp.arange(half, dtype=F32) / half)
    ang = pos.astype(F32)[:, None] * inv[None, :]
    cos, sin = jnp.cos(ang), jnp.sin(ang)
    return (jnp.tile(jnp.concatenate([cos, cos], axis=1), (1, NH)),
            jnp.tile(jnp.concatenate([-sin, sin], axis=1), (1, NH)))


def _rotary(x, cos, sin):
    lane = lax.broadcasted_iota(jnp.int32, x.shape, 1)
    swapped = jnp.where((lane % RET_DK) < RET_DK // 2,
                        pltpu.roll(x, x.shape[1] - RET_DK // 2, 1), pltpu.roll(x, RET_DK // 2, 1))
    return x * cos + swapped * sin


def _ret_intra_multi(zs, coss, sins, k):
    qs = [_rotary(z[:, Z_RQ:Z_RQ + 256], c, s) for z, c, s in zip(zs, coss, sins)]
    kks = [_rotary(z[:, Z_RK:Z_RK + 256], c, s) * (RET_DK ** -0.5) for z, c, s in zip(zs, coss, sins)]
    vs = [z[:, Z_RV:Z_RV + 256] for z in zs]
    kbds = [_stack_bd(kk.astype(BF16), RET_DK) for kk in kks]
    scs = [_mm(q, kbd, NT) * k["decc"][...] for q, kbd in zip(qs, kbds)]
    vbds = [_stack_bd(v.astype(BF16), RET_DV) for v in vs]
    os = [_mm(sc, vbd) for sc, vbd in zip(scs, vbds)]
    return [(o, q * k["dq"][...], kk * k["dk"][...], v) for o, q, kk, v in zip(os, qs, kks, vs)]


def _ret_intra(z, cos, sin, k):
    return _ret_intra_multi([z], [cos], [sin], k)[0]


def _ret_out(o, rg, k):
    g = k["g64"][...]
    d = o - _mm_data_const(o, g)
    var = _mm_data_const(d * d, g)
    return _silu(rg) * (d * lax.rsqrt(var + GN_EPS) * k["ret_gn_w"][...] + k["ret_gn_b"][...])


def _gla_intra_multi(zs, k, nlev):
    gks = [_log_sigmoid(_mm(z[:, Z_SM:Z_SM + LANES], k["gwp"][...]) + k["gla_gate_b"][...]) * (1.0 / GLA_GATE_NORM)
           for z in zs]
    qs = [z[:, Z_LQ:Z_LQ + 128] * (GLA_DK ** -0.5) for z in zs]
    kks = [z[:, Z_LK:Z_LK + 128] for z in zs]
    vs = [z[:, Z_LV:Z_LV + 256] for z in zs]
    css = [_mm_const_data(k["cumg"][...], gk) for gk in gks]
    scs = [_mm(q, _stack_bd(kk.astype(BF16), GLA_DK), NT) * k["pmc"][0] for q, kk in zip(qs, kks)]
    for li in range(nlev):
        es = [jnp.exp(cs[(2 + li) * C:(3 + li) * C]) for cs in css]
        scs = [sc + _mm(q * e, _stack_bd((kk * e).astype(BF16), GLA_DK), NT) * k["pmc"][li + 1]
               for sc, q, kk, e in zip(scs, qs, kks, es)]
    os = [_mm(sc, _stack_bd(v.astype(BF16), GLA_DV)) for sc, v in zip(scs, vs)]
    out = []
    for o, q, kk, v, cs in zip(os, qs, kks, vs, css):
        b, blast = cs[:C], cs[C:2 * C]
        out.append((o, q * jnp.exp(b), kk * jnp.exp(blast - b), v, b, blast))
    return out


def _gla_intra(z, k, nlev):
    return _gla_intra_multi([z], k, nlev)[0]


def _gla_out(o, lg, k):
    ms = _mm_data_const(o * o, k["g64"][...])
    return o * lax.rsqrt(ms + EPS) * k["gla_norm_w"][...] * _silu(lg)


def _lane_col(blk, lane0):
    lane = lax.broadcasted_iota(jnp.int32, blk.shape, 1)
    st = jnp.concatenate([jnp.where(lane == lane0 + h, blk, 0.0) for h in range(NH)], axis=0)
    return jnp.sum(st, axis=-1, keepdims=True)


def _gdn_front(conv, sm, k):
    def l2n(x):
        return x * lax.rsqrt(jnp.sum(x * x, axis=-1, keepdims=True) + EPS)

    qs = [l2n(conv[h]) * (GDN_DK ** -0.5) for h in range(NH)]
    ks = [l2n(conv[NH + h]) for h in range(NH)]
    vst = jnp.concatenate(conv[2 * NH:], axis=0)
    qst = jnp.concatenate(qs, axis=0)
    kst = jnp.concatenate(ks, axis=0)
    kq = []
    for p in range(NH // 2):
        kp = jnp.concatenate(ks[2 * p:2 * p + 2], axis=1).astype(BF16)
        lane = lax.broadcasted_iota(jnp.int32, kp.shape, 1) // GDN_DK
        kbd = jnp.concatenate([jnp.where(lane == hl, kp, jnp.zeros_like(kp)) for hl in range(2)], axis=0)
        lhs = jnp.concatenate([kp, jnp.concatenate(qs[2 * p:2 * p + 2], axis=1).astype(BF16)], axis=0)
        kq.append(lax.dot_general(lhs, kbd, NT, preferred_element_type=F32))
    kq = jnp.concatenate(kq, axis=1)
    gd = -jnp.exp(k["alog"][...]) * _softplus(sm + k["dtb"][...])
    beta = jax.nn.sigmoid(sm)
    cs = _mm_const_data(k["cumd"][...], gd)
    bcol = _lane_col(cs[:C], SM_DA)
    blcol = _lane_col(cs[C:], SM_DA)
    betacol = _lane_col(beta, SM_DB)
    bcolc = _col_to_compact(bcol)
    browc = jnp.sum(bcolc * k["eye4"][...], axis=0, keepdims=True)
    causal = k["causal4"][...] > 0.5
    decc = jnp.where(causal, jnp.exp(jnp.where(causal, bcolc - browc, 0.0)), 0.0)
    ac = kq[:C] * decc * _col_to_compact(betacol)
    qkc = kq[C:] * decc
    return dict(ac=ac, qkc=qkc, qst=qst, kst=kst, vst=vst, bcol=bcol, blcol=blcol, betacol=betacol)


def _gdn_inverse(acs, k, nlev):
    xs = [k["eye4"][...] - a * k["lmc"][0] for a in acs]
    for li in range(1, nlev):
        ms = [a * k["lmc"][li] for a in acs]
        ys = [_mm(m, _stack_bd(x.astype(BF16), C)) for m, x in zip(ms, xs)]
        xs = [x - _mm(x, _stack_bd(y.astype(BF16), C)) for x, y in zip(xs, ys)]
    return xs


def _gdn_back(f, xc):
    kb = f["kst"] * f["betacol"]
    rhs = jnp.concatenate([f["vst"] * f["betacol"], kb * jnp.exp(f["bcol"])], axis=1)
    sol = _mm(_stack_bd(xc.astype(BF16), C), rhs)
    return dict(qst=f["qst"], solv=sol[:, :GDN_DV], solk=sol[:, GDN_DV:],
                qk=_stack_bd(f["qkc"].astype(BF16), C), ebcol=jnp.exp(f["bcol"]),
                kdec=f["kst"] * jnp.exp(f["blcol"] - f["bcol"]), eblast=jnp.exp(f["blcol"]))


def _gdn_out(o, dg, k):
    w = k["gdn_norm_w"][...]
    y = jnp.concatenate([_rms_rows(o[:, h * GDN_DV:(h + 1) * GDN_DV], w) for h in range(NH)], axis=1)
    return y * _silu(dg)


_PARAM_NAMES = ("alog", "dtb", "gwp", "gla_gate_b", "ret_gn_w", "ret_gn_b", "gdn_norm_w", "gla_norm_w", "conv_w")
_CONST_NAMES = ("cumg", "cumd", "pmc", "lmc", "causal4", "eye4", "g64", "bdfull", "glabd", "seqsel",
                "decc", "dq", "dk", "dch")


def _mixer_prompt_kernel(*refs, tc, nlev):
    n_in = 6 + len(_PARAM_NAMES) + len(_CONST_NAMES)
    x_ref, nw_ref, win_ref, wout_ref, cos_ref, sin_ref = refs[:6]
    k = dict(zip(_PARAM_NAMES + _CONST_NAMES, refs[6:n_in]))
    o_ref, sret_ref, sgdn_ref, sgla_ref, conv_ref = refs[n_in:n_in + 5]
    z_ref, xp_ref, cv_ref, y_ref = refs[n_in + 5:]
    t = pl.program_id(1)
    nch = tc // C
    ng = CONV_DIM // LANES

    @pl.when(t == 0)
    def _():
        sret_ref[...] = jnp.zeros_like(sret_ref)
        sgdn_ref[...] = jnp.zeros_like(sgdn_ref)
        sgla_ref[...] = jnp.zeros_like(sgla_ref)
        xp_ref[:, pl.ds(0, 8), :] = jnp.zeros((ng, 8, LANES), F32)

    h = _rms_rows(x_ref[...], nw_ref[...]).astype(BF16)
    zc = _proj(h, win_ref[Z_CONV:Z_CONV + CONV_DIM, :])
    for g in range(ng):
        xp_ref[g, pl.ds(8, tc), :] = zc[:, g * LANES:(g + 1) * LANES]
    z_ref[:, :Z_CONV] = _proj(h, win_ref[:Z_CONV, :])
    z_ref[:, Z_DG:] = _proj(h, win_ref[Z_DG:, :])

    cw = k["conv_w"]
    nb = tc // 8
    for g in range(ng):
        wg = [cw[i:i + 1, g * LANES:(g + 1) * LANES] for i in range(CONV_W)]
        taps = {s: xp_ref[g, pl.ds(s, nb, stride=8), :] for s in range(5, 5 + 8 + CONV_W - 1)}
        for j in range(8):
            acc = taps[j + 8] * wg[3]
            for i in range(CONV_W - 1):
                acc = acc + taps[j + 5 + i] * wg[i]
            cv_ref[g, pl.ds(j, nb, stride=8), :] = _silu(acc)
        xp_ref[g, pl.ds(0, 8), :] = xp_ref[g, pl.ds(tc, 8), :]

    zs = [z_ref[pl.ds(c * C, C), :] for c in range(nch)]
    fronts = [_gdn_front([cv_ref[g, pl.ds(c * C, C), :] for g in range(ng)], z[:, Z_SM:Z_SM + LANES], k)
              for c, z in enumerate(zs)]
    xcs = _gdn_inverse([f["ac"] for f in fronts], k, nlev)
    gs = [_gdn_back(f, xc) for f, xc in zip(fronts, xcs)]
    rets = _ret_intra_multi(zs, [cos_ref[pl.ds(c * C, C), :] for c in range(nch)],
                            [sin_ref[pl.ds(c * C, C), :] for c in range(nch)], k)
    glas = _gla_intra_multi(zs, k, nlev)

    for c in range(nch):
        rows = pl.ds(c * C, C)

        o, qd, kd, v = rets[c]
        s = sret_ref[0]
        z_ref[rows, Z_RQ:Z_RQ + 256] = o + _mm(qd, s)
        sret_ref[0] = s * k["dch"][...] + _mm(kd, v, TN) * k["bdfull"][...]

        g = gs[c]
        us, qss = [], []
        for hh in range(NH):
            lhs = jnp.concatenate([g["solk"][hh * C:(hh + 1) * C], g["qst"][hh * C:(hh + 1) * C]], axis=0)
            r = _mm(lhs, sgdn_ref[0, hh])
            us.append(g["solv"][hh * C:(hh + 1) * C] - r[:C])
            qss.append(r[C:])
        ost = g["ebcol"] * jnp.concatenate(qss, axis=0) + _mm(g["qk"], jnp.concatenate(us, axis=0))
        for hh in range(NH):
            sl = slice(hh * C, (hh + 1) * C)
            sgdn_ref[0, hh] = (g["eblast"][hh * C:hh * C + 1] * sgdn_ref[0, hh]
                               + _mm(g["kdec"][sl], us[hh], TN))
        z_ref[rows, Z_CONV:Z_CONV + NH * GDN_DV] = jnp.concatenate(
            [ost[hh * C:(hh + 1) * C] for hh in range(NH)], axis=1)

        o, qe, ke, v, b, _ = glas[c]
        s = sgla_ref[0]
        z_ref[rows, Z_LQ:Z_LQ + 256] = o + _mm(qe, s)
        escale = jnp.exp(b[C - 8:, :].T[:, 7:8])
        sgla_ref[0] = s * escale + _mm(ke, v, TN) * k["glabd"][...]

    y_r = _ret_out(z_ref[:, Z_RQ:Z_RQ + 256], z_ref[:, Z_RG:Z_RG + 256], k)
    y_d = _gdn_out(z_ref[:, Z_CONV:Z_CONV + NH * GDN_DV], z_ref[:, Z_DG:Z_DG + 512], k)
    y_l = _gla_out(z_ref[:, Z_LQ:Z_LQ + 256], z_ref[:, Z_LG:Z_LG + 256], k)
    y_ref[...] = jnp.concatenate([y_r, y_d, y_l], axis=1).astype(BF16)

    o_ref[...] = x_ref[...] + jnp.dot(y_ref[...], wout_ref[...].astype(BF16), preferred_element_type=F32)

    @pl.when(t == pl.num_programs(1) - 1)
    def _():
        for g in range(ng):
            conv_ref[0, :, g * LANES:(g + 1) * LANES] = xp_ref[g, pl.ds(5, 3), :]


def _mixer_prompt(x, l, nw, win, wout, cos, sin, params, consts, *, bsz, tlen, tc=MIX_TC):
    nlev = len(_levels(C))
    nt = tlen // tc
    in_specs = [
        pl.BlockSpec((tc, D_MODEL), lambda b, t: (b * nt + t, 0)),
        _layer_resident(nw, l), _layer_resident(win, l), _layer_resident(wout, l),
        pl.BlockSpec((tc, 256), lambda b, t: (t, 0)),
        pl.BlockSpec((tc, 256), lambda b, t: (t, 0)),
    ] + [_layer_resident(params[n], l) for n in _PARAM_NAMES] + [_resident(consts[n]) for n in _CONST_NAMES]
    extras = [params[n] for n in _PARAM_NAMES] + [consts[n] for n in _CONST_NAMES]
    out_shape = (
        jax.ShapeDtypeStruct(x.shape, F32),
        jax.ShapeDtypeStruct((bsz, 256, 256), F32),
        jax.ShapeDtypeStruct((bsz, NH, GDN_DK, GDN_DV), F32),
        jax.ShapeDtypeStruct((bsz, NH * GLA_DK, NH * GLA_DV), F32),
        jax.ShapeDtypeStruct((bsz, CONV_W - 1, CONV_DIM), F32),
    )
    out_specs = (
        pl.BlockSpec((tc, D_MODEL), lambda b, t: (b * nt + t, 0)),
        pl.BlockSpec((1, 256, 256), lambda b, t: (b, 0, 0)),
        pl.BlockSpec((1, NH, GDN_DK, GDN_DV), lambda b, t: (b, 0, 0, 0)),
        pl.BlockSpec((1, NH * GLA_DK, NH * GLA_DV), lambda b, t: (b, 0, 0)),
        pl.BlockSpec((1, CONV_W - 1, CONV_DIM), lambda b, t: (b, 0, 0)),
    )
    return pl.pallas_call(
        functools.partial(_mixer_prompt_kernel, tc=tc, nlev=nlev),
        grid=(bsz, nt),
        in_specs=in_specs,
        out_specs=out_specs,
        out_shape=out_shape,
        input_output_aliases={0: 0},
        scratch_shapes=[
            pltpu.VMEM((tc, NZ), F32),
            pltpu.VMEM((CONV_DIM // LANES, tc + 8, LANES), F32),
            pltpu.VMEM((CONV_DIM // LANES, tc, LANES), F32),
            pltpu.VMEM((tc, D_MODEL), BF16),
        ],
        compiler_params=pltpu.CompilerParams(
            dimension_semantics=("parallel", "arbitrary"), vmem_limit_bytes=VMEM_LIMIT),
        name="mixer_prompt",
    )(x, nw, win, wout, cos, sin, *extras)


SEQ_S = 4
NSEQ = C // SEQ_S


def _inproj_kernel(x_ref, nw_ref, win_ref, z_ref):
    z_ref[...] = _proj(_rms_rows(x_ref[...], nw_ref[...]).astype(BF16), win_ref[...])


def _sample_inproj(x, l, nw, win, *, row0, n):
    return pl.pallas_call(
        _inproj_kernel,
        grid=(1,),
        in_specs=[pl.BlockSpec((n, D_MODEL), lambda i: (row0 // n, 0)), _layer_resident(nw, l),
                  _layer_resident(win, l)],
        out_specs=pl.BlockSpec((n, NZ), lambda i: (0, 0)),
        out_shape=jax.ShapeDtypeStruct((n, NZ), F32),
        compiler_params=pltpu.CompilerParams(
            dimension_semantics=("arbitrary",), vmem_limit_bytes=VMEM_LIMIT),
        name="sample_inproj",
    )(x, nw, win)


def _mixer_sample_kernel(*refs, nlev):
    n_in = 9 + len(_PARAM_NAMES) + len(_CONST_NAMES)
    (x_ref, zin_ref, wout_ref, cos_ref, sin_ref,
     sret_in, sgdn_in, sgla_in, cbuf_ref) = refs[:9]
    k = dict(zip(_PARAM_NAMES + _CONST_NAMES, refs[9:n_in]))
    o_ref, sret_out, sgdn_out, sgla_out, cout_ref = refs[n_in + 3:n_in + 8]

    x = x_ref[...]
    z = zin_ref[...]

    rowi = lax.broadcasted_iota(jnp.int32, (C, 1), 0)
    tpos = rowi % SEQ_S

    xc = z[:, Z_CONV:Z_CONV + CONV_DIM]
    cb = cbuf_ref[...]
    cw = k["conv_w"]
    acc = xc * cw[3:4, :]
    for i in range(CONV_W - 1):
        cur = pltpu.roll(xc, 3 - i, 0)
        old = cb if i == 0 else pltpu.roll(cb, C - i, 0)
        acc = acc + jnp.where(tpos + i >= 3, cur, old) * cw[i:i + 1, :]
    cout_ref[...] = pltpu.roll(xc, C - 1, 0)
    conv = _silu(acc)

    seq_of_row = rowi // SEQ_S
    seq_of_lane = lax.broadcasted_iota(jnp.int32, (1, C), 1) // SEQ_S

    o, qd, kd, v = _ret_intra(z, cos_ref[...], sin_ref[...], k)
    dch = k["dch"][...]
    kdt = kd.T
    accs = [jnp.zeros((C, RET_DV), F32) for _ in range(NH)]
    for n in range(NSEQ):
        rm = seq_of_row == n
        lm = seq_of_lane == n
        for hh in range(NH):
            sl = slice(hh * RET_DK, (hh + 1) * RET_DK)
            s = sret_in[n, hh]
            accs[hh] = jnp.where(rm, _mm(qd[:, sl], s), accs[hh])
            sret_out[n, hh] = s * dch[:, sl] + _mm(jnp.where(lm, kdt[sl], 0.0), v[:, sl])
    o = o + jnp.concatenate(accs, axis=1)
    y_r = _ret_out(o, z[:, Z_RG:Z_RG + 256], k)

    f = _gdn_front([conv[:, gi * LANES:(gi + 1) * LANES] for gi in range(CONV_DIM // LANES)],
                   z[:, Z_SM:Z_SM + LANES], k)
    g = _gdn_back(f, _gdn_inverse([f["ac"]], k, nlev)[0])
    rowi2 = lax.broadcasted_iota(jnp.int32, (2 * C, 1), 0)
    seq_of_row2 = (rowi2 % C) // SEQ_S
    lhs = [jnp.concatenate([g["solk"][hh * C:(hh + 1) * C], g["qst"][hh * C:(hh + 1) * C]], axis=0)
           for hh in range(NH)]
    accs = [jnp.zeros((2 * C, GDN_DV), F32) for _ in range(NH)]
    for n in range(NSEQ):
        rm2 = seq_of_row2 == n
        for hh in range(NH):
            accs[hh] = jnp.where(rm2, _mm(lhs[hh], sgdn_in[n, hh]), accs[hh])
    us = [g["solv"][hh * C:(hh + 1) * C] - accs[hh][:C] for hh in range(NH)]
    ost = (g["ebcol"] * jnp.concatenate([accs[hh][C:] for hh in range(NH)], axis=0)
           + _mm(g["qk"], jnp.concatenate(us, axis=0)))
    kdts = [g["kdec"][hh * C:(hh + 1) * C].T for hh in range(NH)]
    for n in range(NSEQ):
        lm = seq_of_lane == n
        for hh in range(NH):
            scale = g["eblast"][hh * C + n * SEQ_S:hh * C + n * SEQ_S + 1]
            sgdn_out[n, hh] = scale * sgdn_in[n, hh] + _mm(jnp.where(lm, kdts[hh], 0.0), us[hh])
    y_d = _gdn_out(jnp.concatenate([ost[hh * C:(hh + 1) * C] for hh in range(NH)], axis=1),
                   z[:, Z_DG:Z_DG + 512], k)

    o, qe, ke, v, b, blast = _gla_intra(z, k, nlev)
    ket = ke.T
    hi, mid = _split2(blast * (1.0 / SEQ_S))
    blt2 = lax.dot_general(jnp.concatenate([hi, mid], axis=1), k["seqsel"][...], TN, preferred_element_type=F32)
    eblt = jnp.exp(blt2[:NH * GLA_DK] + blt2[NH * GLA_DK:])
    accs = [jnp.zeros((C, GLA_DV), F32) for _ in range(NH)]
    for n in range(NSEQ):
        rm = seq_of_row == n
        lm = seq_of_lane == n
        for hh in range(NH):
            sk = slice(hh * GLA_DK, (hh + 1) * GLA_DK)
            sv = slice(hh * GLA_DV, (hh + 1) * GLA_DV)
            s = sgla_in[n, hh]
            accs[hh] = jnp.where(rm, _mm(qe[:, sk], s), accs[hh])
            sgla_out[n, hh] = s * eblt[sk, n:n + 1] + _mm(jnp.where(lm, ket[sk], 0.0), v[:, sv])
    o = o + jnp.concatenate(accs, axis=1)
    y_l = _gla_out(o, z[:, Z_LG:Z_LG + 256], k)

    y = jnp.concatenate([y_r, y_d, y_l], axis=1).astype(BF16)
    o_ref[...] = x + jnp.dot(y, wout_ref[...].astype(BF16), preferred_element_type=F32)


def _mixer_sample(x, l, nw, win, wout, cos, sin, sret, sgdn, sgla, cbuf, acc_ret, acc_gdn, acc_gla,
                  params, consts, *, row0):
    nb = sret.shape[1]
    n = nb * SEQ_S
    nlev = len(_levels(SEQ_S))
    blk0 = row0 // C

    def st_spec(dk, dv):
        return pl.BlockSpec((None, NSEQ, NH, dk, dv), lambda i: (l, i, 0, 0, 0))

    st_specs = [st_spec(RET_DK, RET_DV), st_spec(GDN_DK, GDN_DV), st_spec(GLA_DK, GLA_DV)]
    any_spec = pl.BlockSpec(memory_space=pl.ANY)
    z = _sample_inproj(x, l, nw, win, row0=row0, n=n)
    in_specs = [
        pl.BlockSpec((C, D_MODEL), lambda i: (blk0 + i, 0)),
        pl.BlockSpec((C, NZ), lambda i: (i, 0)),
        _layer_resident(wout, l), _resident(cos), _resident(sin),
    ] + st_specs + [pl.BlockSpec((C, CONV_DIM), lambda i: (i, 0))] + [
        _layer_resident(params[nm], l) for nm in _PARAM_NAMES] + [
        _resident(consts[nm]) for nm in _CONST_NAMES] + [any_spec] * 3
    extras = [params[nm] for nm in _PARAM_NAMES] + [consts[nm] for nm in _CONST_NAMES]
    n_in = len(in_specs)
    out_shape = (
        jax.ShapeDtypeStruct(x.shape, F32),
        jax.ShapeDtypeStruct(acc_ret.shape, F32),
        jax.ShapeDtypeStruct(acc_gdn.shape, F32),
        jax.ShapeDtypeStruct(acc_gla.shape, F32),
        jax.ShapeDtypeStruct((n, CONV_DIM), F32),
    )
    out_specs = (pl.BlockSpec((C, D_MODEL), lambda i: (blk0 + i, 0)),) + tuple(st_specs) + (
        pl.BlockSpec((C, CONV_DIM), lambda i: (i, 0)),)
    return pl.pallas_call(
        functools.partial(_mixer_sample_kernel, nlev=nlev),
        grid=(n // C,),
        in_specs=in_specs,
        out_specs=out_specs,
        out_shape=out_shape,
        input_output_aliases={0: 0, n_in - 3: 1, n_in - 2: 2, n_in - 1: 3},
        compiler_params=pltpu.CompilerParams(
            dimension_semantics=("parallel",), vmem_limit_bytes=VMEM_LIMIT),
        name="mixer_sample",
    )(x, z, wout, cos, sin, sret, sgdn, sgla, cbuf, *extras, acc_ret, acc_gdn, acc_gla)


def _permute_w_in(w):
    wt = jnp.swapaxes(w, 1, 2)
    pad = jnp.zeros((w.shape[0], NZ - Z_SM - 24, w.shape[1]), w.dtype)
    return jnp.concatenate(
        [wt[:, 0:2560], wt[:, 2568:3080], wt[:, 3080:3592], wt[:, 3608:3864], wt[:, 2560:2568],
         wt[:, 3592:3608], pad], axis=1).astype(BF16)


def _rows(v, width=None):
    v = v.astype(F32)[:, None, :]
    if width is not None and v.shape[-1] < width:
        v = jnp.pad(v, ((0, 0), (0, 0), (0, width - v.shape[-1])))
    return v


def kernel(x_prompt, x_sample, state_ret, state_gdn, state_gdn_conv, state_gla, norm_ffn1, ffn1_w1, ffn1_w3, ffn1_w2, norm_mix, w_in, ret_gn_w, ret_gn_b, gdn_conv_w, gdn_A_log, gdn_dt_bias, gdn_norm_w, gla_gate_w, gla_gate_b, gla_norm_w, w_out, norm_ffn2, ffn2_w1, ffn2_w3, ffn2_w2, norm_final):
    bp, tp, _ = x_prompt.shape
    bs, ts, _ = x_sample.shape
    n_prompt, n_sample = bp * tp, bs * ts
    assert ts == SEQ_S and tp % MIX_TC == 0 and n_sample % C == 0 and n_prompt % C == 0

    consts_p = dict(_chunk_consts(C), **_ret_consts(C))
    consts_s = dict(_chunk_consts(SEQ_S), **_ret_consts(SEQ_S))
    cos_p, sin_p = _rope_tables(jnp.arange(tp, dtype=jnp.int32))
    cos_s, sin_s = _rope_tables(PAST_LEN + (jnp.arange(C, dtype=jnp.int32) % SEQ_S))

    params = dict(
        alog=_rows(gdn_A_log, LANES), dtb=_rows(gdn_dt_bias, LANES),
        gwp=jnp.pad(gla_gate_w, ((0, 0), (SM_LLR, LANES - SM_LLR - GLA_RANK), (0, 0))).astype(BF16),
        gla_gate_b=_rows(gla_gate_b), ret_gn_w=_rows(ret_gn_w), ret_gn_b=_rows(ret_gn_b),
        gdn_norm_w=_rows(gdn_norm_w), gla_norm_w=_rows(jnp.tile(gla_norm_w, (1, NH))),
        conv_w=gdn_conv_w.astype(F32),
    )
    win = _permute_w_in(w_in)
    wout = w_out.astype(F32)
    f1 = (_rows(norm_ffn1), ffn1_w1.astype(F32), ffn1_w3.astype(F32), ffn1_w2.astype(F32))
    f2 = (_rows(norm_ffn2), ffn2_w1.astype(F32), ffn2_w3.astype(F32), ffn2_w2.astype(F32))
    nw = _rows(norm_mix)
    fw = norm_final.astype(F32)[None, :]
    sret_in, sgdn_in, sgla_in = state_ret.astype(F32), state_gdn.astype(F32), state_gla.astype(F32)
    cbuf = jnp.pad(state_gdn_conv.astype(F32), ((0, 0), (0, 0), (0, 1), (0, 0))).reshape(DEPTH, n_sample, CONV_DIM)
    acc_ret, acc_gdn, acc_gla = (lax.empty(s.shape, F32) for s in (sret_in, sgdn_in, sgla_in))

    ffn = functools.partial(_ffn, n_prompt=n_prompt, n_sample=n_sample)
    xs = (x_prompt.reshape(n_prompt, D_MODEL), x_sample.reshape(n_sample, D_MODEL))
    outs_p = [[] for _ in range(4)]
    conv_s = []
    for l in range(DEPTH):
        x = ffn(xs, l, *f1, fw, first=(l == 0), last=False)
        x, sret, sgdn, sgla, conv = _mixer_prompt(x, l, nw, win, wout, cos_p, sin_p, params, consts_p,
                                                  bsz=bp, tlen=tp)
        outs_p[0].append(jnp.stack([sret[:, h * 64:(h + 1) * 64, h * 64:(h + 1) * 64] for h in range(NH)], axis=1))
        outs_p[1].append(sgdn)
        outs_p[2].append(conv)
        outs_p[3].append(jnp.stack([sgla[:, h * 32:(h + 1) * 32, h * 64:(h + 1) * 64] for h in range(NH)], axis=1))
        x, acc_ret, acc_gdn, acc_gla, cout = _mixer_sample(
            x, l, nw, win, wout, cos_s, sin_s, sret_in, sgdn_in, sgla_in, cbuf[l], acc_ret, acc_gdn, acc_gla,
            params, consts_s, row0=n_prompt)
        conv_s.append(cout.reshape(bs, ts, CONV_DIM)[:, :CONV_W - 1])
        xs = (ffn((x,), l, *f2, fw, first=False, last=(l == DEPTH - 1)),)

    y_prompt, y_sample = xs[0]
    dts = (state_ret.dtype, state_gdn.dtype, state_gdn_conv.dtype, state_gla.dtype)
    sp = [jnp.stack(o).astype(d) for o, d in zip(outs_p, dts)]
    return (y_prompt.reshape(bp, tp, D_MODEL), y_sample.reshape(bs, ts, D_MODEL), sp[0], sp[1], sp[2], sp[3],
            acc_ret.astype(dts[0]), acc_gdn.astype(dts[1]), jnp.stack(conv_s).astype(dts[2]), acc_gla.astype(dts[3]))
```

```python
import functools

import numpy as np
import jax
import jax.numpy as jnp
from jax import lax
from jax.experimental import pallas as pl
from jax.experimental.pallas import tpu as pltpu

F32 = jnp.float32
BF16 = jnp.bfloat16

D_MODEL = 1024
DEPTH = 4
PAST_LEN = 16384
NH = 4
RET_DK = 64
RET_DV = 64
GDN_DK = 128
GDN_DV = 128
GLA_DK = 32
GLA_DV = 64
GLA_RANK = 16
GLA_GATE_NORM = 16.0
CONV_W = 4
CONV_DIM = NH * (2 * GDN_DK + GDN_DV)
D_FF = 2816
ROPE_BASE = 10000.0
EPS = 1e-6
GN_EPS = 1e-5

C = 64
R = NH * C
LANES = 128

Z_RQ, Z_RK, Z_RV, Z_RG = 0, 256, 512, 768
Z_CONV = 1024
Z_DG = 2560
Z_LQ, Z_LK, Z_LV, Z_LG = 3072, 3200, 3328, 3584
Z_SM = 3840
NZ = 3968
SM_DA, SM_DB, SM_LLR = 0, 4, 8

NN = (((1,), (0,)), ((), ()))
NT = (((1,), (1,)), ((), ()))
TN = (((0,), (0,)), ((), ()))

VMEM_LIMIT = 56 * 1024 * 1024
FFN_TM = 512
FFN_TF = 256
MIX_TC = 512


def _mm(a, b, dims=NN):
    return lax.dot_general(a.astype(BF16), b.astype(BF16), dims, preferred_element_type=F32)


def _proj(h, w_rows):
    return lax.dot_general(h, w_rows, NT, preferred_element_type=F32)


def _split2(x):
    hi = x.astype(BF16)
    return hi, (x - hi.astype(F32)).astype(BF16)


def _mm_data_const(x, cb):
    hi, mid = _split2(x)
    m = x.shape[0]
    r = lax.dot_general(jnp.concatenate([hi, mid], axis=0), cb, NN, preferred_element_type=F32)
    return r[:m] + r[m:]


def _mm_const_data(cb, x):
    hi, mid = _split2(x)
    n = x.shape[1]
    r = lax.dot_general(cb, jnp.concatenate([hi, mid], axis=1), NN, preferred_element_type=F32)
    return r[:, :n] + r[:, n:]


def _silu(x):
    hx = 0.5 * x
    return hx + hx * jnp.tanh(hx)


def _softplus(x):
    return jnp.maximum(x, 0.0) + jnp.log1p(jnp.exp(-jnp.abs(x)))


def _log_sigmoid(x):
    return jnp.minimum(x, 0.0) - jnp.log1p(jnp.exp(-jnp.abs(x)))


def _rms_rows(x, w):
    ms = jnp.mean(x * x, axis=-1, keepdims=True)
    return x * lax.rsqrt(ms + EPS) * w


def _stack_bd(x, width):
    lane = lax.broadcasted_iota(jnp.int32, x.shape, 1) // width
    zero = jnp.zeros_like(x)
    return jnp.concatenate([jnp.where(lane == h, x, zero) for h in range(NH)], axis=0)


def _col_to_compact(col):
    lane = lax.broadcasted_iota(jnp.int32, (C, R), 1) // C
    out = jnp.zeros((C, R), F32)
    for h in range(NH):
        out = jnp.where(lane == h, col[h * C:(h + 1) * C], out)
    return out


def _ffn_kernel(*refs, first, last, npb, tf):
    n_x = 2 if first else 1
    xrefs, (nw_ref, w1_ref, w3_ref, w2_ref, fw_ref) = refs[:n_x], refs[n_x:n_x + 5]
    n_o = 2 if last else 1
    orefs = refs[n_x + 5:n_x + 5 + n_o]
    h_ref, acc_ref = refs[n_x + 5 + n_o:]
    i = pl.program_id(0)
    x = jnp.where(i < npb, xrefs[0][...], xrefs[1][...]) if first else xrefs[0][...]
    h_ref[...] = _rms_rows(x, nw_ref[...]).astype(BF16)
    for j in range(D_FF // tf):
        h = h_ref[...]
        a = jnp.dot(h, w1_ref[:, j * tf:(j + 1) * tf].astype(BF16), preferred_element_type=F32)
        g = jnp.dot(h, w3_ref[:, j * tf:(j + 1) * tf].astype(BF16), preferred_element_type=F32)
        p = jnp.dot((_silu(a) * g).astype(BF16), w2_ref[j * tf:(j + 1) * tf, :].astype(BF16),
                    preferred_element_type=F32)
        if j == 0:
            acc_ref[...] = p
        else:
            acc_ref[...] += p
    y = x + 0.5 * acc_ref[...]
    if last:
        y = _rms_rows(y, fw_ref[...])

        @pl.when(i < npb)
        def _():
            orefs[0][...] = y

        @pl.when(i >= npb)
        def _():
            orefs[1][...] = y
    else:
        orefs[0][...] = y


def _resident(a):
    nd = a.ndim
    return pl.BlockSpec(a.shape, lambda *_, _n=nd: (0,) * _n, pipeline_mode=pl.Buffered(1))


def _layer_resident(a, l):
    nd = a.ndim - 1
    return pl.BlockSpec((None,) + a.shape[1:], lambda *_, _n=nd: (l,) + (0,) * _n, pipeline_mode=pl.Buffered(1))


def _ffn(xs, l, nw, w1, w3, w2, fw, *, first, last, n_prompt, n_sample, tm=FFN_TM, tf=FFN_TF):
    assert n_sample == tm and n_prompt % tm == 0
    npb = n_prompt // tm
    prompt_blk = pl.BlockSpec((tm, D_MODEL), lambda i: (jnp.minimum(i, npb - 1), 0))
    sample_blk = pl.BlockSpec((tm, D_MODEL), lambda i: (0, 0))
    unified_blk = pl.BlockSpec((tm, D_MODEL), lambda i: (i, 0))
    if last:
        out_specs = (prompt_blk, sample_blk)
        out_shape = (jax.ShapeDtypeStruct((n_prompt, D_MODEL), F32), jax.ShapeDtypeStruct((n_sample, D_MODEL), F32))
    else:
        out_specs = unified_blk
        out_shape = jax.ShapeDtypeStruct((n_prompt + n_sample, D_MODEL), F32)
    return pl.pallas_call(
        functools.partial(_ffn_kernel, first=first, last=last, npb=npb, tf=tf),
        grid=(npb + 1,),
        in_specs=([prompt_blk, sample_blk] if first else [unified_blk]) + [
            _layer_resident(nw, l), _layer_resident(w1, l), _layer_resident(w3, l), _layer_resident(w2, l),
            _resident(fw)],
        out_specs=out_specs,
        out_shape=out_shape,
        scratch_shapes=[pltpu.VMEM((tm, D_MODEL), BF16), pltpu.VMEM((tm, D_MODEL), F32)],
        compiler_params=pltpu.CompilerParams(
            dimension_semantics=("arbitrary",), vmem_limit_bytes=VMEM_LIMIT),
        name="ffn",
    )(*xs, nw, w1, w3, w2, fw)


def _levels(L):
    return [s for s in (1, 2, 4, 8, 16, 32) if s < L]


def _chunk_consts(L):
    i = np.arange(C)
    sid, p = i // L, i % L
    same = sid[:, None] == sid[None, :]
    causal = same & (i[:, None] >= i[None, :])
    strict = same & (i[:, None] > i[None, :])
    lv = _levels(L)
    cum, pm, lm = [causal, same], [np.eye(C, dtype=bool)], []
    for s in lv:
        blk = p // s
        inblk = same & (blk[:, None] == blk[None, :])
        odd = (blk % 2 == 1)[:, None]
        cum.append(inblk & np.where(odd, i[None, :] <= i[:, None], i[None, :] > i[:, None]))
        pm.append(same & odd & (blk[None, :] == blk[:, None] - 1))
        lm.append(strict & ((p[:, None] // (2 * s)) == (p[None, :] // (2 * s))) & (blk[:, None] != blk[None, :]))
    r = np.arange(R)
    bdfull = (r[:, None] // C) == (r[None, :] // C)
    wide = lambda m: np.tile(m, (1, NH))
    f = lambda m: jnp.asarray(np.asarray(m, dtype=np.float32))
    gla_bd = (np.arange(NH * GLA_DK)[:, None] // GLA_DK) == (np.arange(NH * GLA_DV)[None, :] // GLA_DV)
    seqsel = np.zeros((C, LANES), np.float32)
    seqsel[i, sid] = 1.0
    return dict(
        cumg=f(np.concatenate(cum, axis=0)).astype(BF16),
        cumd=f(np.concatenate(cum[:2], axis=0)).astype(BF16),
        pmc=f(np.stack([wide(m) for m in pm])),
        lmc=f(np.stack([wide(m) for m in lm])),
        causal4=f(wide(causal)), eye4=f(wide(np.eye(C, dtype=bool))),
        g64=(f(bdfull) * (1.0 / 64.0)).astype(BF16), bdfull=f(bdfull), glabd=f(gla_bd),
        seqsel=jnp.asarray(seqsel).astype(BF16),
    )


def _ret_consts(L):
    log_gamma = jnp.log(1.0 - 2.0 ** (-5.0 - jnp.arange(NH, dtype=F32)))
    i = np.arange(C)
    sid, p = i // L, (i % L).astype(np.float32)
    causal = (sid[:, None] == sid[None, :]) & (i[:, None] >= i[None, :])
    diff = jnp.asarray(np.where(causal, p[:, None] - p[None, :], 0.0).astype(np.float32))
    dec = jnp.where(causal[None], jnp.exp(diff[None] * log_gamma[:, None, None]), 0.0)
    lg_l = jnp.repeat(log_gamma, RET_DV)[None, :]
    pj = jnp.asarray(p)[:, None]
    return dict(
        decc=jnp.concatenate([dec[h] for h in range(NH)], axis=1),
        dq=jnp.exp((pj + 1.0) * lg_l),
        dk=jnp.exp((L - 1.0 - pj) * lg_l),
        dch=jnp.exp(L * lg_l),
    )


def _rope_tables(pos):
    half = RET_DK // 2
    inv = ROPE_BASE ** (-jnp.arange(half, dtype=F32) / half)
    ang = pos.astype(F32)[:, None] * inv[None, :]
    cos, sin = jnp.cos(ang), jnp.sin(ang)
    return (jnp.tile(jnp.concatenate([cos, cos], axis=1), (1, NH)),
            jnp.tile(jnp.concatenate([-sin, sin], axis=1), (1, NH)))


def _rotary(x, cos, sin):
    lane = lax.broadcasted_iota(jnp.int32, x.shape, 1)
    swapped = jnp.where((lane % RET_DK) < RET_DK // 2,
                        pltpu.roll(x, x.shape[1] - RET_DK // 2, 1), pltpu.roll(x, RET_DK // 2, 1))
    return x * cos + swapped * sin


def _ret_intra_multi(zs, coss, sins, k):
    qs = [_rotary(z[:, Z_RQ:Z_RQ + 256], c, s) for z, c, s in zip(zs, coss, sins)]
    kks = [_rotary(z[:, Z_RK:Z_RK + 256], c, s) * (RET_DK ** -0.5) for z, c, s in zip(zs, coss, sins)]
    vs = [z[:, Z_RV:Z_RV + 256] for z in zs]
    kbds = [_stack_bd(kk.astype(BF16), RET_DK) for kk in kks]
    scs = [_mm(q, kbd, NT) * k["decc"][...] for q, kbd in zip(qs, kbds)]
    vbds = [_stack_bd(v.astype(BF16), RET_DV) for v in vs]
    os = [_mm(sc, vbd) for sc, vbd in zip(scs, vbds)]
    return [(o, q * k["dq"][...], kk * k["dk"][...], v) for o, q, kk, v in zip(os, qs, kks, vs)]


def _ret_intra(z, cos, sin, k):
    return _ret_intra_multi([z], [cos], [sin], k)[0]


def _ret_out(o, rg, k):
    g = k["g64"][...]
    d = o - _mm_data_const(o, g)
    var = _mm_data_const(d * d, g)
    return _silu(rg) * (d * lax.rsqrt(var + GN_EPS) * k["ret_gn_w"][...] + k["ret_gn_b"][...])


def _gla_intra_multi(zs, k, nlev):
    gks = [_log_sigmoid(_mm(z[:, Z_SM:Z_SM + LANES], k["gwp"][...]) + k["gla_gate_b"][...]) * (1.0 / GLA_GATE_NORM)
           for z in zs]
    qs = [z[:, Z_LQ:Z_LQ + 128] * (GLA_DK ** -0.5) for z in zs]
    kks = [z[:, Z_LK:Z_LK + 128] for z in zs]
    vs = [z[:, Z_LV:Z_LV + 256] for z in zs]
    css = [_mm_const_data(k["cumg"][...], gk) for gk in gks]
    scs = [_mm(q, _stack_bd(kk.astype(BF16), GLA_DK), NT) * k["pmc"][0] for q, kk in zip(qs, kks)]
    for li in range(nlev):
        es = [jnp.exp(cs[(2 + li) * C:(3 + li) * C]) for cs in css]
        scs = [sc + _mm(q * e, _stack_bd((kk * e).astype(BF16), GLA_DK), NT) * k["pmc"][li + 1]
               for sc, q, kk, e in zip(scs, qs, kks, es)]
    os = [_mm(sc, _stack_bd(v.astype(BF16), GLA_DV)) for sc, v in zip(scs, vs)]
    out = []
    for o, q, kk, v, cs in zip(os, qs, kks, vs, css):
        b, blast = cs[:C], cs[C:2 * C]
        out.append((o, q * jnp.exp(b), kk * jnp.exp(blast - b), v, b, blast))
    return out


def _gla_intra(z, k, nlev):
    return _gla_intra_multi([z], k, nlev)[0]


def _gla_out(o, lg, k):
    ms = _mm_data_const(o * o, k["g64"][...])
    return o * lax.rsqrt(ms + EPS) * k["gla_norm_w"][...] * _silu(lg)


def _lane_col(blk, lane0):
    lane = lax.broadcasted_iota(jnp.int32, blk.shape, 1)
    st = jnp.concatenate([jnp.where(lane == lane0 + h, blk, 0.0) for h in range(NH)], axis=0)
    return jnp.sum(st, axis=-1, keepdims=True)


def _gdn_front_multi(convs, sms, k):
    n = len(convs)

    def l2n(x):
        return x * lax.rsqrt(jnp.sum(x * x, axis=-1, keepdims=True) + EPS)

    qss = [[l2n(conv[h]) * (GDN_DK ** -0.5) for h in range(NH)] for conv in convs]
    kss = [[l2n(conv[NH + h]) for h in range(NH)] for conv in convs]
    operands = []
    for qs, ks in zip(qss, kss):
        pairs = []
        for p in range(NH // 2):
            kp = jnp.concatenate(ks[2 * p:2 * p + 2], axis=1).astype(BF16)
            lane = lax.broadcasted_iota(jnp.int32, kp.shape, 1) // GDN_DK
            kbd = jnp.concatenate([jnp.where(lane == hl, kp, jnp.zeros_like(kp)) for hl in range(2)], axis=0)
            lhs = jnp.concatenate([kp, jnp.concatenate(qs[2 * p:2 * p + 2], axis=1).astype(BF16)], axis=0)
            pairs.append((lhs, kbd))
        operands.append(pairs)
    kqs = [[lax.dot_general(operands[c][p][0], operands[c][p][1], NT, preferred_element_type=F32)
            for c in range(n)] for p in range(NH // 2)]
    gds = [-jnp.exp(k["alog"][...]) * _softplus(sm + k["dtb"][...]) for sm in sms]
    css = [_mm_const_data(k["cumd"][...], gd) for gd in gds]
    out = []
    for c in range(n):
        kq = jnp.concatenate([kqs[p][c] for p in range(NH // 2)], axis=1)
        beta = jax.nn.sigmoid(sms[c])
        bcol = _lane_col(css[c][:C], SM_DA)
        blcol = _lane_col(css[c][C:], SM_DA)
        betacol = _lane_col(beta, SM_DB)
        bcolc = _col_to_compact(bcol)
        browc = jnp.sum(bcolc * k["eye4"][...], axis=0, keepdims=True)
        causal = k["causal4"][...] > 0.5
        decc = jnp.where(causal, jnp.exp(jnp.where(causal, bcolc - browc, 0.0)), 0.0)
        out.append(dict(ac=kq[:C] * decc * _col_to_compact(betacol), qkc=kq[C:] * decc,
                        qst=jnp.concatenate(qss[c], axis=0), kst=jnp.concatenate(kss[c], axis=0),
                        vst=jnp.concatenate(convs[c][2 * NH:], axis=0),
                        bcol=bcol, blcol=blcol, betacol=betacol))
    return out


def _gdn_front(conv, sm, k):
    return _gdn_front_multi([conv], [sm], k)[0]


def _gdn_inverse(acs, k, nlev):
    xs = [k["eye4"][...] - a * k["lmc"][0] for a in acs]
    for li in range(1, nlev):
        ms = [a * k["lmc"][li] for a in acs]
        ys = [_mm(m, _stack_bd(x.astype(BF16), C)) for m, x in zip(ms, xs)]
        xs = [x - _mm(x, _stack_bd(y.astype(BF16), C)) for x, y in zip(xs, ys)]
    return xs


def _gdn_back(f, xc):
    kb = f["kst"] * f["betacol"]
    rhs = jnp.concatenate([f["vst"] * f["betacol"], kb * jnp.exp(f["bcol"])], axis=1)
    sol = _mm(_stack_bd(xc.astype(BF16), C), rhs)
    return dict(qst=f["qst"], solv=sol[:, :GDN_DV], solk=sol[:, GDN_DV:],
                qk=_stack_bd(f["qkc"].astype(BF16), C), ebcol=jnp.exp(f["bcol"]),
                kdec=f["kst"] * jnp.exp(f["blcol"] - f["bcol"]), eblast=jnp.exp(f["blcol"]))


def _gdn_out(o, dg, k):
    w = k["gdn_norm_w"][...]
    y = jnp.concatenate([_rms_rows(o[:, h * GDN_DV:(h + 1) * GDN_DV], w) for h in range(NH)], axis=1)
    return y * _silu(dg)


_PARAM_NAMES = ("alog", "dtb", "gwp", "gla_gate_b", "ret_gn_w", "ret_gn_b", "gdn_norm_w", "gla_norm_w", "conv_w")
_CONST_NAMES = ("cumg", "cumd", "pmc", "lmc", "causal4", "eye4", "g64", "bdfull", "glabd", "seqsel",
                "decc", "dq", "dk", "dch")


def _mixer_prompt_kernel(*refs, tc, nlev):
    n_in = 6 + len(_PARAM_NAMES) + len(_CONST_NAMES)
    x_ref, nw_ref, win_ref, wout_ref, cos_ref, sin_ref = refs[:6]
    k = dict(zip(_PARAM_NAMES + _CONST_NAMES, refs[6:n_in]))
    o_ref, sret_ref, sgdn_ref, sgla_ref, conv_ref = refs[n_in:n_in + 5]
    z_ref, xp_ref, cv_ref, y_ref = refs[n_in + 5:]
    t = pl.program_id(1)
    nch = tc // C
    ng = CONV_DIM // LANES

    @pl.when(t == 0)
    def _():
        sret_ref[...] = jnp.zeros_like(sret_ref)
        sgdn_ref[...] = jnp.zeros_like(sgdn_ref)
        sgla_ref[...] = jnp.zeros_like(sgla_ref)
        xp_ref[:, pl.ds(0, 8), :] = jnp.zeros((ng, 8, LANES), F32)

    h = _rms_rows(x_ref[...], nw_ref[...]).astype(BF16)
    zc = _proj(h, win_ref[Z_CONV:Z_CONV + CONV_DIM, :])
    for g in range(ng):
        xp_ref[g, pl.ds(8, tc), :] = zc[:, g * LANES:(g + 1) * LANES]
    z_ref[:, :Z_CONV] = _proj(h, win_ref[:Z_CONV, :])
    z_ref[:, Z_DG:] = _proj(h, win_ref[Z_DG:, :])

    cw = k["conv_w"]
    nb = tc // 8
    for g in range(ng):
        wg = [cw[i:i + 1, g * LANES:(g + 1) * LANES] for i in range(CONV_W)]
        taps = {s: xp_ref[g, pl.ds(s, nb, stride=8), :] for s in range(5, 5 + 8 + CONV_W - 1)}
        for j in range(8):
            acc = taps[j + 8] * wg[3]
            for i in range(CONV_W - 1):
                acc = acc + taps[j + 5 + i] * wg[i]
            cv_ref[g, pl.ds(j, nb, stride=8), :] = _silu(acc)
        xp_ref[g, pl.ds(0, 8), :] = xp_ref[g, pl.ds(tc, 8), :]

    zs = [z_ref[pl.ds(c * C, C), :] for c in range(nch)]
    fronts = _gdn_front_multi([[cv_ref[g, pl.ds(c * C, C), :] for g in range(ng)] for c in range(nch)],
                              [z[:, Z_SM:Z_SM + LANES] for z in zs], k)
    xcs = _gdn_inverse([f["ac"] for f in fronts], k, nlev)
    gs = [_gdn_back(f, xc) for f, xc in zip(fronts, xcs)]
    rets = _ret_intra_multi(zs, [cos_ref[pl.ds(c * C, C), :] for c in range(nch)],
                            [sin_ref[pl.ds(c * C, C), :] for c in range(nch)], k)
    glas = _gla_intra_multi(zs, k, nlev)

    for c in range(nch):
        rows = pl.ds(c * C, C)

        o, qd, kd, v = rets[c]
        s = sret_ref[0]
        z_ref[rows, Z_RQ:Z_RQ + 256] = o + _mm(qd, s)
        sret_ref[0] = s * k["dch"][...] + _mm(kd, v, TN) * k["bdfull"][...]

        g = gs[c]
        us, qss = [], []
        for hh in range(NH):
            lhs = jnp.concatenate([g["solk"][hh * C:(hh + 1) * C], g["qst"][hh * C:(hh + 1) * C]], axis=0)
            r = _mm(lhs, sgdn_ref[0, hh])
            us.append(g["solv"][hh * C:(hh + 1) * C] - r[:C])
            qss.append(r[C:])
        ost = g["ebcol"] * jnp.concatenate(qss, axis=0) + _mm(g["qk"], jnp.concatenate(us, axis=0))
        for hh in range(NH):
            sl = slice(hh * C, (hh + 1) * C)
            sgdn_ref[0, hh] = (g["eblast"][hh * C:hh * C + 1] * sgdn_ref[0, hh]
                               + _mm(g["kdec"][sl], us[hh], TN))
        z_ref[rows, Z_CONV:Z_CONV + NH * GDN_DV] = jnp.concatenate(
            [ost[hh * C:(hh + 1) * C] for hh in range(NH)], axis=1)

        o, qe, ke, v, b, _ = glas[c]
        s = sgla_ref[0]
        z_ref[rows, Z_LQ:Z_LQ + 256] = o + _mm(qe, s)
        escale = jnp.exp(b[C - 8:, :].T[:, 7:8])
        sgla_ref[0] = s * escale + _mm(ke, v, TN) * k["glabd"][...]

    y_r = _ret_out(z_ref[:, Z_RQ:Z_RQ + 256], z_ref[:, Z_RG:Z_RG + 256], k)
    y_d = _gdn_out(z_ref[:, Z_CONV:Z_CONV + NH * GDN_DV], z_ref[:, Z_DG:Z_DG + 512], k)
    y_l = _gla_out(z_ref[:, Z_LQ:Z_LQ + 256], z_ref[:, Z_LG:Z_LG + 256], k)
    y_ref[...] = jnp.concatenate([y_r, y_d, y_l], axis=1).astype(BF16)

    o_ref[...] = x_ref[...] + jnp.dot(y_ref[...], wout_ref[...].astype(BF16), preferred_element_type=F32)

    @pl.when(t == pl.num_programs(1) - 1)
    def _():
        for g in range(ng):
            conv_ref[0, :, g * LANES:(g + 1) * LANES] = xp_ref[g, pl.ds(5, 3), :]


def _mixer_prompt(x, l, nw, win, wout, cos, sin, params, consts, *, bsz, tlen, tc=MIX_TC):
    nlev = len(_levels(C))
    nt = tlen // tc
    in_specs = [
        pl.BlockSpec((tc, D_MODEL), lambda b, t: (b * nt + t, 0)),
        _layer_resident(nw, l), _layer_resident(win, l), _layer_resident(wout, l),
        pl.BlockSpec((tc, 256), lambda b, t: (t, 0)),
        pl.BlockSpec((tc, 256), lambda b, t: (t, 0)),
    ] + [_layer_resident(params[n], l) for n in _PARAM_NAMES] + [_resident(consts[n]) for n in _CONST_NAMES]
    extras = [params[n] for n in _PARAM_NAMES] + [consts[n] for n in _CONST_NAMES]
    out_shape = (
        jax.ShapeDtypeStruct(x.shape, F32),
        jax.ShapeDtypeStruct((bsz, 256, 256), F32),
        jax.ShapeDtypeStruct((bsz, NH, GDN_DK, GDN_DV), F32),
        jax.ShapeDtypeStruct((bsz, NH * GLA_DK, NH * GLA_DV), F32),
        jax.ShapeDtypeStruct((bsz, CONV_W - 1, CONV_DIM), F32),
    )
    out_specs = (
        pl.BlockSpec((tc, D_MODEL), lambda b, t: (b * nt + t, 0)),
        pl.BlockSpec((1, 256, 256), lambda b, t: (b, 0, 0)),
        pl.BlockSpec((1, NH, GDN_DK, GDN_DV), lambda b, t: (b, 0, 0, 0)),
        pl.BlockSpec((1, NH * GLA_DK, NH * GLA_DV), lambda b, t: (b, 0, 0)),
        pl.BlockSpec((1, CONV_W - 1, CONV_DIM), lambda b, t: (b, 0, 0)),
    )
    return pl.pallas_call(
        functools.partial(_mixer_prompt_kernel, tc=tc, nlev=nlev),
        grid=(bsz, nt),
        in_specs=in_specs,
        out_specs=out_specs,
        out_shape=out_shape,
        input_output_aliases={0: 0},
        scratch_shapes=[
            pltpu.VMEM((tc, NZ), F32),
            pltpu.VMEM((CONV_DIM // LANES, tc + 8, LANES), F32),
            pltpu.VMEM((CONV_DIM // LANES, tc, LANES), F32),
            pltpu.VMEM((tc, D_MODEL), BF16),
        ],
        compiler_params=pltpu.CompilerParams(
            dimension_semantics=("parallel", "arbitrary"), vmem_limit_bytes=VMEM_LIMIT),
        name="mixer_prompt",
    )(x, nw, win, wout, cos, sin, *extras)


SEQ_S = 4
NSEQ = C // SEQ_S


def _inproj_kernel(x_ref, nw_ref, win_ref, z_ref):
    z_ref[...] = _proj(_rms_rows(x_ref[...], nw_ref[...]).astype(BF16), win_ref[...])


def _sample_inproj(x, l, nw, win, *, row0, n):
    return pl.pallas_call(
        _inproj_kernel,
        grid=(1,),
        in_specs=[pl.BlockSpec((n, D_MODEL), lambda i: (row0 // n, 0)), _layer_resident(nw, l),
                  _layer_resident(win, l)],
        out_specs=pl.BlockSpec((n, NZ), lambda i: (0, 0)),
        out_shape=jax.ShapeDtypeStruct((n, NZ), F32),
        compiler_params=pltpu.CompilerParams(
            dimension_semantics=("arbitrary",), vmem_limit_bytes=VMEM_LIMIT),
        name="sample_inproj",
    )(x, nw, win)


def _mixer_sample_kernel(*refs, nlev):
    n_in = 9 + len(_PARAM_NAMES) + len(_CONST_NAMES)
    (x_ref, zin_ref, wout_ref, cos_ref, sin_ref,
     sret_in, sgdn_in, sgla_in, cbuf_ref) = refs[:9]
    k = dict(zip(_PARAM_NAMES + _CONST_NAMES, refs[9:n_in]))
    o_ref, sret_out, sgdn_out, sgla_out, cout_ref = refs[n_in + 3:n_in + 8]

    x = x_ref[...]
    z = zin_ref[...]

    rowi = lax.broadcasted_iota(jnp.int32, (C, 1), 0)
    tpos = rowi % SEQ_S

    xc = z[:, Z_CONV:Z_CONV + CONV_DIM]
    cb = cbuf_ref[...]
    cw = k["conv_w"]
    acc = xc * cw[3:4, :]
    for i in range(CONV_W - 1):
        cur = pltpu.roll(xc, 3 - i, 0)
        old = cb if i == 0 else pltpu.roll(cb, C - i, 0)
        acc = acc + jnp.where(tpos + i >= 3, cur, old) * cw[i:i + 1, :]
    cout_ref[...] = pltpu.roll(xc, C - 1, 0)
    conv = _silu(acc)

    seq_of_row = rowi // SEQ_S
    seq_of_lane = lax.broadcasted_iota(jnp.int32, (1, C), 1) // SEQ_S

    o, qd, kd, v = _ret_intra(z, cos_ref[...], sin_ref[...], k)
    dch = k["dch"][...]
    kdt = kd.T
    accs = [jnp.zeros((C, RET_DV), F32) for _ in range(NH)]
    for n in range(NSEQ):
        rm = seq_of_row == n
        lm = seq_of_lane == n
        for hh in range(NH):
            sl = slice(hh * RET_DK, (hh + 1) * RET_DK)
            s = sret_in[n, hh]
            accs[hh] = jnp.where(rm, _mm(qd[:, sl], s), accs[hh])
            sret_out[n, hh] = s * dch[:, sl] + _mm(jnp.where(lm, kdt[sl], 0.0), v[:, sl])
    o = o + jnp.concatenate(accs, axis=1)
    y_r = _ret_out(o, z[:, Z_RG:Z_RG + 256], k)

    f = _gdn_front([conv[:, gi * LANES:(gi + 1) * LANES] for gi in range(CONV_DIM // LANES)],
                   z[:, Z_SM:Z_SM + LANES], k)
    g = _gdn_back(f, _gdn_inverse([f["ac"]], k, nlev)[0])
    rowi2 = lax.broadcasted_iota(jnp.int32, (2 * C, 1), 0)
    seq_of_row2 = (rowi2 % C) // SEQ_S
    lhs = [jnp.concatenate([g["solk"][hh * C:(hh + 1) * C], g["qst"][hh * C:(hh + 1) * C]], axis=0)
           for hh in range(NH)]
    accs = [jnp.zeros((2 * C, GDN_DV), F32) for _ in range(NH)]
    for n in range(NSEQ):
        rm2 = seq_of_row2 == n
        for hh in range(NH):
            accs[hh] = jnp.where(rm2, _mm(lhs[hh], sgdn_in[n, hh]), accs[hh])
    us = [g["solv"][hh * C:(hh + 1) * C] - accs[hh][:C] for hh in range(NH)]
    ost = (g["ebcol"] * jnp.concatenate([accs[hh][C:] for hh in range(NH)], axis=0)
           + _mm(g["qk"], jnp.concatenate(us, axis=0)))
    kdts = [g["kdec"][hh * C:(hh + 1) * C].T for hh in range(NH)]
    for n in range(NSEQ):
        lm = seq_of_lane == n
        for hh in range(NH):
            scale = g["eblast"][hh * C + n * SEQ_S:hh * C + n * SEQ_S + 1]
            sgdn_out[n, hh] = scale * sgdn_in[n, hh] + _mm(jnp.where(lm, kdts[hh], 0.0), us[hh])
    y_d = _gdn_out(jnp.concatenate([ost[hh * C:(hh + 1) * C] for hh in range(NH)], axis=1),
                   z[:, Z_DG:Z_DG + 512], k)

    o, qe, ke, v, b, blast = _gla_intra(z, k, nlev)
    ket = ke.T
    hi, mid = _split2(blast * (1.0 / SEQ_S))
    blt2 = lax.dot_general(jnp.concatenate([hi, mid], axis=1), k["seqsel"][...], TN, preferred_element_type=F32)
    eblt = jnp.exp(blt2[:NH * GLA_DK] + blt2[NH * GLA_DK:])
    accs = [jnp.zeros((C, GLA_DV), F32) for _ in range(NH)]
    for n in range(NSEQ):
        rm = seq_of_row == n
        lm = seq_of_lane == n
        for hh in range(NH):
            sk = slice(hh * GLA_DK, (hh + 1) * GLA_DK)
            sv = slice(hh * GLA_DV, (hh + 1) * GLA_DV)
            s = sgla_in[n, hh]
            accs[hh] = jnp.where(rm, _mm(qe[:, sk], s), accs[hh])
            sgla_out[n, hh] = s * eblt[sk, n:n + 1] + _mm(jnp.where(lm, ket[sk], 0.0), v[:, sv])
    o = o + jnp.concatenate(accs, axis=1)
    y_l = _gla_out(o, z[:, Z_LG:Z_LG + 256], k)

    y = jnp.concatenate([y_r, y_d, y_l], axis=1).astype(BF16)
    o_ref[...] = x + jnp.dot(y, wout_ref[...].astype(BF16), preferred_element_type=F32)


def _mixer_sample(x, l, nw, win, wout, cos, sin, sret, sgdn, sgla, cbuf, acc_ret, acc_gdn, acc_gla,
                  params, consts, *, row0):
    nb = sret.shape[1]
    n = nb * SEQ_S
    nlev = len(_levels(SEQ_S))
    blk0 = row0 // C

    def st_spec(dk, dv):
        return pl.BlockSpec((None, NSEQ, NH, dk, dv), lambda i: (l, i, 0, 0, 0))

    st_specs = [st_spec(RET_DK, RET_DV), st_spec(GDN_DK, GDN_DV), st_spec(GLA_DK, GLA_DV)]
    any_spec = pl.BlockSpec(memory_space=pl.ANY)
    z = _sample_inproj(x, l, nw, win, row0=row0, n=n)
    in_specs = [
        pl.BlockSpec((C, D_MODEL), lambda i: (blk0 + i, 0)),
        pl.BlockSpec((C, NZ), lambda i: (i, 0)),
        _layer_resident(wout, l), _resident(cos), _resident(sin),
    ] + st_specs + [pl.BlockSpec((C, CONV_DIM), lambda i: (i, 0))] + [
        _layer_resident(params[nm], l) for nm in _PARAM_NAMES] + [
        _resident(consts[nm]) for nm in _CONST_NAMES] + [any_spec] * 3
    extras = [params[nm] for nm in _PARAM_NAMES] + [consts[nm] for nm in _CONST_NAMES]
    n_in = len(in_specs)
    out_shape = (
        jax.ShapeDtypeStruct(x.shape, F32),
        jax.ShapeDtypeStruct(acc_ret.shape, F32),
        jax.ShapeDtypeStruct(acc_gdn.shape, F32),
        jax.ShapeDtypeStruct(acc_gla.shape, F32),
        jax.ShapeDtypeStruct((n, CONV_DIM), F32),
    )
    out_specs = (pl.BlockSpec((C, D_MODEL), lambda i: (blk0 + i, 0)),) + tuple(st_specs) + (
        pl.BlockSpec((C, CONV_DIM), lambda i: (i, 0)),)
    return pl.pallas_call(
        functools.partial(_mixer_sample_kernel, nlev=nlev),
        grid=(n // C,),
        in_specs=in_specs,
        out_specs=out_specs,
        out_shape=out_shape,
        input_output_aliases={0: 0, n_in - 3: 1, n_in - 2: 2, n_in - 1: 3},
        compiler_params=pltpu.CompilerParams(
            dimension_semantics=("parallel",), vmem_limit_bytes=VMEM_LIMIT),
        name="mixer_sample",
    )(x, z, wout, cos, sin, sret, sgdn, sgla, cbuf, *extras, acc_ret, acc_gdn, acc_gla)


def _permute_w_in(w):
    wt = jnp.swapaxes(w, 1, 2)
    pad = jnp.zeros((w.shape[0], NZ - Z_SM - 24, w.shape[1]), w.dtype)
    return jnp.concatenate(
        [wt[:, 0:2560], wt[:, 2568:3080], wt[:, 3080:3592], wt[:, 3608:3864], wt[:, 2560:2568],
         wt[:, 3592:3608], pad], axis=1).astype(BF16)


def _rows(v, width=None):
    v = v.astype(F32)[:, None, :]
    if width is not None and v.shape[-1] < width:
        v = jnp.pad(v, ((0, 0), (0, 0), (0, width - v.shape[-1])))
    return v


def kernel(x_prompt, x_sample, state_ret, state_gdn, state_gdn_conv, state_gla, norm_ffn1, ffn1_w1, ffn1_w3, ffn1_w2, norm_mix, w_in, ret_gn_w, ret_gn_b, gdn_conv_w, gdn_A_log, gdn_dt_bias, gdn_norm_w, gla_gate_w, gla_gate_b, gla_norm_w, w_out, norm_ffn2, ffn2_w1, ffn2_w3, ffn2_w2, norm_final):
    bp, tp, _ = x_prompt.shape
    bs, ts, _ = x_sample.shape
    n_prompt, n_sample = bp * tp, bs * ts
    assert ts == SEQ_S and tp % MIX_TC == 0 and n_sample % C == 0 and n_prompt % C == 0

    consts_p = dict(_chunk_consts(C), **_ret_consts(C))
    consts_s = dict(_chunk_consts(SEQ_S), **_ret_consts(SEQ_S))
    cos_p, sin_p = _rope_tables(jnp.arange(tp, dtype=jnp.int32))
    cos_s, sin_s = _rope_tables(PAST_LEN + (jnp.arange(C, dtype=jnp.int32) % SEQ_S))

    params = dict(
        alog=_rows(gdn_A_log, LANES), dtb=_rows(gdn_dt_bias, LANES),
        gwp=jnp.pad(gla_gate_w, ((0, 0), (SM_LLR, LANES - SM_LLR - GLA_RANK), (0, 0))).astype(BF16),
        gla_gate_b=_rows(gla_gate_b), ret_gn_w=_rows(ret_gn_w), ret_gn_b=_rows(ret_gn_b),
        gdn_norm_w=_rows(gdn_norm_w), gla_norm_w=_rows(jnp.tile(gla_norm_w, (1, NH))),
        conv_w=gdn_conv_w.astype(F32),
    )
    win = _permute_w_in(w_in)
    wout = w_out.astype(F32)
    f1 = (_rows(norm_ffn1), ffn1_w1.astype(F32), ffn1_w3.astype(F32), ffn1_w2.astype(F32))
    f2 = (_rows(norm_ffn2), ffn2_w1.astype(F32), ffn2_w3.astype(F32), ffn2_w2.astype(F32))
    nw = _rows(norm_mix)
    fw = norm_final.astype(F32)[None, :]
    sret_in, sgdn_in, sgla_in = state_ret.astype(F32), state_gdn.astype(F32), state_gla.astype(F32)
    cbuf = jnp.pad(state_gdn_conv.astype(F32), ((0, 0), (0, 0), (0, 1), (0, 0))).reshape(DEPTH, n_sample, CONV_DIM)
    acc_ret, acc_gdn, acc_gla = (lax.empty(s.shape, F32) for s in (sret_in, sgdn_in, sgla_in))

    ffn = functools.partial(_ffn, n_prompt=n_prompt, n_sample=n_sample)
    xs = (x_prompt.reshape(n_prompt, D_MODEL), x_sample.reshape(n_sample, D_MODEL))
    outs_p = [[] for _ in range(4)]
    conv_s = []
    for l in range(DEPTH):
        x = ffn(xs, l, *f1, fw, first=(l == 0), last=False)
        x, sret, sgdn, sgla, conv = _mixer_prompt(x, l, nw, win, wout, cos_p, sin_p, params, consts_p,
                                                  bsz=bp, tlen=tp)
        outs_p[0].append(jnp.stack([sret[:, h * 64:(h + 1) * 64, h * 64:(h + 1) * 64] for h in range(NH)], axis=1))
        outs_p[1].append(sgdn)
        outs_p[2].append(conv)
        outs_p[3].append(jnp.stack([sgla[:, h * 32:(h + 1) * 32, h * 64:(h + 1) * 64] for h in range(NH)], axis=1))
        x, acc_ret, acc_gdn, acc_gla, cout = _mixer_sample(
            x, l, nw, win, wout, cos_s, sin_s, sret_in, sgdn_in, sgla_in, cbuf[l], acc_ret, acc_gdn, acc_gla,
            params, consts_s, row0=n_prompt)
        conv_s.append(cout.reshape(bs, ts, CONV_DIM)[:, :CONV_W - 1])
        xs = (ffn((x,), l, *f2, fw, first=False, last=(l == DEPTH - 1)),)

    y_prompt, y_sample = xs[0]
    dts = (state_ret.dtype, state_gdn.dtype, state_gdn_conv.dtype, state_gla.dtype)
    sp = [jnp.stack(o).astype(d) for o, d in zip(outs_p, dts)]
    return (y_prompt.reshape(bp, tp, D_MODEL), y_sample.reshape(bs, ts, D_MODEL), sp[0], sp[1], sp[2], sp[3],
            acc_ret.astype(dts[0]), acc_gdn.astype(dts[1]), jnp.stack(conv_s).astype(dts[2]), acc_gla.astype(dts[3]))
```

```python
import functools

import numpy as np
import jax
import jax.numpy as jnp
from jax import lax
from jax.experimental import pallas as pl
from jax.experimental.pallas import tpu as pltpu

F32 = jnp.float32
BF16 = jnp.bfloat16

D_MODEL = 1024
DEPTH = 4
PAST_LEN = 16384
NH = 4
RET_DK = 64
RET_DV = 64
GDN_DK = 128
GDN_DV = 128
GLA_DK = 32
GLA_DV = 64
GLA_RANK = 16
GLA_GATE_NORM = 16.0
CONV_W = 4
CONV_DIM = NH * (2 * GDN_DK + GDN_DV)
D_FF = 2816
ROPE_BASE = 10000.0
EPS = 1e-6
GN_EPS = 1e-5

C = 64
R = NH * C
LANES = 128

Z_RQ, Z_RK, Z_RV, Z_RG = 0, 256, 512, 768
Z_CONV = 1024
Z_DG = 2560
Z_LQ, Z_LK, Z_LV, Z_LG = 3072, 3200, 3328, 3584
Z_SM = 3840
NZ = 3968
SM_DA, SM_DB, SM_LLR = 0, 4, 8

NN = (((1,), (0,)), ((), ()))
NT = (((1,), (1,)), ((), ()))
TN = (((0,), (0,)), ((), ()))

VMEM_LIMIT = 56 * 1024 * 1024
FFN_TM = 512
FFN_TF = 256
MIX_TC = 512


def _mm(a, b, dims=NN):
    return lax.dot_general(a.astype(BF16), b.astype(BF16), dims, preferred_element_type=F32)


def _proj(h, w_rows):
    return lax.dot_general(h, w_rows, NT, preferred_element_type=F32)


def _split2(x):
    hi = x.astype(BF16)
    return hi, (x - hi.astype(F32)).astype(BF16)


def _mm_data_const(x, cb):
    hi, mid = _split2(x)
    m = x.shape[0]
    r = lax.dot_general(jnp.concatenate([hi, mid], axis=0), cb, NN, preferred_element_type=F32)
    return r[:m] + r[m:]


def _mm_const_data(cb, x):
    hi, mid = _split2(x)
    n = x.shape[1]
    r = lax.dot_general(cb, jnp.concatenate([hi, mid], axis=1), NN, preferred_element_type=F32)
    return r[:, :n] + r[:, n:]


def _silu(x):
    hx = 0.5 * x
    return hx + hx * jnp.tanh(hx)


def _softplus(x):
    return jnp.maximum(x, 0.0) + jnp.log1p(jnp.exp(-jnp.abs(x)))


def _log_sigmoid(x):
    return jnp.minimum(x, 0.0) - jnp.log1p(jnp.exp(-jnp.abs(x)))


def _rms_rows(x, w):
    ms = jnp.mean(x * x, axis=-1, keepdims=True)
    return x * lax.rsqrt(ms + EPS) * w


def _stack_bd(x, width):
    lane = lax.broadcasted_iota(jnp.int32, x.shape, 1) // width
    zero = jnp.zeros_like(x)
    return jnp.concatenate([jnp.where(lane == h, x, zero) for h in range(NH)], axis=0)


def _col_to_compact(col):
    lane = lax.broadcasted_iota(jnp.int32, (C, R), 1) // C
    out = jnp.zeros((C, R), F32)
    for h in range(NH):
        out = jnp.where(lane == h, col[h * C:(h + 1) * C], out)
    return out


def _ffn_kernel(*refs, first, last, npb, tf):
    n_x = 2 if first else 1
    xrefs, (nw_ref, w1_ref, w3_ref, w2_ref, fw_ref) = refs[:n_x], refs[n_x:n_x + 5]
    n_o = 2 if last else 1
    orefs = refs[n_x + 5:n_x + 5 + n_o]
    h_ref, acc_ref = refs[n_x + 5 + n_o:]
    i = pl.program_id(0)
    x = jnp.where(i < npb, xrefs[0][...], xrefs[1][...]) if first else xrefs[0][...]
    h_ref[...] = _rms_rows(x, nw_ref[...]).astype(BF16)
    for j in range(D_FF // tf):
        h = h_ref[...]
        a = jnp.dot(h, w1_ref[:, j * tf:(j + 1) * tf].astype(BF16), preferred_element_type=F32)
        g = jnp.dot(h, w3_ref[:, j * tf:(j + 1) * tf].astype(BF16), preferred_element_type=F32)
        p = jnp.dot((_silu(a) * g).astype(BF16), w2_ref[j * tf:(j + 1) * tf, :].astype(BF16),
                    preferred_element_type=F32)
        if j == 0:
            acc_ref[...] = p
        else:
            acc_ref[...] += p
    y = x + 0.5 * acc_ref[...]
    if last:
        y = _rms_rows(y, fw_ref[...])

        @pl.when(i < npb)
        def _():
            orefs[0][...] = y

        @pl.when(i >= npb)
        def _():
            orefs[1][...] = y
    else:
        orefs[0][...] = y


def _resident(a):
    nd = a.ndim
    return pl.BlockSpec(a.shape, lambda *_, _n=nd: (0,) * _n, pipeline_mode=pl.Buffered(1))


def _layer_resident(a, l):
    nd = a.ndim - 1
    return pl.BlockSpec((None,) + a.shape[1:], lambda *_, _n=nd: (l,) + (0,) * _n, pipeline_mode=pl.Buffered(1))


def _ffn(xs, l, nw, w1, w3, w2, fw, *, first, last, n_prompt, n_sample, tm=FFN_TM, tf=FFN_TF):
    assert n_sample == tm and n_prompt % tm == 0
    npb = n_prompt // tm
    prompt_blk = pl.BlockSpec((tm, D_MODEL), lambda i: (jnp.minimum(i, npb - 1), 0))
    sample_blk = pl.BlockSpec((tm, D_MODEL), lambda i: (0, 0))
    unified_blk = pl.BlockSpec((tm, D_MODEL), lambda i: (i, 0))
    if last:
        out_specs = (prompt_blk, sample_blk)
        out_shape = (jax.ShapeDtypeStruct((n_prompt, D_MODEL), F32), jax.ShapeDtypeStruct((n_sample, D_MODEL), F32))
    else:
        out_specs = unified_blk
        out_shape = jax.ShapeDtypeStruct((n_prompt + n_sample, D_MODEL), F32)
    return pl.pallas_call(
        functools.partial(_ffn_kernel, first=first, last=last, npb=npb, tf=tf),
        grid=(npb + 1,),
        in_specs=([prompt_blk, sample_blk] if first else [unified_blk]) + [
            _layer_resident(nw, l), _layer_resident(w1, l), _layer_resident(w3, l), _layer_resident(w2, l),
            _resident(fw)],
        out_specs=out_specs,
        out_shape=out_shape,
        scratch_shapes=[pltpu.VMEM((tm, D_MODEL), BF16), pltpu.VMEM((tm, D_MODEL), F32)],
        compiler_params=pltpu.CompilerParams(
            dimension_semantics=("arbitrary",), vmem_limit_bytes=VMEM_LIMIT),
        name="ffn",
    )(*xs, nw, w1, w3, w2, fw)


def _levels(L):
    return [s for s in (1, 2, 4, 8, 16, 32) if s < L]


def _chunk_consts(L):
    i = np.arange(C)
    sid, p = i // L, i % L
    same = sid[:, None] == sid[None, :]
    causal = same & (i[:, None] >= i[None, :])
    strict = same & (i[:, None] > i[None, :])
    lv = _levels(L)
    cum, pm, lm = [causal, same], [np.eye(C, dtype=bool)], []
    for s in lv:
        blk = p // s
        inblk = same & (blk[:, None] == blk[None, :])
        odd = (blk % 2 == 1)[:, None]
        cum.append(inblk & np.where(odd, i[None, :] <= i[:, None], i[None, :] > i[:, None]))
        pm.append(same & odd & (blk[None, :] == blk[:, None] - 1))
        lm.append(strict & ((p[:, None] // (2 * s)) == (p[None, :] // (2 * s))) & (blk[:, None] != blk[None, :]))
    r = np.arange(R)
    bdfull = (r[:, None] // C) == (r[None, :] // C)
    wide = lambda m: np.tile(m, (1, NH))
    f = lambda m: jnp.asarray(np.asarray(m, dtype=np.float32))
    gla_bd = (np.arange(NH * GLA_DK)[:, None] // GLA_DK) == (np.arange(NH * GLA_DV)[None, :] // GLA_DV)
    seqsel = np.zeros((C, LANES), np.float32)
    seqsel[i, sid] = 1.0
    return dict(
        cumg=f(np.concatenate(cum, axis=0)).astype(BF16),
        cumd=f(np.concatenate(cum[:2], axis=0)).astype(BF16),
        pmc=f(np.stack([wide(m) for m in pm])),
        lmc=f(np.stack([wide(m) for m in lm])),
        causal4=f(wide(causal)), eye4=f(wide(np.eye(C, dtype=bool))),
        g64=(f(bdfull) * (1.0 / 64.0)).astype(BF16), bdfull=f(bdfull), glabd=f(gla_bd),
        seqsel=jnp.asarray(seqsel).astype(BF16),
    )


def _ret_consts(L):
    log_gamma = jnp.log(1.0 - 2.0 ** (-5.0 - jnp.arange(NH, dtype=F32)))
    i = np.arange(C)
    sid, p = i // L, (i % L).astype(np.float32)
    causal = (sid[:, None] == sid[None, :]) & (i[:, None] >= i[None, :])
    diff = jnp.asarray(np.where(causal, p[:, None] - p[None, :], 0.0).astype(np.float32))
    dec = jnp.where(causal[None], jnp.exp(diff[None] * log_gamma[:, None, None]), 0.0)
    lg_l = jnp.repeat(log_gamma, RET_DV)[None, :]
    pj = jnp.asarray(p)[:, None]
    return dict(
        decc=jnp.concatenate([dec[h] for h in range(NH)], axis=1),
        dq=jnp.exp((pj + 1.0) * lg_l),
        dk=jnp.exp((L - 1.0 - pj) * lg_l),
        dch=jnp.exp(L * lg_l),
    )


def _rope_tables(pos):
    half = RET_DK // 2
    inv = ROPE_BASE ** (-jnp.arange(half, dtype=F32) / half)
    ang = pos.astype(F32)[:, None] * inv[None, :]
    cos, sin = jnp.cos(ang), jnp.sin(ang)
    return (jnp.tile(jnp.concatenate([cos, cos], axis=1), (1, NH)),
            jnp.tile(jnp.concatenate([-sin, sin], axis=1), (1, NH)))


def _rotary(x, cos, sin):
    lane = lax.broadcasted_iota(jnp.int32, x.shape, 1)
    swapped = jnp.where((lane % RET_DK) < RET_DK // 2,
                        pltpu.roll(x, x.shape[1] - RET_DK // 2, 1), pltpu.roll(x, RET_DK // 2, 1))
    return x * cos + swapped * sin


def _ret_intra_multi(zs, coss, sins, k):
    qs = [_rotary(z[:, Z_RQ:Z_RQ + 256], c, s) for z, c, s in zip(zs, coss, sins)]
    kks = [_rotary(z[:, Z_RK:Z_RK + 256], c, s) * (RET_DK ** -0.5) for z, c, s in zip(zs, coss, sins)]
    vs = [z[:, Z_RV:Z_RV + 256] for z in zs]
    kbds = [_stack_bd(kk.astype(BF16), RET_DK) for kk in kks]
    scs = [_mm(q, kbd, NT) * k["decc"][...] for q, kbd in zip(qs, kbds)]
    vbds = [_stack_bd(v.astype(BF16), RET_DV) for v in vs]
    os = [_mm(sc, vbd) for sc, vbd in zip(scs, vbds)]
    return [(o, q * k["dq"][...], kk * k["dk"][...], v) for o, q, kk, v in zip(os, qs, kks, vs)]


def _ret_intra(z, cos, sin, k):
    return _ret_intra_multi([z], [cos], [sin], k)[0]


def _ret_out(o, rg, k):
    g = k["g64"][...]
    d = o - _mm_data_const(o, g)
    var = _mm_data_const(d * d, g)
    return _silu(rg) * (d * lax.rsqrt(var + GN_EPS) * k["ret_gn_w"][...] + k["ret_gn_b"][...])


def _gla_intra_multi(zs, k, nlev):
    gks = [_log_sigmoid(_mm(z[:, Z_SM:Z_SM + LANES], k["gwp"][...]) + k["gla_gate_b"][...]) * (1.0 / GLA_GATE_NORM)
           for z in zs]
    qs = [z[:, Z_LQ:Z_LQ + 128] * (GLA_DK ** -0.5) for z in zs]
    kks = [z[:, Z_LK:Z_LK + 128] for z in zs]
    vs = [z[:, Z_LV:Z_LV + 256] for z in zs]
    css = [_mm_const_data(k["cumg"][...], gk) for gk in gks]
    scs = [_mm(q, _stack_bd(kk.astype(BF16), GLA_DK), NT) * k["pmc"][0] for q, kk in zip(qs, kks)]
    for li in range(nlev):
        es = [jnp.exp(cs[(2 + li) * C:(3 + li) * C]) for cs in css]
        scs = [sc + _mm(q * e, _stack_bd((kk * e).astype(BF16), GLA_DK), NT) * k["pmc"][li + 1]
               for sc, q, kk, e in zip(scs, qs, kks, es)]
    os = [_mm(sc, _stack_bd(v.astype(BF16), GLA_DV)) for sc, v in zip(scs, vs)]
    out = []
    for o, q, kk, v, cs in zip(os, qs, kks, vs, css):
        b, blast = cs[:C], cs[C:2 * C]
        out.append((o, q * jnp.exp(b), kk * jnp.exp(blast - b), v, b, blast))
    return out


def _gla_intra(z, k, nlev):
    return _gla_intra_multi([z], k, nlev)[0]


def _gla_out(o, lg, k):
    ms = _mm_data_const(o * o, k["g64"][...])
    return o * lax.rsqrt(ms + EPS) * k["gla_norm_w"][...] * _silu(lg)


def _lane_col(blk, lane0):
    lane = lax.broadcasted_iota(jnp.int32, blk.shape, 1)
    st = jnp.concatenate([jnp.where(lane == lane0 + h, blk, 0.0) for h in range(NH)], axis=0)
    return jnp.sum(st, axis=-1, keepdims=True)


def _gdn_front(conv, sm, k):
    def l2n(x):
        return x * lax.rsqrt(jnp.sum(x * x, axis=-1, keepdims=True) + EPS)

    qs = [l2n(conv[h]) * (GDN_DK ** -0.5) for h in range(NH)]
    ks = [l2n(conv[NH + h]) for h in range(NH)]
    vst = jnp.concatenate(conv[2 * NH:], axis=0)
    qst = jnp.concatenate(qs, axis=0)
    kst = jnp.concatenate(ks, axis=0)
    kq = []
    for p in range(NH // 2):
        kp = jnp.concatenate(ks[2 * p:2 * p + 2], axis=1).astype(BF16)
        lane = lax.broadcasted_iota(jnp.int32, kp.shape, 1) // GDN_DK
        kbd = jnp.concatenate([jnp.where(lane == hl, kp, jnp.zeros_like(kp)) for hl in range(2)], axis=0)
        lhs = jnp.concatenate([kp, jnp.concatenate(qs[2 * p:2 * p + 2], axis=1).astype(BF16)], axis=0)
        kq.append(lax.dot_general(lhs, kbd, NT, preferred_element_type=F32))
    kq = jnp.concatenate(kq, axis=1)
    gd = -jnp.exp(k["alog"][...]) * _softplus(sm + k["dtb"][...])
    beta = jax.nn.sigmoid(sm)
    cs = _mm_const_data(k["cumd"][...], gd)
    bcol = _lane_col(cs[:C], SM_DA)
    blcol = _lane_col(cs[C:], SM_DA)
    betacol = _lane_col(beta, SM_DB)
    bcolc = _col_to_compact(bcol)
    browc = jnp.sum(bcolc * k["eye4"][...], axis=0, keepdims=True)
    causal = k["causal4"][...] > 0.5
    decc = jnp.where(causal, jnp.exp(jnp.where(causal, bcolc - browc, 0.0)), 0.0)
    ac = kq[:C] * decc * _col_to_compact(betacol)
    qkc = kq[C:] * decc
    return dict(ac=ac, qkc=qkc, qst=qst, kst=kst, vst=vst, bcol=bcol, blcol=blcol, betacol=betacol)


def _gdn_inverse(acs, k, nlev):
    xs = [k["eye4"][...] - a * k["lmc"][0] for a in acs]
    for li in range(1, nlev):
        ms = [a * k["lmc"][li] for a in acs]
        ys = [_mm(m, _stack_bd(x.astype(BF16), C)) for m, x in zip(ms, xs)]
        xs = [x - _mm(x, _stack_bd(y.astype(BF16), C)) for x, y in zip(xs, ys)]
    return xs


def _gdn_back(f, xc):
    kb = f["kst"] * f["betacol"]
    rhs = jnp.concatenate([f["vst"] * f["betacol"], kb * jnp.exp(f["bcol"])], axis=1)
    sol = _mm(_stack_bd(xc.astype(BF16), C), rhs)
    return dict(qst=f["qst"], solv=sol[:, :GDN_DV], solk=sol[:, GDN_DV:],
                qk=_stack_bd(f["qkc"].astype(BF16), C), ebcol=jnp.exp(f["bcol"]),
                kdec=f["kst"] * jnp.exp(f["blcol"] - f["bcol"]), eblast=jnp.exp(f["blcol"]))


def _gdn_out(o, dg, k):
    w = k["gdn_norm_w"][...]
    y = jnp.concatenate([_rms_rows(o[:, h * GDN_DV:(h + 1) * GDN_DV], w) for h in range(NH)], axis=1)
    return y * _silu(dg)


_PARAM_NAMES = ("alog", "dtb", "gwp", "gla_gate_b", "ret_gn_w", "ret_gn_b", "gdn_norm_w", "gla_norm_w", "conv_w")
_CONST_NAMES = ("cumg", "cumd", "pmc", "lmc", "causal4", "eye4", "g64", "bdfull", "glabd", "seqsel",
                "decc", "dq", "dk", "dch")


def _mixer_prompt_kernel(*refs, tc, nlev):
    n_in = 6 + len(_PARAM_NAMES) + len(_CONST_NAMES)
    x_ref, nw_ref, win_ref, wout_ref, cos_ref, sin_ref = refs[:6]
    k = dict(zip(_PARAM_NAMES + _CONST_NAMES, refs[6:n_in]))
    o_ref, sret_ref, sgdn_ref, sgla_ref, conv_ref = refs[n_in:n_in + 5]
    z_ref, xp_ref, cv_ref, y_ref = refs[n_in + 5:]
    t = pl.program_id(1)
    nch = tc // C
    ng = CONV_DIM // LANES

    @pl.when(t == 0)
    def _():
        sret_ref[...] = jnp.zeros_like(sret_ref)
        sgdn_ref[...] = jnp.zeros_like(sgdn_ref)
        sgla_ref[...] = jnp.zeros_like(sgla_ref)
        xp_ref[:, pl.ds(0, 8), :] = jnp.zeros((ng, 8, LANES), F32)

    h = _rms_rows(x_ref[...], nw_ref[...]).astype(BF16)
    zc = _proj(h, win_ref[Z_CONV:Z_CONV + CONV_DIM, :])
    for g in range(ng):
        xp_ref[g, pl.ds(8, tc), :] = zc[:, g * LANES:(g + 1) * LANES]
    z_ref[:, :Z_CONV] = _proj(h, win_ref[:Z_CONV, :])
    z_ref[:, Z_DG:] = _proj(h, win_ref[Z_DG:, :])

    cw = k["conv_w"]
    nb = tc // 8
    for g in range(ng):
        wg = [cw[i:i + 1, g * LANES:(g + 1) * LANES] for i in range(CONV_W)]
        taps = {s: xp_ref[g, pl.ds(s, nb, stride=8), :] for s in range(5, 5 + 8 + CONV_W - 1)}
        for j in range(8):
            acc = taps[j + 8] * wg[3]
            for i in range(CONV_W - 1):
                acc = acc + taps[j + 5 + i] * wg[i]
            cv_ref[g, pl.ds(j, nb, stride=8), :] = _silu(acc)
        xp_ref[g, pl.ds(0, 8), :] = xp_ref[g, pl.ds(tc, 8), :]

    zs = [z_ref[pl.ds(c * C, C), :] for c in range(nch)]
    rets = _ret_intra_multi(zs, [cos_ref[pl.ds(c * C, C), :] for c in range(nch)],
                            [sin_ref[pl.ds(c * C, C), :] for c in range(nch)], k)
    glas = _gla_intra_multi(zs, k, nlev)
    fronts = [_gdn_front([cv_ref[g, pl.ds(c * C, C), :] for g in range(ng)], z[:, Z_SM:Z_SM + LANES], k)
              for c, z in enumerate(zs)]
    xcs = _gdn_inverse([f["ac"] for f in fronts], k, nlev)
    gs = [_gdn_back(f, xc) for f, xc in zip(fronts, xcs)]

    for c in range(nch):
        rows = pl.ds(c * C, C)

        o, qd, kd, v = rets[c]
        s = sret_ref[0]
        z_ref[rows, Z_RQ:Z_RQ + 256] = o + _mm(qd, s)
        sret_ref[0] = s * k["dch"][...] + _mm(kd, v, TN) * k["bdfull"][...]

        g = gs[c]
        us, qss = [], []
        for hh in range(NH):
            lhs = jnp.concatenate([g["solk"][hh * C:(hh + 1) * C], g["qst"][hh * C:(hh + 1) * C]], axis=0)
            r = _mm(lhs, sgdn_ref[0, hh])
            us.append(g["solv"][hh * C:(hh + 1) * C] - r[:C])
            qss.append(r[C:])
        ost = g["ebcol"] * jnp.concatenate(qss, axis=0) + _mm(g["qk"], jnp.concatenate(us, axis=0))
        for hh in range(NH):
            sl = slice(hh * C, (hh + 1) * C)
            sgdn_ref[0, hh] = (g["eblast"][hh * C:hh * C + 1] * sgdn_ref[0, hh]
                               + _mm(g["kdec"][sl], us[hh], TN))
        z_ref[rows, Z_CONV:Z_CONV + NH * GDN_DV] = jnp.concatenate(
            [ost[hh * C:(hh + 1) * C] for hh in range(NH)], axis=1)

        o, qe, ke, v, b, _ = glas[c]
        s = sgla_ref[0]
        z_ref[rows, Z_LQ:Z_LQ + 256] = o + _mm(qe, s)
        escale = jnp.exp(b[C - 8:, :].T[:, 7:8])
        sgla_ref[0] = s * escale + _mm(ke, v, TN) * k["glabd"][...]

    y_r = _ret_out(z_ref[:, Z_RQ:Z_RQ + 256], z_ref[:, Z_RG:Z_RG + 256], k)
    y_d = _gdn_out(z_ref[:, Z_CONV:Z_CONV + NH * GDN_DV], z_ref[:, Z_DG:Z_DG + 512], k)
    y_l = _gla_out(z_ref[:, Z_LQ:Z_LQ + 256], z_ref[:, Z_LG:Z_LG + 256], k)
    y_ref[...] = jnp.concatenate([y_r, y_d, y_l], axis=1).astype(BF16)

    o_ref[...] = x_ref[...] + jnp.dot(y_ref[...], wout_ref[...].astype(BF16), preferred_element_type=F32)

    @pl.when(t == pl.num_programs(1) - 1)
    def _():
        for g in range(ng):
            conv_ref[0, :, g * LANES:(g + 1) * LANES] = xp_ref[g, pl.ds(5, 3), :]


def _mixer_prompt(x, l, nw, win, wout, cos, sin, params, consts, *, bsz, tlen, tc=MIX_TC):
    nlev = len(_levels(C))
    nt = tlen // tc
    in_specs = [
        pl.BlockSpec((tc, D_MODEL), lambda b, t: (b * nt + t, 0)),
        _layer_resident(nw, l), _layer_resident(win, l), _layer_resident(wout, l),
        pl.BlockSpec((tc, 256), lambda b, t: (t, 0)),
        pl.BlockSpec((tc, 256), lambda b, t: (t, 0)),
    ] + [_layer_resident(params[n], l) for n in _PARAM_NAMES] + [_resident(consts[n]) for n in _CONST_NAMES]
    extras = [params[n] for n in _PARAM_NAMES] + [consts[n] for n in _CONST_NAMES]
    out_shape = (
        jax.ShapeDtypeStruct(x.shape, F32),
        jax.ShapeDtypeStruct((bsz, 256, 256), F32),
        jax.ShapeDtypeStruct((bsz, NH, GDN_DK, GDN_DV), F32),
        jax.ShapeDtypeStruct((bsz, NH * GLA_DK, NH * GLA_DV), F32),
        jax.ShapeDtypeStruct((bsz, CONV_W - 1, CONV_DIM), F32),
    )
    out_specs = (
        pl.BlockSpec((tc, D_MODEL), lambda b, t: (b * nt + t, 0)),
        pl.BlockSpec((1, 256, 256), lambda b, t: (b, 0, 0)),
        pl.BlockSpec((1, NH, GDN_DK, GDN_DV), lambda b, t: (b, 0, 0, 0)),
        pl.BlockSpec((1, NH * GLA_DK, NH * GLA_DV), lambda b, t: (b, 0, 0)),
        pl.BlockSpec((1, CONV_W - 1, CONV_DIM), lambda b, t: (b, 0, 0)),
    )
    return pl.pallas_call(
        functools.partial(_mixer_prompt_kernel, tc=tc, nlev=nlev),
        grid=(bsz, nt),
        in_specs=in_specs,
        out_specs=out_specs,
        out_shape=out_shape,
        input_output_aliases={0: 0},
        scratch_shapes=[
            pltpu.VMEM((tc, NZ), F32),
            pltpu.VMEM((CONV_DIM // LANES, tc + 8, LANES), F32),
            pltpu.VMEM((CONV_DIM // LANES, tc, LANES), F32),
            pltpu.VMEM((tc, D_MODEL), BF16),
        ],
        compiler_params=pltpu.CompilerParams(
            dimension_semantics=("parallel", "arbitrary"), vmem_limit_bytes=VMEM_LIMIT),
        name="mixer_prompt",
    )(x, nw, win, wout, cos, sin, *extras)


SEQ_S = 4
NSEQ = C // SEQ_S


def _inproj_kernel(x_ref, nw_ref, win_ref, z_ref):
    z_ref[...] = _proj(_rms_rows(x_ref[...], nw_ref[...]).astype(BF16), win_ref[...])


def _sample_inproj(x, l, nw, win, *, row0, n):
    return pl.pallas_call(
        _inproj_kernel,
        grid=(1,),
        in_specs=[pl.BlockSpec((n, D_MODEL), lambda i: (row0 // n, 0)), _layer_resident(nw, l),
                  _layer_resident(win, l)],
        out_specs=pl.BlockSpec((n, NZ), lambda i: (0, 0)),
        out_shape=jax.ShapeDtypeStruct((n, NZ), F32),
        compiler_params=pltpu.CompilerParams(
            dimension_semantics=("arbitrary",), vmem_limit_bytes=VMEM_LIMIT),
        name="sample_inproj",
    )(x, nw, win)


def _mixer_sample_kernel(*refs, nlev):
    n_in = 9 + len(_PARAM_NAMES) + len(_CONST_NAMES)
    (x_ref, zin_ref, wout_ref, cos_ref, sin_ref,
     sret_in, sgdn_in, sgla_in, cbuf_ref) = refs[:9]
    k = dict(zip(_PARAM_NAMES + _CONST_NAMES, refs[9:n_in]))
    o_ref, sret_out, sgdn_out, sgla_out, cout_ref = refs[n_in + 3:n_in + 8]

    x = x_ref[...]
    z = zin_ref[...]

    rowi = lax.broadcasted_iota(jnp.int32, (C, 1), 0)
    tpos = rowi % SEQ_S

    xc = z[:, Z_CONV:Z_CONV + CONV_DIM]
    cb = cbuf_ref[...]
    cw = k["conv_w"]
    acc = xc * cw[3:4, :]
    for i in range(CONV_W - 1):
        cur = pltpu.roll(xc, 3 - i, 0)
        old = cb if i == 0 else pltpu.roll(cb, C - i, 0)
        acc = acc + jnp.where(tpos + i >= 3, cur, old) * cw[i:i + 1, :]
    cout_ref[...] = pltpu.roll(xc, C - 1, 0)
    conv = _silu(acc)

    seq_of_row = rowi // SEQ_S
    seq_of_lane = lax.broadcasted_iota(jnp.int32, (1, C), 1) // SEQ_S

    o, qd, kd, v = _ret_intra(z, cos_ref[...], sin_ref[...], k)
    dch = k["dch"][...]
    kdt = kd.T
    accs = [jnp.zeros((C, RET_DV), F32) for _ in range(NH)]
    for n in range(NSEQ):
        rm = seq_of_row == n
        lm = seq_of_lane == n
        for hh in range(NH):
            sl = slice(hh * RET_DK, (hh + 1) * RET_DK)
            s = sret_in[n, hh]
            accs[hh] = jnp.where(rm, _mm(qd[:, sl], s), accs[hh])
            sret_out[n, hh] = s * dch[:, sl] + _mm(jnp.where(lm, kdt[sl], 0.0), v[:, sl])
    o = o + jnp.concatenate(accs, axis=1)
    y_r = _ret_out(o, z[:, Z_RG:Z_RG + 256], k)

    f = _gdn_front([conv[:, gi * LANES:(gi + 1) * LANES] for gi in range(CONV_DIM // LANES)],
                   z[:, Z_SM:Z_SM + LANES], k)
    g = _gdn_back(f, _gdn_inverse([f["ac"]], k, nlev)[0])
    rowi2 = lax.broadcasted_iota(jnp.int32, (2 * C, 1), 0)
    seq_of_row2 = (rowi2 % C) // SEQ_S
    lhs = [jnp.concatenate([g["solk"][hh * C:(hh + 1) * C], g["qst"][hh * C:(hh + 1) * C]], axis=0)
           for hh in range(NH)]
    accs = [jnp.zeros((2 * C, GDN_DV), F32) for _ in range(NH)]
    for n in range(NSEQ):
        rm2 = seq_of_row2 == n
        for hh in range(NH):
            accs[hh] = jnp.where(rm2, _mm(lhs[hh], sgdn_in[n, hh]), accs[hh])
    us = [g["solv"][hh * C:(hh + 1) * C] - accs[hh][:C] for hh in range(NH)]
    ost = (g["ebcol"] * jnp.concatenate([accs[hh][C:] for hh in range(NH)], axis=0)
           + _mm(g["qk"], jnp.concatenate(us, axis=0)))
    kdts = [g["kdec"][hh * C:(hh + 1) * C].T for hh in range(NH)]
    for n in range(NSEQ):
        lm = seq_of_lane == n
        for hh in range(NH):
            scale = g["eblast"][hh * C + n * SEQ_S:hh * C + n * SEQ_S + 1]
            sgdn_out[n, hh] = scale * sgdn_in[n, hh] + _mm(jnp.where(lm, kdts[hh], 0.0), us[hh])
    y_d = _gdn_out(jnp.concatenate([ost[hh * C:(hh + 1) * C] for hh in range(NH)], axis=1),
                   z[:, Z_DG:Z_DG + 512], k)

    o, qe, ke, v, b, blast = _gla_intra(z, k, nlev)
    ket = ke.T
    hi, mid = _split2(blast * (1.0 / SEQ_S))
    blt2 = lax.dot_general(jnp.concatenate([hi, mid], axis=1), k["seqsel"][...], TN, preferred_element_type=F32)
    eblt = jnp.exp(blt2[:NH * GLA_DK] + blt2[NH * GLA_DK:])
    accs = [jnp.zeros((C, GLA_DV), F32) for _ in range(NH)]
    for n in range(NSEQ):
        rm = seq_of_row == n
        lm = seq_of_lane == n
        for hh in range(NH):
            sk = slice(hh * GLA_DK, (hh + 1) * GLA_DK)
            sv = slice(hh * GLA_DV, (hh + 1) * GLA_DV)
            s = sgla_in[n, hh]
            accs[hh] = jnp.where(rm, _mm(qe[:, sk], s), accs[hh])
            sgla_out[n, hh] = s * eblt[sk, n:n + 1] + _mm(jnp.where(lm, ket[sk], 0.0), v[:, sv])
    o = o + jnp.concatenate(accs, axis=1)
    y_l = _gla_out(o, z[:, Z_LG:Z_LG + 256], k)

    y = jnp.concatenate([y_r, y_d, y_l], axis=1).astype(BF16)
    o_ref[...] = x + jnp.dot(y, wout_ref[...].astype(BF16), preferred_element_type=F32)


def _mixer_sample(x, l, nw, win, wout, cos, sin, sret, sgdn, sgla, cbuf, acc_ret, acc_gdn, acc_gla,
                  params, consts, *, row0):
    nb = sret.shape[1]
    n = nb * SEQ_S
    nlev = len(_levels(SEQ_S))
    blk0 = row0 // C

    def st_spec(dk, dv):
        return pl.BlockSpec((None, NSEQ, NH, dk, dv), lambda i: (l, i, 0, 0, 0))

    st_specs = [st_spec(RET_DK, RET_DV), st_spec(GDN_DK, GDN_DV), st_spec(GLA_DK, GLA_DV)]
    any_spec = pl.BlockSpec(memory_space=pl.ANY)
    z = _sample_inproj(x, l, nw, win, row0=row0, n=n)
    in_specs = [
        pl.BlockSpec((C, D_MODEL), lambda i: (blk0 + i, 0)),
        pl.BlockSpec((C, NZ), lambda i: (i, 0)),
        _layer_resident(wout, l), _resident(cos), _resident(sin),
    ] + st_specs + [pl.BlockSpec((C, CONV_DIM), lambda i: (i, 0))] + [
        _layer_resident(params[nm], l) for nm in _PARAM_NAMES] + [
        _resident(consts[nm]) for nm in _CONST_NAMES] + [any_spec] * 3
    extras = [params[nm] for nm in _PARAM_NAMES] + [consts[nm] for nm in _CONST_NAMES]
    n_in = len(in_specs)
    out_shape = (
        jax.ShapeDtypeStruct(x.shape, F32),
        jax.ShapeDtypeStruct(acc_ret.shape, F32),
        jax.ShapeDtypeStruct(acc_gdn.shape, F32),
        jax.ShapeDtypeStruct(acc_gla.shape, F32),
        jax.ShapeDtypeStruct((n, CONV_DIM), F32),
    )
    out_specs = (pl.BlockSpec((C, D_MODEL), lambda i: (blk0 + i, 0)),) + tuple(st_specs) + (
        pl.BlockSpec((C, CONV_DIM), lambda i: (i, 0)),)
    return pl.pallas_call(
        functools.partial(_mixer_sample_kernel, nlev=nlev),
        grid=(n // C,),
        in_specs=in_specs,
        out_specs=out_specs,
        out_shape=out_shape,
        input_output_aliases={0: 0, n_in - 3: 1, n_in - 2: 2, n_in - 1: 3},
        compiler_params=pltpu.CompilerParams(
            dimension_semantics=("parallel",), vmem_limit_bytes=VMEM_LIMIT),
        name="mixer_sample",
    )(x, z, wout, cos, sin, sret, sgdn, sgla, cbuf, *extras, acc_ret, acc_gdn, acc_gla)


def _permute_w_in(w):
    wt = jnp.swapaxes(w, 1, 2)
    pad = jnp.zeros((w.shape[0], NZ - Z_SM - 24, w.shape[1]), w.dtype)
    return jnp.concatenate(
        [wt[:, 0:2560], wt[:, 2568:3080], wt[:, 3080:3592], wt[:, 3608:3864], wt[:, 2560:2568],
         wt[:, 3592:3608], pad], axis=1).astype(BF16)


def _rows(v, width=None):
    v = v.astype(F32)[:, None, :]
    if width is not None and v.shape[-1] < width:
        v = jnp.pad(v, ((0, 0), (0, 0), (0, width - v.shape[-1])))
    return v


def kernel(x_prompt, x_sample, state_ret, state_gdn, state_gdn_conv, state_gla, norm_ffn1, ffn1_w1, ffn1_w3, ffn1_w2, norm_mix, w_in, ret_gn_w, ret_gn_b, gdn_conv_w, gdn_A_log, gdn_dt_bias, gdn_norm_w, gla_gate_w, gla_gate_b, gla_norm_w, w_out, norm_ffn2, ffn2_w1, ffn2_w3, ffn2_w2, norm_final):
    bp, tp, _ = x_prompt.shape
    bs, ts, _ = x_sample.shape
    n_prompt, n_sample = bp * tp, bs * ts
    assert ts == SEQ_S and tp % MIX_TC == 0 and n_sample % C == 0 and n_prompt % C == 0

    consts_p = dict(_chunk_consts(C), **_ret_consts(C))
    consts_s = dict(_chunk_consts(SEQ_S), **_ret_consts(SEQ_S))
    cos_p, sin_p = _rope_tables(jnp.arange(tp, dtype=jnp.int32))
    cos_s, sin_s = _rope_tables(PAST_LEN + (jnp.arange(C, dtype=jnp.int32) % SEQ_S))

    params = dict(
        alog=_rows(gdn_A_log, LANES), dtb=_rows(gdn_dt_bias, LANES),
        gwp=jnp.pad(gla_gate_w, ((0, 0), (SM_LLR, LANES - SM_LLR - GLA_RANK), (0, 0))).astype(BF16),
        gla_gate_b=_rows(gla_gate_b), ret_gn_w=_rows(ret_gn_w), ret_gn_b=_rows(ret_gn_b),
        gdn_norm_w=_rows(gdn_norm_w), gla_norm_w=_rows(jnp.tile(gla_norm_w, (1, NH))),
        conv_w=gdn_conv_w.astype(F32),
    )
    win = _permute_w_in(w_in)
    wout = w_out.astype(F32)
    f1 = (_rows(norm_ffn1), ffn1_w1.astype(F32), ffn1_w3.astype(F32), ffn1_w2.astype(F32))
    f2 = (_rows(norm_ffn2), ffn2_w1.astype(F32), ffn2_w3.astype(F32), ffn2_w2.astype(F32))
    nw = _rows(norm_mix)
    fw = norm_final.astype(F32)[None, :]
    sret_in, sgdn_in, sgla_in = state_ret.astype(F32), state_gdn.astype(F32), state_gla.astype(F32)
    cbuf = jnp.pad(state_gdn_conv.astype(F32), ((0, 0), (0, 0), (0, 1), (0, 0))).reshape(DEPTH, n_sample, CONV_DIM)
    acc_ret, acc_gdn, acc_gla = (lax.empty(s.shape, F32) for s in (sret_in, sgdn_in, sgla_in))

    ffn = functools.partial(_ffn, n_prompt=n_prompt, n_sample=n_sample)
    xs = (x_prompt.reshape(n_prompt, D_MODEL), x_sample.reshape(n_sample, D_MODEL))
    outs_p = [[] for _ in range(4)]
    conv_s = []
    for l in range(DEPTH):
        x = ffn(xs, l, *f1, fw, first=(l == 0), last=False)
        x, sret, sgdn, sgla, conv = _mixer_prompt(x, l, nw, win, wout, cos_p, sin_p, params, consts_p,
                                                  bsz=bp, tlen=tp)
        outs_p[0].append(jnp.stack([sret[:, h * 64:(h + 1) * 64, h * 64:(h + 1) * 64] for h in range(NH)], axis=1))
        outs_p[1].append(sgdn)
        outs_p[2].append(conv)
        outs_p[3].append(jnp.stack([sgla[:, h * 32:(h + 1) * 32, h * 64:(h + 1) * 64] for h in range(NH)], axis=1))
        x, acc_ret, acc_gdn, acc_gla, cout = _mixer_sample(
            x, l, nw, win, wout, cos_s, sin_s, sret_in, sgdn_in, sgla_in, cbuf[l], acc_ret, acc_gdn, acc_gla,
            params, consts_s, row0=n_prompt)
        conv_s.append(cout.reshape(bs, ts, CONV_DIM)[:, :CONV_W - 1])
        xs = (ffn((x,), l, *f2, fw, first=False, last=(l == DEPTH - 1)),)

    y_prompt, y_sample = xs[0]
    dts = (state_ret.dtype, state_gdn.dtype, state_gdn_conv.dtype, state_gla.dtype)
    sp = [jnp.stack(o).astype(d) for o, d in zip(outs_p, dts)]
    return (y_prompt.reshape(bp, tp, D_MODEL), y_sample.reshape(bs, ts, D_MODEL), sp[0], sp[1], sp[2], sp[3],
            acc_ret.astype(dts[0]), acc_gdn.astype(dts[1]), jnp.stack(conv_s).astype(dts[2]), acc_gla.astype(dts[3]))
```
